```python
import jax, jax.numpy as jnp
from jax import lax
import numpy as np

D_MODEL = 2048
BATCH = 4
SEQ = 4096
DEPTH = 4
DEC_BATCH = 16
DEC_SEQ = 16
PAST_LEN = 2048

CHUNK = 64
N_A = DEPTH // 2
N_B = DEPTH - N_A
EXPAND_A = 2
E_A = EXPAND_A * D_MODEL
POOL_WINDOWS = (2, 4, 8, 16)
N_POOL_GROUPS = len(POOL_WINDOWS)
G_A = E_A // N_POOL_GROUPS
POOL_PAD = max(POOL_WINDOWS) - 1
N_HEADS = 16
HEAD_DIM = D_MODEL // N_HEADS
W_B = N_HEADS * HEAD_DIM
Q_BLOCK = 128
EPS = 1e-6
F_BIAS_MEAN = 3.0

kernel_name = 'yoco_pool_fox_streaming_step'


def rmsnorm(x, g):
    xf = x.astype(jnp.float32)
    r = lax.rsqrt(jnp.mean(xf * xf, axis=-1, keepdims=True) + EPS)
    return (xf * r).astype(x.dtype) * g


def pool_mix(u, prev, pos0, w_grp, scale):
    B, T, _ = u.shape
    full = jnp.concatenate([prev.astype(u.dtype), u], axis=1)
    cs = jnp.cumsum(full.astype(jnp.float32), axis=1)
    cs = jnp.pad(cs, ((0, 0), (1, 0), (0, 0)))
    end = cs[:, POOL_PAD + 1:]
    uf = u.astype(jnp.float32)
    pos = pos0 + jnp.arange(T)
    outs = []
    for g, w in enumerate(POOL_WINDOWS):
        sl = slice(g * G_A, (g + 1) * G_A)
        start = lax.slice_in_dim(cs, POOL_PAD + 1 - w, POOL_PAD + 1 - w + T, axis=1)[..., sl]
        cnt = jnp.minimum(pos + 1, w).astype(jnp.float32)
        d = (end[..., sl] - start) / cnt[None, :, None] - uf[..., sl]
        outs.append(jnp.einsum('btc,cd->btd', d, w_grp[g].astype(jnp.float32)))
    y = jnp.concatenate(outs, axis=-1) * scale.astype(jnp.float32)
    return y.astype(u.dtype), full[:, -POOL_PAD:]


def fox_attention(q, k, v, Fq, Fk, q_pos, k_pos):
    B, T, H, Dh = q.shape
    scale = HEAD_DIM ** -0.5
    Fk_t = jnp.transpose(Fk, (0, 2, 1))

    def block(args):
        qb, Fqb, pb = args
        s = jnp.einsum('bqhd,bkhd->bhqk', qb, k, preferred_element_type=jnp.float32) * scale
        s = s + jnp.transpose(Fqb, (0, 2, 1))[..., None] - Fk_t[:, :, None, :]
        mask = k_pos[None, :] <= pb[:, None]
        s = jnp.where(mask[None, None], s, -jnp.inf)
        p = jax.nn.softmax(s, axis=-1)
        o = jnp.einsum('bhqk,bkhd->bqhd', p.astype(v.dtype), v, preferred_element_type=jnp.float32)
        return o.astype(q.dtype)

    if T > Q_BLOCK and T % Q_BLOCK == 0:
        nb = T // Q_BLOCK
        qb = jnp.moveaxis(q.reshape(B, nb, Q_BLOCK, H, Dh), 1, 0)
        Fqb = jnp.moveaxis(Fq.reshape(B, nb, Q_BLOCK, H), 1, 0)
        pb = q_pos.reshape(nb, Q_BLOCK)
        o = lax.map(block, (qb, Fqb, pb))
        return jnp.moveaxis(o, 0, 1).reshape(B, T, H, Dh)
    return block((q, Fq, q_pos))


def trunk(x, pos0, pool_prev, past_k, past_v, past_logf, norm_a, w_in_a, w_grp_a, scale_a, w_out_a,
          norm_kv, w_kv, b_f, norm_b, w_in_b, w_out_b, norm_f):
    B, T, _ = x.shape
    new_pool = []
    for l in range(N_A):
        h = rmsnorm(x, norm_a[l])
        z = h @ w_in_a[l]
        u, gate = z[..., :E_A], z[..., E_A:]
        y, st = pool_mix(u, pool_prev[l], pos0, w_grp_a[l], scale_a[l])
        x = x + (y * jax.nn.silu(gate)) @ w_out_a[l]
        new_pool.append(st)
    hk = rmsnorm(x, norm_kv)
    kvf = hk @ w_kv
    k = kvf[..., :W_B].reshape(B, T, N_HEADS, HEAD_DIM)
    v = kvf[..., W_B:2 * W_B].reshape(B, T, N_HEADS, HEAD_DIM)
    logf = jax.nn.log_sigmoid(kvf[..., 2 * W_B:].astype(jnp.float32) + b_f.astype(jnp.float32))
    if past_k is None:
        k_all, v_all, logf_all = k, v, logf
    else:
        k_all = jnp.concatenate([past_k.astype(k.dtype), k], axis=1)
        v_all = jnp.concatenate([past_v.astype(v.dtype), v], axis=1)
        logf_all = jnp.concatenate([past_logf.astype(jnp.float32), logf], axis=1)
    F_all = jnp.cumsum(logf_all, axis=1)
    P = k_all.shape[1] - T
    q_pos = pos0 + jnp.arange(T)
    k_pos = jnp.arange(P + T)
    Fq = F_all[:, P:]
    for l in range(N_B):
        h = rmsnorm(x, norm_b[l])
        z = h @ w_in_b[l]
        q = z[..., :W_B].reshape(B, T, N_HEADS, HEAD_DIM)
        gate = z[..., W_B:]
        o = fox_attention(q, k_all, v_all, Fq, F_all, q_pos, k_pos).reshape(B, T, W_B)
        x = x + (o * jax.nn.silu(gate)) @ w_out_b[l]
    return rmsnorm(x, norm_f), k, v, logf, jnp.stack(new_pool)


def setup_inputs(seed: int = 0) -> dict:
    key = jax.random.key(seed)
    ks = jax.random.split(key, 20)
    f32 = jnp.float32
    nrm = lambda k, s: jax.random.normal(k, s, f32)
    return {
        'x_prompt': nrm(ks[0], (BATCH, SEQ, D_MODEL)),
        'x_sample': nrm(ks[1], (DEC_BATCH, DEC_SEQ, D_MODEL)),
        'cache_k': nrm(ks[2], (DEC_BATCH, PAST_LEN, N_HEADS, HEAD_DIM)),
        'cache_v': nrm(ks[3], (DEC_BATCH, PAST_LEN, N_HEADS, HEAD_DIM)),
        'cache_logf': jax.nn.log_sigmoid(F_BIAS_MEAN + nrm(ks[4], (DEC_BATCH, PAST_LEN, N_HEADS))),
        'state_pool': nrm(ks[5], (N_A, DEC_BATCH, POOL_PAD, E_A)),
        'norm_a': 1.0 + 0.02 * nrm(ks[6], (N_A, D_MODEL)),
        'w_in_a': nrm(ks[7], (N_A, D_MODEL, 2 * E_A)) * D_MODEL ** -0.5,
        'w_grp_a': nrm(ks[8], (N_A, N_POOL_GROUPS, G_A, G_A)) * G_A ** -0.5,
        'scale_a': 1.0 + 0.02 * nrm(ks[9], (N_A, E_A)),
        'w_out_a': nrm(ks[10], (N_A, E_A, D_MODEL)) * E_A ** -0.5,
        'norm_kv': 1.0 + 0.02 * nrm(ks[11], (D_MODEL,)),
        'w_kv': nrm(ks[12], (D_MODEL, 2 * W_B + N_HEADS)) * D_MODEL ** -0.5,
        'b_f': F_BIAS_MEAN + 0.1 * nrm(ks[13], (N_HEADS,)),
        'norm_b': 1.0 + 0.02 * nrm(ks[14], (N_B, D_MODEL)),
        'w_in_b': nrm(ks[15], (N_B, D_MODEL, 2 * W_B)) * D_MODEL ** -0.5,
        'w_out_b': nrm(ks[16], (N_B, W_B, D_MODEL)) * W_B ** -0.5,
        'norm_f': 1.0 + 0.02 * nrm(ks[17], (D_MODEL,)),
    }


def reference(x_prompt, x_sample, cache_k, cache_v, cache_logf, state_pool, norm_a, w_in_a, w_grp_a,
              scale_a, w_out_a, norm_kv, w_kv, b_f, norm_b, w_in_b, w_out_b, norm_f):
    pool_zero = jnp.zeros((N_A, x_prompt.shape[0], POOL_PAD, E_A), x_prompt.dtype)
    y_prompt, k_prompt, v_prompt, logf_prompt, pool_prompt = trunk(
        x_prompt, 0, pool_zero, None, None, None, norm_a, w_in_a, w_grp_a, scale_a, w_out_a,
        norm_kv, w_kv, b_f, norm_b, w_in_b, w_out_b, norm_f)
    y_sample, k_sample, v_sample, logf_sample, pool_sample = trunk(
        x_sample, cache_k.shape[1], state_pool, cache_k, cache_v, cache_logf, norm_a, w_in_a, w_grp_a,
        scale_a, w_out_a, norm_kv, w_kv, b_f, norm_b, w_in_b, w_out_b, norm_f)
    return (y_prompt, y_sample, k_prompt, v_prompt, logf_prompt, pool_prompt,
            k_sample, v_sample, logf_sample, pool_sample)
```

```python
import functools

import jax
import jax.numpy as jnp
from jax import lax
from jax.experimental import pallas as pl
from jax.experimental.pallas import tpu as pltpu

F32 = jnp.float32
BF16 = jnp.bfloat16

EPS = 1e-6
N_HEADS = 16
HEAD_DIM = 128
W_B = N_HEADS * HEAD_DIM
POOL_WINDOWS = (2, 4, 8, 16)
POOL_PAD = max(POOL_WINDOWS) - 1
HALO = 16
NEG = -1e30
LANES = 128
VMEM_LIMIT = 56 * 1024 * 1024


def _params(*sem):
    return pltpu.CompilerParams(dimension_semantics=sem, vmem_limit_bytes=VMEM_LIMIT)


def _silu(g):
    return g * (1.0 / (1.0 + jnp.exp(-g)))


def _rms_rows(xf, g):
    r = lax.rsqrt(jnp.mean(xf * xf, axis=-1, keepdims=True) + EPS)
    return (xf * r) * g


def _norm_proj_kernel(x_ref, g_ref, w_ref, *refs, n_scaled, scale, dual):
    h_ref = refs[-1]
    j = pl.program_id(1)

    @pl.when(j == 0)
    def _():
        h_ref[...] = _rms_rows(x_ref[...], g_ref[...]).astype(BF16)

    acc = jnp.dot(h_ref[...], w_ref[...], preferred_element_type=F32)
    if n_scaled:
        acc = acc * jnp.where(j < n_scaled, scale, 1.0)
    if dual:
        refs[0][...] = acc
        refs[1][...] = acc.astype(BF16)
    else:
        refs[0][...] = acc.astype(refs[0].dtype)


def _norm_proj(x, g, w, *, tm, tn, dual=False, n_scaled=0, scale=1.0, name):
    M, D = x.shape
    N = w.shape[1]
    o_spec = pl.BlockSpec((tm, tn), lambda i, j: (i, j))
    if dual:
        out_shape = (jax.ShapeDtypeStruct((M, N), F32), jax.ShapeDtypeStruct((M, N), BF16))
        out_specs = (o_spec, o_spec)
    else:
        out_shape = jax.ShapeDtypeStruct((M, N), BF16)
        out_specs = o_spec
    return pl.pallas_call(
        functools.partial(_norm_proj_kernel, n_scaled=n_scaled, scale=scale, dual=dual),
        out_shape=out_shape,
        grid=(M // tm, N // tn),
        in_specs=[
            pl.BlockSpec((tm, D), lambda i, j: (i, 0)),
            pl.BlockSpec((1, D), lambda i, j: (0, 0)),
            pl.BlockSpec((D, tn), lambda i, j: (0, j)),
        ],
        out_specs=out_specs,
        scratch_shapes=[pltpu.VMEM((tm, D), BF16)],
        compiler_params=_params("parallel", "arbitrary"),
        name=name,
    )(x, g.reshape(1, D), w)


def _proj_res_kernel(a_ref, w_ref, x_ref, o_ref):
    o_ref[...] = x_ref[...] + jnp.dot(a_ref[...], w_ref[...], preferred_element_type=F32)


def _proj_res(a, w, x, *, tm, tn, name):
    M, K = a.shape
    N = w.shape[1]
    return pl.pallas_call(
        _proj_res_kernel,
        out_shape=jax.ShapeDtypeStruct((M, N), F32),
        grid=(N // tn, M // tm),
        in_specs=[
            pl.BlockSpec((tm, K), lambda j, i: (i, 0)),
            pl.BlockSpec((K, tn), lambda j, i: (0, j)),
            pl.BlockSpec((tm, tn), lambda j, i: (i, j)),
        ],
        out_specs=pl.BlockSpec((tm, tn), lambda j, i: (i, j)),
        compiler_params=_params("parallel", "parallel"),
        name=name,
    )(a, w, x)


def _window_sum(full, w):
    s = full
    k = 1
    while k < w:
        s = s + pltpu.roll(s, k, 0)
        k *= 2
    return s


def _pool_finish(s, uf, inv_cnt, gate, wg, sc):
    d = s * inv_cnt - uf
    y = jnp.dot(d.astype(BF16), wg, preferred_element_type=F32) * sc
    return (y * _silu(gate)).astype(BF16)


def _pool_prompt_kernel(u_ref, halo_ref, gate_ref, wg_ref, sc_ref, o_ref, *, tm, tiles_per_seq, pos0):
    G = wg_ref.shape[-1]
    ti = pl.program_id(0) % tiles_per_seq
    t = pos0 + ti * tm + lax.broadcasted_iota(jnp.int32, (tm, 1), 0)
    for g, w in enumerate(POOL_WINDOWS):
        c = slice(g * G, (g + 1) * G)
        uf = u_ref[:, c].astype(F32)
        hf = jnp.where(ti == 0, 0.0, halo_ref[:, c].astype(F32))
        s = _window_sum(jnp.concatenate([hf, uf], axis=0), w)[HALO:, :]
        inv_cnt = 1.0 / jnp.minimum(t + 1, w).astype(F32)
        o_ref[:, c] = _pool_finish(s, uf, inv_cnt, gate_ref[:, c].astype(F32), wg_ref[g], sc_ref[:, c])


def _pool_prompt(z, wg, sc, *, T, tm, name):
    M = z.shape[0]
    E = z.shape[1] // 2
    hb = tm // HALO
    return pl.pallas_call(
        functools.partial(_pool_prompt_kernel, tm=tm, tiles_per_seq=T // tm, pos0=0),
        out_shape=jax.ShapeDtypeStruct((M, E), BF16),
        grid=(M // tm,),
        in_specs=[
            pl.BlockSpec((tm, E), lambda i: (i, 0)),
            pl.BlockSpec((HALO, E), lambda i: (jnp.maximum(i * hb - 1, 0), 0)),
            pl.BlockSpec((tm, E), lambda i: (i, 1)),
            pl.BlockSpec(wg.shape, lambda i: (0, 0, 0)),
            pl.BlockSpec((1, E), lambda i: (0, 0)),
        ],
        out_specs=pl.BlockSpec((tm, E), lambda i: (i, 0)),
        compiler_params=_params("parallel"),
        name=name,
    )(z, z, z, wg, sc.reshape(1, E))


def _pool_sample_kernel(full_ref, gate_ref, wg_ref, sc_ref, o_ref, *, nseq, T, pos0):
    G = wg_ref.shape[-1]
    seg = HALO + T
    t = pos0 + lax.broadcasted_iota(jnp.int32, (nseq * T, 1), 0) % T

    def tail(a):
        return a.reshape(nseq, seg, G)[:, HALO:, :].reshape(nseq * T, G)

    for g, w in enumerate(POOL_WINDOWS):
        c = slice(g * G, (g + 1) * G)
        full = full_ref[:, c]
        inv_cnt = 1.0 / jnp.minimum(t + 1, w).astype(F32)
        o_ref[:, c] = _pool_finish(tail(_window_sum(full, w)), tail(full), inv_cnt,
                                   gate_ref[:, c].astype(F32), wg_ref[g], sc_ref[:, c])


def _pool_sample(full, z, wg, sc, *, nseq, T, pos0, name):
    E = full.shape[1]
    return pl.pallas_call(
        functools.partial(_pool_sample_kernel, nseq=nseq, T=T, pos0=pos0),
        out_shape=jax.ShapeDtypeStruct((nseq * T, E), BF16),
        grid=(1,),
        in_specs=[
            pl.BlockSpec(full.shape, lambda i: (0, 0)),
            pl.BlockSpec((nseq * T, E), lambda i: (0, 1)),
            pl.BlockSpec(wg.shape, lambda i: (0, 0, 0)),
            pl.BlockSpec((1, E), lambda i: (0, 0)),
        ],
        out_specs=pl.BlockSpec((nseq * T, E), lambda i: (0, 0)),
        compiler_params=_params("arbitrary"),
        name=name,
    )(full, z, wg, sc.reshape(1, E))


def _logf_kernel(x_ref, g_ref, w_ref, b_ref, o_ref):
    h = _rms_rows(x_ref[...], g_ref[...]).astype(BF16)
    a = jnp.dot(h, w_ref[...], preferred_element_type=F32) + b_ref[...]
    o_ref[...] = jnp.minimum(a, 0.0) - jnp.log1p(jnp.exp(-jnp.abs(a)))


def _logf(x, g, wf, bf, *, tm, name):
    M, D = x.shape
    H = wf.shape[1]
    return pl.pallas_call(
        _logf_kernel,
        out_shape=jax.ShapeDtypeStruct((M, H), F32),
        grid=(M // tm,),
        in_specs=[
            pl.BlockSpec((tm, D), lambda i: (i, 0)),
            pl.BlockSpec((1, D), lambda i: (0, 0)),
            pl.BlockSpec((D, H), lambda i: (0, 0)),
            pl.BlockSpec((1, H), lambda i: (0, 0)),
        ],
        out_specs=pl.BlockSpec((tm, H), lambda i: (i, 0)),
        compiler_params=_params("parallel"),
        name=name,
    )(x, g.reshape(1, D), wf, bf.reshape(1, H))


def _cumsum_kernel(x_ref, o_ref):
    x = x_ref[...]
    n = x.shape[-1]
    lane = lax.broadcasted_iota(jnp.int32, x.shape, 1)
    k = 1
    while k < n:
        x = x + jnp.where(lane >= k, pltpu.roll(x, k, 1), 0.0)
        k *= 2
    o_ref[...] = x


def _cumsum_rows(x, *, name):
    R, n = x.shape
    return pl.pallas_call(
        _cumsum_kernel,
        out_shape=jax.ShapeDtypeStruct((R, n), F32),
        grid=(R // 8,),
        in_specs=[pl.BlockSpec((8, n), lambda i: (i, 0))],
        out_specs=pl.BlockSpec((8, n), lambda i: (i, 0)),
        compiler_params=_params("parallel"),
        name=name,
    )(x)


def _fox_prompt_kernel(q_ref, k_ref, v_ref, f_ref, gate_ref, o_ref, m_sc, l_sc, acc_sc, *, tq, tk):
    qi = pl.program_id(2)
    m_sc[...] = jnp.full(m_sc.shape, NEG, F32)
    l_sc[...] = jnp.zeros(l_sc.shape, F32)
    acc_sc[...] = jnp.zeros(acc_sc.shape, F32)
    q = q_ref[0]

    def step(j, masked):
        off = pl.multiple_of(j * tk, tk)
        kb = k_ref[0, pl.ds(off, tk), :]
        vb = v_ref[0, pl.ds(off, tk), :]
        s = lax.dot_general(q, kb, (((1,), (1,)), ((), ())), preferred_element_type=F32) - f_ref[0, j]
        if masked:
            rows = qi * tq + lax.broadcasted_iota(jnp.int32, (tq, tk), 0)
            cols = off + lax.broadcasted_iota(jnp.int32, (tq, tk), 1)
            s = jnp.where(cols <= rows, s, NEG)
        m_prev = m_sc[...]
        m_new = jnp.maximum(m_prev, jnp.max(s, axis=-1, keepdims=True))
        alpha = jnp.exp(m_prev - m_new)
        p = jnp.exp(s - m_new)
        l_sc[...] = alpha * l_sc[...] + jnp.sum(p, axis=-1, keepdims=True)
        acc_sc[...] = alpha * acc_sc[...] + jnp.dot(p.astype(BF16), vb, preferred_element_type=F32)
        m_sc[...] = m_new

    per = tq // tk
    n_full = qi * per

    def body(j, c):
        step(j, False)
        return c

    lax.fori_loop(0, n_full, body, 0)
    for d in range(per):
        step(n_full + d, True)
    o = acc_sc[...] * (1.0 / l_sc[...])
    o_ref[0] = (o * _silu(gate_ref[0].astype(F32))).astype(BF16)


def _fox_prompt(zb, k16, v16, Fk, *, B, T, tq, tk, name):
    nkb = T // tk
    return pl.pallas_call(
        functools.partial(_fox_prompt_kernel, tq=tq, tk=tk),
        out_shape=jax.ShapeDtypeStruct((B, T, W_B), BF16),
        grid=(B, N_HEADS, T // tq),
        in_specs=[
            pl.BlockSpec((1, tq, HEAD_DIM), lambda b, h, i: (b, i, h)),
            pl.BlockSpec((1, T, HEAD_DIM), lambda b, h, i: (b, 0, h)),
            pl.BlockSpec((1, T, HEAD_DIM), lambda b, h, i: (b, 0, h)),
            pl.BlockSpec((1, nkb, 1, tk), lambda b, h, i: (b * N_HEADS + h, 0, 0, 0)),
            pl.BlockSpec((1, tq, HEAD_DIM), lambda b, h, i: (b, i, N_HEADS + h)),
        ],
        out_specs=pl.BlockSpec((1, tq, HEAD_DIM), lambda b, h, i: (b, i, h)),
        scratch_shapes=[
            pltpu.VMEM((tq, 1), F32),
            pltpu.VMEM((tq, 1), F32),
            pltpu.VMEM((tq, HEAD_DIM), F32),
        ],
        compiler_params=_params("parallel", "parallel", "arbitrary"),
        name=name,
    )(zb, k16, v16, Fk, zb)


def _fox_decode_kernel(q_ref, kc_ref, vc_ref, kn_ref, vn_ref, f_ref, gate_ref, o_ref, *, P, T):
    q = q_ref[...]
    f = f_ref[0]
    nt = (((1,), (1,)), ((), ()))
    s1 = lax.dot_general(q, kc_ref[0].astype(BF16), nt, preferred_element_type=F32) - f[:, :P]
    s2 = lax.dot_general(q, kn_ref[...], nt, preferred_element_type=F32) - f[:, P:]
    rows = lax.broadcasted_iota(jnp.int32, (T, T), 0)
    cols = lax.broadcasted_iota(jnp.int32, (T, T), 1)
    s2 = jnp.where(cols <= rows, s2, NEG)
    m = jnp.maximum(jnp.max(s1, axis=-1, keepdims=True), jnp.max(s2, axis=-1, keepdims=True))
    p1 = jnp.exp(s1 - m)
    p2 = jnp.exp(s2 - m)
    l = jnp.sum(p1, axis=-1, keepdims=True) + jnp.sum(p2, axis=-1, keepdims=True)
    o = jnp.dot(p1.astype(BF16), vc_ref[0].astype(BF16), preferred_element_type=F32)
    o = o + jnp.dot(p2.astype(BF16), vn_ref[...], preferred_element_type=F32)
    o_ref[...] = (o * (1.0 / l) * _silu(gate_ref[...].astype(F32))).astype(BF16)


def _fox_decode(zb, k16, v16, cache_k, cache_v, Fk, *, B, T, P, name):
    q_spec = pl.BlockSpec((T, HEAD_DIM), lambda b, h: (b, h))
    c_spec = pl.BlockSpec((1, P, HEAD_DIM), lambda b, h: (b, 0, h))
    return pl.pallas_call(
        functools.partial(_fox_decode_kernel, P=P, T=T),
        out_shape=jax.ShapeDtypeStruct((B * T, W_B), BF16),
        grid=(B, N_HEADS),
        in_specs=[
            q_spec, c_spec, c_spec, q_spec, q_spec,
            pl.BlockSpec((1, 1, P + T), lambda b, h: (b * N_HEADS + h, 0, 0)),
            pl.BlockSpec((T, HEAD_DIM), lambda b, h: (b, N_HEADS + h)),
        ],
        out_specs=q_spec,
        compiler_params=_params("parallel", "parallel"),
        name=name,
    )(zb, cache_k, cache_v, k16, v16, Fk, zb)


def _rmsnorm_kernel(x_ref, g_ref, o_ref):
    o_ref[...] = _rms_rows(x_ref[...], g_ref[...])


def _rmsnorm(x, g, *, tm, name):
    M, D = x.shape
    return pl.pallas_call(
        _rmsnorm_kernel,
        out_shape=jax.ShapeDtypeStruct((M, D), F32),
        grid=(M // tm,),
        in_specs=[pl.BlockSpec((tm, D), lambda i: (i, 0)), pl.BlockSpec((1, D), lambda i: (0, 0))],
        out_specs=pl.BlockSpec((tm, D), lambda i: (i, 0)),
        compiler_params=_params("parallel"),
        name=name,
    )(x, g.reshape(1, D))


def _trunk(x3, pos0, pool_prev, past, wts, tag):
    B, T, D = x3.shape
    M = B * T
    x = x3.reshape(M, D)
    prompt = pool_prev is None
    tm = 1024 if prompt else M
    tm_res = 512 if prompt else M
    E = wts["w_out_a"].shape[1]
    q_scale = HEAD_DIM ** -0.5

    new_pool = []
    for l in range(wts["w_in_a"].shape[0]):
        z = _norm_proj(x, wts["norm_a"][l], wts["w_in_a"][l], tm=tm, tn=1024, name=f"in_a{l}_{tag}")
        u3 = z[:, :E].reshape(B, T, E) if not prompt else None
        if prompt:
            t = _pool_prompt(z, wts["w_grp_a"][l], wts["scale_a"][l], T=T, tm=256, name=f"pool{l}_{tag}")
            new_pool.append(z.reshape(B, T, 2 * E)[:, T - POOL_PAD:, :E].astype(F32))
        else:
            hist = jnp.pad(pool_prev[l].astype(F32), ((0, 0), (HALO - POOL_PAD, 0), (0, 0)))
            full = jnp.concatenate([hist, u3.astype(F32)], axis=1)
            t = _pool_sample(full.reshape(B * (HALO + T), E), z, wts["w_grp_a"][l], wts["scale_a"][l],
                             nseq=B, T=T, pos0=pos0, name=f"pool{l}_{tag}")
            new_pool.append(full[:, HALO + T - POOL_PAD:, :])
        x = _proj_res(t, wts["w_out_a"][l], x, tm=tm_res, tn=1024, name=f"out_a{l}_{tag}")

    k32, k16 = _norm_proj(x, wts["norm_kv"], wts["w_k"], tm=tm, tn=1024, dual=True, name=f"k_{tag}")
    v32, v16 = _norm_proj(x, wts["norm_kv"], wts["w_v"], tm=tm, tn=1024, dual=True, name=f"v_{tag}")
    logf = _logf(x, wts["norm_kv"], wts["w_f"], wts["b_f"], tm=tm, name=f"logf_{tag}")

    logf3 = logf.reshape(B, T, N_HEADS)
    if prompt:
        tq = tk = 512
        lf = jnp.transpose(logf3, (0, 2, 1)).reshape(B * N_HEADS, T)
        Fk = _cumsum_rows(lf, name=f"cumsum_{tag}").reshape(B * N_HEADS, T // tk, 1, tk)
    else:
        past_k, past_v, past_logf = past
        P = past_k.shape[1]
        n = P + T
        n_pad = -(-n // LANES) * LANES
        lf = jnp.concatenate([past_logf.astype(F32), logf3], axis=1)
        lf = jnp.transpose(lf, (0, 2, 1)).reshape(B * N_HEADS, n)
        lf = jnp.pad(lf, ((0, 0), (0, n_pad - n)))
        Fk = _cumsum_rows(lf, name=f"cumsum_{tag}")[:, :n].reshape(B * N_HEADS, 1, n)
        ck = past_k.reshape(B, P, W_B)
        cv = past_v.reshape(B, P, W_B)

    for l in range(wts["w_in_b"].shape[0]):
        zb = _norm_proj(x, wts["norm_b"][l], wts["w_in_b"][l], tm=tm, tn=1024,
                        n_scaled=W_B // 1024, scale=q_scale, name=f"in_b{l}_{tag}")
        if prompt:
            og = _fox_prompt(zb.reshape(B, T, 2 * W_B), k16.reshape(B, T, W_B), v16.reshape(B, T, W_B), Fk,
                             B=B, T=T, tq=tq, tk=tk, name=f"attn{l}_{tag}").reshape(M, W_B)
        else:
            og = _fox_decode(zb, k16, v16, ck, cv, Fk, B=B, T=T, P=P, name=f"attn{l}_{tag}")
        x = _proj_res(og, wts["w_out_b"][l], x, tm=tm_res, tn=1024, name=f"out_b{l}_{tag}")

    y = _rmsnorm(x, wts["norm_f"], tm=min(tm, 512), name=f"final_{tag}")
    return (y.reshape(B, T, D), k32.reshape(B, T, N_HEADS, HEAD_DIM), v32.reshape(B, T, N_HEADS, HEAD_DIM),
            logf3, jnp.stack(new_pool))


def kernel(x_prompt, x_sample, cache_k, cache_v, cache_logf, state_pool, norm_a, w_in_a, w_grp_a, scale_a,
           w_out_a, norm_kv, w_kv, b_f, norm_b, w_in_b, w_out_b, norm_f):
    wts = dict(
        norm_a=norm_a, w_in_a=w_in_a.astype(BF16), w_grp_a=w_grp_a.astype(BF16), scale_a=scale_a,
        w_out_a=w_out_a.astype(BF16), norm_kv=norm_kv,
        w_k=w_kv[:, :W_B].astype(BF16), w_v=w_kv[:, W_B:2 * W_B].astype(BF16),
        w_f=w_kv[:, 2 * W_B:].astype(BF16), b_f=b_f,
        norm_b=norm_b, w_in_b=w_in_b.astype(BF16), w_out_b=w_out_b.astype(BF16), norm_f=norm_f,
    )
    y_p, k_p, v_p, lf_p, pool_p = _trunk(x_prompt, 0, None, None, wts, "p")
    y_s, k_s, v_s, lf_s, pool_s = _trunk(x_sample, cache_k.shape[1], state_pool,
                                         (cache_k, cache_v, cache_logf), wts, "s")
    return (y_p, y_s, k_p, v_p, lf_p, pool_p, k_s, v_s, lf_s, pool_s)
```

```python
import functools

import jax
import jax.numpy as jnp
from jax import lax
from jax.experimental import pallas as pl
from jax.experimental.pallas import tpu as pltpu

F32 = jnp.float32
BF16 = jnp.bfloat16

EPS = 1e-6
N_HEADS = 16
HEAD_DIM = 128
W_B = N_HEADS * HEAD_DIM
POOL_WINDOWS = (2, 4, 8, 16)
POOL_PAD = max(POOL_WINDOWS) - 1
HALO = 16
NEG = -1e30
LANES = 128
VMEM_LIMIT = 56 * 1024 * 1024


def _params(*sem):
    return pltpu.CompilerParams(dimension_semantics=sem, vmem_limit_bytes=VMEM_LIMIT)


def _silu(g):
    return g * (1.0 / (1.0 + jnp.exp(-g)))


def _rms_rows(xf, g):
    r = lax.rsqrt(jnp.mean(xf * xf, axis=-1, keepdims=True) + EPS)
    return (xf * r) * g


def _norm_proj_kernel(x_ref, g_ref, w_ref, *refs, n_scaled, scale, dual):
    h_ref = refs[-1]
    j = pl.program_id(1)

    @pl.when(j == 0)
    def _():
        h_ref[...] = _rms_rows(x_ref[...], g_ref[...]).astype(BF16)

    acc = jnp.dot(h_ref[...], w_ref[...], preferred_element_type=F32)
    if n_scaled:
        acc = acc * jnp.where(j < n_scaled, scale, 1.0)
    if dual:
        refs[0][...] = acc
        refs[1][...] = acc.astype(BF16)
    else:
        refs[0][...] = acc.astype(refs[0].dtype)


def _norm_proj(x, g, w, *, tm, tn, dual=False, n_scaled=0, scale=1.0, name):
    M, D = x.shape
    N = w.shape[1]
    o_spec = pl.BlockSpec((tm, tn), lambda i, j: (i, j))
    if dual:
        out_shape = (jax.ShapeDtypeStruct((M, N), F32), jax.ShapeDtypeStruct((M, N), BF16))
        out_specs = (o_spec, o_spec)
    else:
        out_shape = jax.ShapeDtypeStruct((M, N), BF16)
        out_specs = o_spec
    return pl.pallas_call(
        functools.partial(_norm_proj_kernel, n_scaled=n_scaled, scale=scale, dual=dual),
        out_shape=out_shape,
        grid=(M // tm, N // tn),
        in_specs=[
            pl.BlockSpec((tm, D), lambda i, j: (i, 0)),
            pl.BlockSpec((1, D), lambda i, j: (0, 0)),
            pl.BlockSpec((D, tn), lambda i, j: (0, j)),
        ],
        out_specs=out_specs,
        scratch_shapes=[pltpu.VMEM((tm, D), BF16)],
        compiler_params=_params("parallel", "arbitrary"),
        name=name,
    )(x, g.reshape(1, D), w)


def _proj_res_kernel(a_ref, w_ref, x_ref, o_ref):
    o_ref[...] = x_ref[...] + jnp.dot(a_ref[...], w_ref[...], preferred_element_type=F32)


def _proj_res(a, w, x, *, tm, tn, name):
    M, K = a.shape
    N = w.shape[1]
    return pl.pallas_call(
        _proj_res_kernel,
        out_shape=jax.ShapeDtypeStruct((M, N), F32),
        grid=(N // tn, M // tm),
        in_specs=[
            pl.BlockSpec((tm, K), lambda j, i: (i, 0)),
            pl.BlockSpec((K, tn), lambda j, i: (0, j)),
            pl.BlockSpec((tm, tn), lambda j, i: (i, j)),
        ],
        out_specs=pl.BlockSpec((tm, tn), lambda j, i: (i, j)),
        compiler_params=_params("parallel", "parallel"),
        name=name,
    )(a, w, x)


def _window_sum(full, w):
    s = full
    k = 1
    while k < w:
        s = s + pltpu.roll(s, k, 0)
        k *= 2
    return s


def _pool_finish(s, uf, inv_cnt, gate, wg, sc):
    d = s * inv_cnt - uf
    y = jnp.dot(d.astype(BF16), wg, preferred_element_type=F32) * sc
    return (y * _silu(gate)).astype(BF16)


def _pool_prompt_kernel(u_ref, halo_ref, gate_ref, wg_ref, sc_ref, o_ref, *, tm, tiles_per_seq, pos0):
    G = wg_ref.shape[-1]
    ti = pl.program_id(0) % tiles_per_seq
    t = pos0 + ti * tm + lax.broadcasted_iota(jnp.int32, (tm, 1), 0)
    for g, w in enumerate(POOL_WINDOWS):
        c = slice(g * G, (g + 1) * G)
        uf = u_ref[:, c].astype(F32)
        hf = jnp.where(ti == 0, 0.0, halo_ref[:, c].astype(F32))
        s = _window_sum(jnp.concatenate([hf, uf], axis=0), w)[HALO:, :]
        inv_cnt = 1.0 / jnp.minimum(t + 1, w).astype(F32)
        o_ref[:, c] = _pool_finish(s, uf, inv_cnt, gate_ref[:, c].astype(F32), wg_ref[g], sc_ref[:, c])


def _pool_prompt(z, wg, sc, *, T, tm, name):
    M = z.shape[0]
    E = z.shape[1] // 2
    hb = tm // HALO
    return pl.pallas_call(
        functools.partial(_pool_prompt_kernel, tm=tm, tiles_per_seq=T // tm, pos0=0),
        out_shape=jax.ShapeDtypeStruct((M, E), BF16),
        grid=(M // tm,),
        in_specs=[
            pl.BlockSpec((tm, E), lambda i: (i, 0)),
            pl.BlockSpec((HALO, E), lambda i: (jnp.maximum(i * hb - 1, 0), 0)),
            pl.BlockSpec((tm, E), lambda i: (i, 1)),
            pl.BlockSpec(wg.shape, lambda i: (0, 0, 0)),
            pl.BlockSpec((1, E), lambda i: (0, 0)),
        ],
        out_specs=pl.BlockSpec((tm, E), lambda i: (i, 0)),
        compiler_params=_params("parallel"),
        name=name,
    )(z, z, z, wg, sc.reshape(1, E))


def _pool_sample_kernel(full_ref, gate_ref, wg_ref, sc_ref, o_ref, *, nseq, T, pos0):
    G = wg_ref.shape[-1]
    seg = HALO + T
    t = pos0 + lax.broadcasted_iota(jnp.int32, (nseq * T, 1), 0) % T

    def tail(a):
        return a.reshape(nseq, seg, G)[:, HALO:, :].reshape(nseq * T, G)

    for g, w in enumerate(POOL_WINDOWS):
        c = slice(g * G, (g + 1) * G)
        full = full_ref[:, c]
        inv_cnt = 1.0 / jnp.minimum(t + 1, w).astype(F32)
        o_ref[:, c] = _pool_finish(tail(_window_sum(full, w)), tail(full), inv_cnt,
                                   gate_ref[:, c].astype(F32), wg_ref[g], sc_ref[:, c])


def _pool_sample(full, z, wg, sc, *, nseq, T, pos0, name):
    E = full.shape[1]
    return pl.pallas_call(
        functools.partial(_pool_sample_kernel, nseq=nseq, T=T, pos0=pos0),
        out_shape=jax.ShapeDtypeStruct((nseq * T, E), BF16),
        grid=(1,),
        in_specs=[
            pl.BlockSpec(full.shape, lambda i: (0, 0)),
            pl.BlockSpec((nseq * T, E), lambda i: (0, 1)),
            pl.BlockSpec(wg.shape, lambda i: (0, 0, 0)),
            pl.BlockSpec((1, E), lambda i: (0, 0)),
        ],
        out_specs=pl.BlockSpec((nseq * T, E), lambda i: (0, 0)),
        compiler_params=_params("arbitrary"),
        name=name,
    )(full, z, wg, sc.reshape(1, E))


def _kv_kernel(x_ref, g_ref, wk_ref, wv_ref, wf_ref, bf_ref, k32_ref, v32_ref, k16_ref, v16_ref, lf_ref, *, tn):
    h = _rms_rows(x_ref[...], g_ref[...]).astype(BF16)
    for w_ref, o32_ref, o16_ref in ((wk_ref, k32_ref, k16_ref), (wv_ref, v32_ref, v16_ref)):
        for c in range(w_ref.shape[1] // tn):
            cs = slice(c * tn, (c + 1) * tn)
            acc = jnp.dot(h, w_ref[:, cs], preferred_element_type=F32)
            o32_ref[:, cs] = acc
            o16_ref[:, cs] = acc.astype(BF16)
    a = jnp.dot(h, wf_ref[...], preferred_element_type=F32) + bf_ref[...]
    lf_ref[...] = jnp.minimum(a, 0.0) - jnp.log1p(jnp.exp(-jnp.abs(a)))


def _kv_proj(x, g, wk, wv, wf, bf, *, tm, name):
    M, D = x.shape
    N = wk.shape[1]
    L = wf.shape[1]
    resident = dict(pipeline_mode=pl.Buffered(1))
    row = lambda i: (i, 0)
    fixed = lambda i: (0, 0)
    return pl.pallas_call(
        functools.partial(_kv_kernel, tn=512),
        out_shape=(jax.ShapeDtypeStruct((M, N), F32), jax.ShapeDtypeStruct((M, N), F32),
                   jax.ShapeDtypeStruct((M, N), BF16), jax.ShapeDtypeStruct((M, N), BF16),
                   jax.ShapeDtypeStruct((M, L), F32)),
        grid=(M // tm,),
        in_specs=[
            pl.BlockSpec((tm, D), row),
            pl.BlockSpec((1, D), fixed),
            pl.BlockSpec((D, N), fixed, **resident),
            pl.BlockSpec((D, N), fixed, **resident),
            pl.BlockSpec((D, L), fixed, **resident),
            pl.BlockSpec((1, L), fixed),
        ],
        out_specs=(pl.BlockSpec((tm, N), row), pl.BlockSpec((tm, N), row), pl.BlockSpec((tm, N), row),
                   pl.BlockSpec((tm, N), row), pl.BlockSpec((tm, L), row)),
        compiler_params=_params("parallel"),
        name=name,
    )(x, g.reshape(1, D), wk, wv, wf, bf.reshape(1, L))


def _forget_cols_kernel(x_ref, o_ref):
    x = x_ref[0]
    n = x.shape[0]
    row = lax.broadcasted_iota(jnp.int32, x.shape, 0)
    lane = lax.broadcasted_iota(jnp.int32, x.shape, 1)
    k = 1
    while k < n:
        x = x + jnp.where(row >= k, pltpu.roll(x, k, 0), 0.0)
        k *= 2
    r1 = x - x.astype(BF16).astype(F32)
    r2 = r1 - r1.astype(BF16).astype(F32)
    piece = jnp.where(lane < N_HEADS, x, jnp.where(lane < 2 * N_HEADS, r1, r2))
    o_ref[0] = jnp.where(lane < 3 * N_HEADS, piece, 0.0).astype(BF16)


def _forget_cols(logf_rep, *, name):
    B, n, L = logf_rep.shape
    return pl.pallas_call(
        _forget_cols_kernel,
        out_shape=jax.ShapeDtypeStruct((B, n, L), BF16),
        grid=(B,),
        in_specs=[pl.BlockSpec((1, n, L), lambda b: (b, 0, 0))],
        out_specs=pl.BlockSpec((1, n, L), lambda b: (b, 0, 0)),
        compiler_params=_params("parallel"),
        name=name,
    )(logf_rep)


def _forget_query_cols(h, rows):
    lane = lax.broadcasted_iota(jnp.int32, (rows, LANES), 1)
    return jnp.where((lane % N_HEADS == h) & (lane < 3 * N_HEADS), -1.0, 0.0).astype(BF16)


def _fox_prompt_kernel(q_ref, k_ref, v_ref, fa_ref, gate_ref, o_ref, m_sc, l_sc, acc_sc, *, tq, tk):
    h = pl.program_id(1)
    qi = pl.program_id(2)
    m_sc[...] = jnp.full(m_sc.shape, NEG, F32)
    l_sc[...] = jnp.zeros(l_sc.shape, F32)
    acc_sc[...] = jnp.zeros(acc_sc.shape, F32)
    qa = jnp.concatenate([q_ref[0], _forget_query_cols(h, tq)], axis=1)

    def scores(j):
        off = pl.multiple_of(j * tk, tk)
        ka = jnp.concatenate([k_ref[0, pl.ds(off, tk), :], fa_ref[0, pl.ds(off, tk), :]], axis=1)
        return lax.dot_general(ka, qa, (((1,), (1,)), ((), ())), preferred_element_type=F32)

    def absorb(j, s, masked):
        off = pl.multiple_of(j * tk, tk)
        if masked:
            kpos = off + lax.broadcasted_iota(jnp.int32, (tk, tq), 0)
            qpos = qi * tq + lax.broadcasted_iota(jnp.int32, (tk, tq), 1)
            s = jnp.where(kpos <= qpos, s, NEG)
        m_prev = m_sc[...]
        m_new = jnp.maximum(m_prev, jnp.max(s, axis=0, keepdims=True))
        alpha = jnp.exp(m_prev - m_new)
        p = jnp.exp(s - m_new)
        l_sc[...] = alpha * l_sc[...] + jnp.sum(p, axis=0, keepdims=True)
        pv = lax.dot_general(v_ref[0, pl.ds(off, tk), :], p.astype(BF16), (((0,), (0,)), ((), ())),
                             preferred_element_type=F32)
        acc_sc[...] = alpha * acc_sc[...] + pv
        m_sc[...] = m_new

    last = qi

    def body(j, s):
        s_next = scores(j + 1)
        absorb(j, s, False)
        return s_next

    s_last = lax.fori_loop(0, last, body, scores(0))
    absorb(last, s_last, True)
    o = (acc_sc[...] * (1.0 / l_sc[...])).T
    o_ref[0] = (o * _silu(gate_ref[0].astype(F32))).astype(BF16)


def _fox_prompt(zb, k16, v16, fa, *, B, T, tq, tk, name):
    assert tq == tk and T % tq == 0
    return pl.pallas_call(
        functools.partial(_fox_prompt_kernel, tq=tq, tk=tk),
        out_shape=jax.ShapeDtypeStruct((B, T, W_B), BF16),
        grid=(B, N_HEADS, T // tq),
        in_specs=[
            pl.BlockSpec((1, tq, HEAD_DIM), lambda b, h, i: (b, i, h)),
            pl.BlockSpec((1, T, HEAD_DIM), lambda b, h, i: (b, 0, h)),
            pl.BlockSpec((1, T, HEAD_DIM), lambda b, h, i: (b, 0, h)),
            pl.BlockSpec((1, T, LANES), lambda b, h, i: (b, 0, 0)),
            pl.BlockSpec((1, tq, HEAD_DIM), lambda b, h, i: (b, i, N_HEADS + h)),
        ],
        out_specs=pl.BlockSpec((1, tq, HEAD_DIM), lambda b, h, i: (b, i, h)),
        scratch_shapes=[
            pltpu.VMEM((1, tq), F32),
            pltpu.VMEM((1, tq), F32),
            pltpu.VMEM((HEAD_DIM, tq), F32),
        ],
        compiler_params=_params("parallel", "parallel", "arbitrary"),
        name=name,
    )(zb, k16, v16, fa, zb)


def _fox_decode_kernel(z_ref, kc_ref, vc_ref, kn_ref, vn_ref, fa_ref, o_ref, m_sc, l_sc, acc_sc, *, P, T, pc):
    c = pl.program_id(1)
    nt = (((1,), (1,)), ((), ()))

    @pl.when(c == 0)
    def _():
        m_sc[...] = jnp.full(m_sc.shape, NEG, F32)
        l_sc[...] = jnp.zeros(l_sc.shape, F32)
        acc_sc[...] = jnp.zeros(acc_sc.shape, F32)

    def qa_of(h):
        return jnp.concatenate([z_ref[:, h * HEAD_DIM:(h + 1) * HEAD_DIM], _forget_query_cols(h, T)], axis=1)

    def absorb(keys, vals, fa_blk, mask):
        s = jnp.concatenate(
            [lax.dot_general(qa_of(h), jnp.concatenate([keys(h), fa_blk], axis=1), nt,
                             preferred_element_type=F32) for h in range(N_HEADS)], axis=0)
        if mask is not None:
            s = jnp.where(mask, s, NEG)
        m_prev = m_sc[...]
        m_new = jnp.maximum(m_prev, jnp.max(s, axis=-1, keepdims=True))
        alpha = jnp.exp(m_prev - m_new)
        p = jnp.exp(s - m_new)
        l_sc[...] = alpha * l_sc[...] + jnp.sum(p, axis=-1, keepdims=True)
        pb = p.astype(BF16)
        pv = jnp.concatenate(
            [jnp.dot(pb[h * T:(h + 1) * T, :], vals(h), preferred_element_type=F32) for h in range(N_HEADS)],
            axis=0)
        acc_sc[...] = alpha * acc_sc[...] + pv
        m_sc[...] = m_new

    absorb(lambda h: kc_ref[0, pl.ds(h, pc, stride=N_HEADS), :].astype(BF16),
           lambda h: vc_ref[0, pl.ds(h, pc, stride=N_HEADS), :].astype(BF16),
           fa_ref[0, pl.ds(pl.multiple_of(c * pc, pc), pc), :], None)

    @pl.when(c == pl.num_programs(1) - 1)
    def _():
        rows = lax.broadcasted_iota(jnp.int32, (N_HEADS * T, T), 0) % T
        cols = lax.broadcasted_iota(jnp.int32, (N_HEADS * T, T), 1)
        absorb(lambda h: kn_ref[:, h * HEAD_DIM:(h + 1) * HEAD_DIM],
               lambda h: vn_ref[:, h * HEAD_DIM:(h + 1) * HEAD_DIM],
               fa_ref[0, P:, :], cols <= rows)
        o = acc_sc[...] * (1.0 / l_sc[...])
        for h in range(N_HEADS):
            hc = slice(h * HEAD_DIM, (h + 1) * HEAD_DIM)
            gate = z_ref[:, W_B + h * HEAD_DIM:W_B + (h + 1) * HEAD_DIM].astype(F32)
            o_ref[:, hc] = (o[h * T:(h + 1) * T, :] * _silu(gate)).astype(BF16)


def _fox_decode(zb, k16, v16, cache_k, cache_v, fa, *, B, T, P, pc, name):
    c_spec = pl.BlockSpec((1, pc * N_HEADS, HEAD_DIM), lambda b, c: (b, c, 0))
    n_spec = pl.BlockSpec((T, W_B), lambda b, c: (b, 0))
    return pl.pallas_call(
        functools.partial(_fox_decode_kernel, P=P, T=T, pc=pc),
        out_shape=jax.ShapeDtypeStruct((B * T, W_B), BF16),
        grid=(B, P // pc),
        in_specs=[
            pl.BlockSpec((T, 2 * W_B), lambda b, c: (b, 0)),
            c_spec, c_spec, n_spec, n_spec,
            pl.BlockSpec((1, P + T, LANES), lambda b, c: (b, 0, 0)),
        ],
        out_specs=n_spec,
        scratch_shapes=[
            pltpu.VMEM((N_HEADS * T, 1), F32),
            pltpu.VMEM((N_HEADS * T, 1), F32),
            pltpu.VMEM((N_HEADS * T, HEAD_DIM), F32),
        ],
        compiler_params=_params("parallel", "arbitrary"),
        name=name,
    )(zb, cache_k, cache_v, k16, v16, fa)


def _rmsnorm_kernel(x_ref, g_ref, o_ref):
    o_ref[...] = _rms_rows(x_ref[...], g_ref[...])


def _rmsnorm(x, g, *, tm, name):
    M, D = x.shape
    return pl.pallas_call(
        _rmsnorm_kernel,
        out_shape=jax.ShapeDtypeStruct((M, D), F32),
        grid=(M // tm,),
        in_specs=[pl.BlockSpec((tm, D), lambda i: (i, 0)), pl.BlockSpec((1, D), lambda i: (0, 0))],
        out_specs=pl.BlockSpec((tm, D), lambda i: (i, 0)),
        compiler_params=_params("parallel"),
        name=name,
    )(x, g.reshape(1, D))


def _trunk(x3, pos0, pool_prev, past, wts, tag):
    B, T, D = x3.shape
    M = B * T
    x = x3.reshape(M, D)
    prompt = pool_prev is None
    tm = 1024 if prompt else M
    tm_res = 512 if prompt else M
    E = wts["w_out_a"].shape[1]
    q_scale = HEAD_DIM ** -0.5

    new_pool = []
    for l in range(wts["w_in_a"].shape[0]):
        z = _norm_proj(x, wts["norm_a"][l], wts["w_in_a"][l], tm=tm, tn=1024, name=f"in_a{l}_{tag}")
        u3 = z[:, :E].reshape(B, T, E) if not prompt else None
        if prompt:
            t = _pool_prompt(z, wts["w_grp_a"][l], wts["scale_a"][l], T=T, tm=256, name=f"pool{l}_{tag}")
            new_pool.append(z.reshape(B, T, 2 * E)[:, T - POOL_PAD:, :E].astype(F32))
        else:
            hist = jnp.pad(pool_prev[l].astype(F32), ((0, 0), (HALO - POOL_PAD, 0), (0, 0)))
            full = jnp.concatenate([hist, u3.astype(F32)], axis=1)
            t = _pool_sample(full.reshape(B * (HALO + T), E), z, wts["w_grp_a"][l], wts["scale_a"][l],
                             nseq=B, T=T, pos0=pos0, name=f"pool{l}_{tag}")
            new_pool.append(full[:, HALO + T - POOL_PAD:, :])
        x = _proj_res(t, wts["w_out_a"][l], x, tm=tm_res, tn=1024, name=f"out_a{l}_{tag}")

    k32, v32, k16, v16, logf_rep = _kv_proj(x, wts["norm_kv"], wts["w_k"], wts["w_v"], wts["w_f_rep"],
                                            wts["b_f_rep"], tm=256, name=f"kv_{tag}")
    logf3 = logf_rep[:, :N_HEADS].reshape(B, T, N_HEADS)
    logf_rep = logf_rep.reshape(B, T, LANES)
    if prompt:
        tq = tk = 512
    else:
        past_k, past_v, past_logf = past
        P = past_k.shape[1]
        past_rep = jnp.pad(jnp.tile(past_logf.astype(F32), (1, 1, 3)), ((0, 0), (0, 0), (0, LANES - 3 * N_HEADS)))
        logf_rep = jnp.concatenate([past_rep, logf_rep], axis=1)
        ck = past_k.reshape(B, P * N_HEADS, HEAD_DIM)
        cv = past_v.reshape(B, P * N_HEADS, HEAD_DIM)
    fa = _forget_cols(logf_rep, name=f"fcols_{tag}")

    for l in range(wts["w_in_b"].shape[0]):
        zb = _norm_proj(x, wts["norm_b"][l], wts["w_in_b"][l], tm=tm, tn=1024,
                        n_scaled=W_B // 1024, scale=q_scale, name=f"in_b{l}_{tag}")
        if prompt:
            og = _fox_prompt(zb.reshape(B, T, 2 * W_B), k16.reshape(B, T, W_B), v16.reshape(B, T, W_B), fa,
                             B=B, T=T, tq=tq, tk=tk, name=f"attn{l}_{tag}").reshape(M, W_B)
        else:
            og = _fox_decode(zb, k16, v16, ck, cv, fa, B=B, T=T, P=P, pc=512, name=f"attn{l}_{tag}")
        x = _proj_res(og, wts["w_out_b"][l], x, tm=tm_res, tn=1024, name=f"out_b{l}_{tag}")

    y = _rmsnorm(x, wts["norm_f"], tm=min(tm, 512), name=f"final_{tag}")
    return (y.reshape(B, T, D), k32.reshape(B, T, N_HEADS, HEAD_DIM), v32.reshape(B, T, N_HEADS, HEAD_DIM),
            logf3, jnp.stack(new_pool))


def kernel(x_prompt, x_sample, cache_k, cache_v, cache_logf, state_pool, norm_a, w_in_a, w_grp_a, scale_a,
           w_out_a, norm_kv, w_kv, b_f, norm_b, w_in_b, w_out_b, norm_f):
    wts = dict(
        norm_a=norm_a, w_in_a=w_in_a.astype(BF16), w_grp_a=w_grp_a.astype(BF16), scale_a=scale_a,
        w_out_a=w_out_a.astype(BF16), norm_kv=norm_kv,
        w_k=w_kv[:, :W_B].astype(BF16), w_v=w_kv[:, W_B:2 * W_B].astype(BF16),
        w_f_rep=jnp.pad(jnp.tile(w_kv[:, 2 * W_B:], (1, 3)), ((0, 0), (0, LANES - 3 * N_HEADS))).astype(BF16),
        b_f_rep=jnp.pad(jnp.tile(b_f, 3), (0, LANES - 3 * N_HEADS)),
        norm_b=norm_b, w_in_b=w_in_b.astype(BF16), w_out_b=w_out_b.astype(BF16), norm_f=norm_f,
    )
    y_p, k_p, v_p, lf_p, pool_p = _trunk(x_prompt, 0, None, None, wts, "p")
    y_s, k_s, v_s, lf_s, pool_s = _trunk(x_sample, cache_k.shape[1], state_pool,
                                         (cache_k, cache_v, cache_logf), wts, "s")
    return (y_p, y_s, k_p, v_p, lf_p, pool_p, k_s, v_s, lf_s, pool_s)
```

```python
import functools

import jax
import jax.numpy as jnp
from jax import lax
from jax.experimental import pallas as pl
from jax.experimental.pallas import tpu as pltpu

F32 = jnp.float32
BF16 = jnp.bfloat16

EPS = 1e-6
N_HEADS = 16
HEAD_DIM = 128
W_B = N_HEADS * HEAD_DIM
POOL_WINDOWS = (2, 4, 8, 16)
POOL_PAD = max(POOL_WINDOWS) - 1
HALO = 16
NEG = -1e30
LOG2E = 1.4426950408889634
ONES_ROWS = 16
LANES = 128
VMEM_LIMIT = 56 * 1024 * 1024


def _params(*sem):
    return pltpu.CompilerParams(dimension_semantics=sem, vmem_limit_bytes=VMEM_LIMIT)


def _silu(g):
    return g * (1.0 / (1.0 + jnp.exp(-g)))


def _rms_rows(xf, g):
    r = lax.rsqrt(jnp.mean(xf * xf, axis=-1, keepdims=True) + EPS)
    return (xf * r) * g


def _norm_proj_kernel(x_ref, g_ref, w_ref, *refs, n_scaled, scale, dual):
    h_ref = refs[-1]
    j = pl.program_id(1)

    @pl.when(j == 0)
    def _():
        h_ref[...] = _rms_rows(x_ref[...], g_ref[...]).astype(BF16)

    acc = jnp.dot(h_ref[...], w_ref[...], preferred_element_type=F32)
    if n_scaled:
        acc = acc * jnp.where(j < n_scaled, scale, 1.0)
    if dual:
        refs[0][...] = acc
        refs[1][...] = acc.astype(BF16)
    else:
        refs[0][...] = acc.astype(refs[0].dtype)


def _norm_proj(x, g, w, *, tm, tn, dual=False, n_scaled=0, scale=1.0, name):
    M, D = x.shape
    N = w.shape[1]
    o_spec = pl.BlockSpec((tm, tn), lambda i, j: (i, j))
    if dual:
        out_shape = (jax.ShapeDtypeStruct((M, N), F32), jax.ShapeDtypeStruct((M, N), BF16))
        out_specs = (o_spec, o_spec)
    else:
        out_shape = jax.ShapeDtypeStruct((M, N), BF16)
        out_specs = o_spec
    return pl.pallas_call(
        functools.partial(_norm_proj_kernel, n_scaled=n_scaled, scale=scale, dual=dual),
        out_shape=out_shape,
        grid=(M // tm, N // tn),
        in_specs=[
            pl.BlockSpec((tm, D), lambda i, j: (i, 0)),
            pl.BlockSpec((1, D), lambda i, j: (0, 0)),
            pl.BlockSpec((D, tn), lambda i, j: (0, j)),
        ],
        out_specs=out_specs,
        scratch_shapes=[pltpu.VMEM((tm, D), BF16)],
        compiler_params=_params("parallel", "arbitrary"),
        name=name,
    )(x, g.reshape(1, D), w)


def _proj_res_kernel(a_ref, w_ref, x_ref, o_ref):
    o_ref[...] = x_ref[...] + jnp.dot(a_ref[...], w_ref[...], preferred_element_type=F32)


def _proj_res(a, w, x, *, tm, tn, name):
    M, K = a.shape
    N = w.shape[1]
    return pl.pallas_call(
        _proj_res_kernel,
        out_shape=jax.ShapeDtypeStruct((M, N), F32),
        grid=(N // tn, M // tm),
        in_specs=[
            pl.BlockSpec((tm, K), lambda j, i: (i, 0)),
            pl.BlockSpec((K, tn), lambda j, i: (0, j)),
            pl.BlockSpec((tm, tn), lambda j, i: (i, j)),
        ],
        out_specs=pl.BlockSpec((tm, tn), lambda j, i: (i, j)),
        compiler_params=_params("parallel", "parallel"),
        name=name,
    )(a, w, x)


def _proj_res_norm_kernel(a_ref, w_ref, x_ref, g_ref, o_ref):
    x = x_ref[...] + jnp.dot(a_ref[...], w_ref[...], preferred_element_type=F32)
    o_ref[...] = _rms_rows(x, g_ref[...])


def _proj_res_norm(a, w, x, g, *, tm, name):
    M, K = a.shape
    N = w.shape[1]
    return pl.pallas_call(
        _proj_res_norm_kernel,
        out_shape=jax.ShapeDtypeStruct((M, N), F32),
        grid=(M // tm,),
        in_specs=[
            pl.BlockSpec((tm, K), lambda i: (i, 0)),
            pl.BlockSpec((K, N), lambda i: (0, 0)),
            pl.BlockSpec((tm, N), lambda i: (i, 0)),
            pl.BlockSpec((1, N), lambda i: (0, 0)),
        ],
        out_specs=pl.BlockSpec((tm, N), lambda i: (i, 0)),
        compiler_params=_params("parallel"),
        name=name,
    )(a, w, x, g.reshape(1, N))


def _window_sum(full, w):
    s = full
    k = 1
    while k < w:
        s = s + pltpu.roll(s, k, 0)
        k *= 2
    return s


def _pool_finish(s, uf, inv_cnt, gate, wg, sc):
    d = s * inv_cnt - uf
    y = jnp.dot(d.astype(BF16), wg, preferred_element_type=F32) * sc
    return (y * _silu(gate)).astype(BF16)


def _pool_prompt_kernel(u_ref, halo_ref, gate_ref, wg_ref, sc_ref, o_ref, *, tm, tiles_per_seq, pos0):
    G = wg_ref.shape[-1]
    ti = pl.program_id(0) % tiles_per_seq
    t = pos0 + ti * tm + lax.broadcasted_iota(jnp.int32, (tm, 1), 0)
    for g, w in enumerate(POOL_WINDOWS):
        c = slice(g * G, (g + 1) * G)
        uf = u_ref[:, c].astype(F32)
        hf = jnp.where(ti == 0, 0.0, halo_ref[:, c].astype(F32))
        s = _window_sum(jnp.concatenate([hf, uf], axis=0), w)[HALO:, :]
        inv_cnt = 1.0 / jnp.minimum(t + 1, w).astype(F32)
        o_ref[:, c] = _pool_finish(s, uf, inv_cnt, gate_ref[:, c].astype(F32), wg_ref[g], sc_ref[:, c])


def _pool_prompt(z, wg, sc, *, T, tm, name):
    M = z.shape[0]
    E = z.shape[1] // 2
    hb = tm // HALO
    return pl.pallas_call(
        functools.partial(_pool_prompt_kernel, tm=tm, tiles_per_seq=T // tm, pos0=0),
        out_shape=jax.ShapeDtypeStruct((M, E), BF16),
        grid=(M // tm,),
        in_specs=[
            pl.BlockSpec((tm, E), lambda i: (i, 0)),
            pl.BlockSpec((HALO, E), lambda i: (jnp.maximum(i * hb - 1, 0), 0)),
            pl.BlockSpec((tm, E), lambda i: (i, 1)),
            pl.BlockSpec(wg.shape, lambda i: (0, 0, 0)),
            pl.BlockSpec((1, E), lambda i: (0, 0)),
        ],
        out_specs=pl.BlockSpec((tm, E), lambda i: (i, 0)),
        compiler_params=_params("parallel"),
        name=name,
    )(z, z, z, wg, sc.reshape(1, E))


def _pool_sample_kernel(full_ref, gate_ref, wg_ref, sc_ref, o_ref, *, nseq, T, pos0):
    G = wg_ref.shape[-1]
    seg = HALO + T
    t = pos0 + lax.broadcasted_iota(jnp.int32, (nseq * T, 1), 0) % T

    def tail(a):
        return a.reshape(nseq, seg, G)[:, HALO:, :].reshape(nseq * T, G)

    for g, w in enumerate(POOL_WINDOWS):
        c = slice(g * G, (g + 1) * G)
        full = full_ref[:, c]
        inv_cnt = 1.0 / jnp.minimum(t + 1, w).astype(F32)
        o_ref[:, c] = _pool_finish(tail(_window_sum(full, w)), tail(full), inv_cnt,
                                   gate_ref[:, c].astype(F32), wg_ref[g], sc_ref[:, c])


def _pool_sample(full, z, wg, sc, *, nseq, T, pos0, name):
    E = full.shape[1]
    return pl.pallas_call(
        functools.partial(_pool_sample_kernel, nseq=nseq, T=T, pos0=pos0),
        out_shape=jax.ShapeDtypeStruct((nseq * T, E), BF16),
        grid=(1,),
        in_specs=[
            pl.BlockSpec(full.shape, lambda i: (0, 0)),
            pl.BlockSpec((nseq * T, E), lambda i: (0, 1)),
            pl.BlockSpec(wg.shape, lambda i: (0, 0, 0)),
            pl.BlockSpec((1, E), lambda i: (0, 0)),
        ],
        out_specs=pl.BlockSpec((nseq * T, E), lambda i: (0, 0)),
        compiler_params=_params("arbitrary"),
        name=name,
    )(full, z, wg, sc.reshape(1, E))


def _kv_kernel(x_ref, g_ref, wk_ref, wv_ref, wf_ref, bf_ref, k32_ref, v32_ref, k16_ref, v16_ref, lf_ref, *, tn):
    h = _rms_rows(x_ref[...], g_ref[...]).astype(BF16)
    for w_ref, o32_ref, o16_ref in ((wk_ref, k32_ref, k16_ref), (wv_ref, v32_ref, v16_ref)):
        for c in range(w_ref.shape[1] // tn):
            cs = slice(c * tn, (c + 1) * tn)
            acc = jnp.dot(h, w_ref[:, cs], preferred_element_type=F32)
            o32_ref[:, cs] = acc
            o16_ref[:, cs] = acc.astype(BF16)
    a = jnp.dot(h, wf_ref[...], preferred_element_type=F32) + bf_ref[...]
    lf_ref[...] = jnp.minimum(a, 0.0) - jnp.log1p(jnp.exp(-jnp.abs(a)))


def _kv_proj(x, g, wk, wv, wf, bf, *, tm, name):
    M, D = x.shape
    N = wk.shape[1]
    L = wf.shape[1]
    resident = dict(pipeline_mode=pl.Buffered(1))
    row = lambda i: (i, 0)
    fixed = lambda i: (0, 0)
    return pl.pallas_call(
        functools.partial(_kv_kernel, tn=512),
        out_shape=(jax.ShapeDtypeStruct((M, N), F32), jax.ShapeDtypeStruct((M, N), F32),
                   jax.ShapeDtypeStruct((M, N), BF16), jax.ShapeDtypeStruct((M, N), BF16),
                   jax.ShapeDtypeStruct((M, L), F32)),
        grid=(M // tm,),
        in_specs=[
            pl.BlockSpec((tm, D), row),
            pl.BlockSpec((1, D), fixed),
            pl.BlockSpec((D, N), fixed, **resident),
            pl.BlockSpec((D, N), fixed, **resident),
            pl.BlockSpec((D, L), fixed, **resident),
            pl.BlockSpec((1, L), fixed),
        ],
        out_specs=(pl.BlockSpec((tm, N), row), pl.BlockSpec((tm, N), row), pl.BlockSpec((tm, N), row),
                   pl.BlockSpec((tm, N), row), pl.BlockSpec((tm, L), row)),
        compiler_params=_params("parallel"),
        name=name,
    )(x, g.reshape(1, D), wk, wv, wf, bf.reshape(1, L))


def _forget_cols_kernel(x_ref, o_ref):
    x = x_ref[0]
    n = x.shape[0]
    row = lax.broadcasted_iota(jnp.int32, x.shape, 0)
    lane = lax.broadcasted_iota(jnp.int32, x.shape, 1)
    k = 1
    while k < n:
        x = x + jnp.where(row >= k, pltpu.roll(x, k, 0), 0.0)
        k *= 2
    x = x * LOG2E
    r1 = x - x.astype(BF16).astype(F32)
    r2 = r1 - r1.astype(BF16).astype(F32)
    piece = jnp.where(lane < N_HEADS, x, jnp.where(lane < 2 * N_HEADS, r1, r2))
    o_ref[0] = jnp.where(lane < 3 * N_HEADS, piece, 0.0).astype(BF16)


def _forget_cols(logf_rep, *, name):
    B, n, L = logf_rep.shape
    return pl.pallas_call(
        _forget_cols_kernel,
        out_shape=jax.ShapeDtypeStruct((B, n, L), BF16),
        grid=(B,),
        in_specs=[pl.BlockSpec((1, n, L), lambda b: (b, 0, 0))],
        out_specs=pl.BlockSpec((1, n, L), lambda b: (b, 0, 0)),
        compiler_params=_params("parallel"),
        name=name,
    )(logf_rep)


def _forget_query_cols(h, rows):
    lane = lax.broadcasted_iota(jnp.int32, (rows, LANES), 1)
    return jnp.where((lane % N_HEADS == h) & (lane < 3 * N_HEADS), -1.0, 0.0).astype(BF16)


def _fox_prompt_kernel(q_ref, k_ref, v_ref, fa_ref, gate_ref, o_ref, m_sc, acc_sc, s0_sc, s1_sc, *,
                       tq, tk, hps):
    qi = pl.program_id(2)
    heads = range(hps)
    cols = [slice(hh * HEAD_DIM, (hh + 1) * HEAD_DIM) for hh in heads]
    m_sc[...] = jnp.full(m_sc.shape, NEG, F32)
    acc_sc[...] = jnp.zeros(acc_sc.shape, F32)
    ones = jnp.ones((ONES_ROWS, tk), BF16)
    qa = [jnp.concatenate([q_ref[0, :, cols[hh]], _forget_query_cols(pl.program_id(1) * hps + hh, tq)], axis=1)
          for hh in heads]

    def scores(j, s_ref):
        off = pl.multiple_of(j * tk, tk)
        fa = fa_ref[0, pl.ds(off, tk), :]
        for hh in heads:
            ka = jnp.concatenate([k_ref[0, pl.ds(off, tk), cols[hh]], fa], axis=1)
            s_ref[hh] = lax.dot_general(ka, qa[hh], (((1,), (1,)), ((), ())),
                                        preferred_element_type=F32)

    def absorb(j, s_ref, masked):
        off = pl.multiple_of(j * tk, tk)
        for hh in heads:
            s = s_ref[hh]
            if masked:
                kpos = off + lax.broadcasted_iota(jnp.int32, (tk, tq), 0)
                qpos = qi * tq + lax.broadcasted_iota(jnp.int32, (tk, tq), 1)
                s = jnp.where(kpos <= qpos, s, NEG)
            m_prev = m_sc[hh]
            m_new = jnp.maximum(m_prev, jnp.max(s, axis=0, keepdims=True))
            alpha = jnp.exp2(m_prev - m_new)
            p = jnp.exp2(s - m_new).astype(BF16)
            vt = jnp.concatenate([v_ref[0, pl.ds(off, tk), cols[hh]].T, ones], axis=0)
            acc_sc[hh] = alpha * acc_sc[hh] + jnp.dot(vt, p, preferred_element_type=F32)
            m_sc[hh] = m_new

    scores(0, s0_sc)

    def body(i, c):
        j = 2 * i
        scores(j + 1, s1_sc)
        absorb(j, s0_sc, False)
        scores(j + 2, s0_sc)
        absorb(j + 1, s1_sc, False)
        return c

    lax.fori_loop(0, qi // 2, body, 0)

    @pl.when(qi % 2 == 0)
    def _():
        absorb(qi, s0_sc, True)

    @pl.when(qi % 2 == 1)
    def _():
        scores(qi, s1_sc)
        absorb(qi - 1, s0_sc, False)
        absorb(qi, s1_sc, True)

    for hh in heads:
        acc = acc_sc[hh]
        o = (acc[:HEAD_DIM] * (1.0 / acc[HEAD_DIM:HEAD_DIM + 1])).T
        o_ref[0, :, cols[hh]] = (o * _silu(gate_ref[0, :, cols[hh]].astype(F32))).astype(BF16)


def _fox_prompt(zb, k16, v16, fa, *, B, T, tq, tk, hps, name):
    assert tq == tk and T % tq == 0 and N_HEADS % hps == 0
    w = hps * HEAD_DIM
    return pl.pallas_call(
        functools.partial(_fox_prompt_kernel, tq=tq, tk=tk, hps=hps),
        out_shape=jax.ShapeDtypeStruct((B, T, W_B), BF16),
        grid=(B, N_HEADS // hps, T // tq),
        in_specs=[
            pl.BlockSpec((1, tq, w), lambda b, h, i: (b, i, h)),
            pl.BlockSpec((1, T, w), lambda b, h, i: (b, 0, h)),
            pl.BlockSpec((1, T, w), lambda b, h, i: (b, 0, h)),
            pl.BlockSpec((1, T, LANES), lambda b, h, i: (b, 0, 0)),
            pl.BlockSpec((1, tq, w), lambda b, h, i: (b, i, N_HEADS // hps + h)),
        ],
        out_specs=pl.BlockSpec((1, tq, w), lambda b, h, i: (b, i, h)),
        scratch_shapes=[
            pltpu.VMEM((hps, 1, tq), F32),
            pltpu.VMEM((hps, HEAD_DIM + ONES_ROWS, tq), F32),
            pltpu.VMEM((hps, tk, tq), F32),
            pltpu.VMEM((hps, tk, tq), F32),
        ],
        compiler_params=_params("parallel", "parallel", "arbitrary"),
        name=name,
    )(zb, k16, v16, fa, zb)


def _fox_decode_kernel(z_ref, kc_ref, vc_ref, kn_ref, vn_ref, fa_ref, o_ref, m_sc, l_sc, acc_sc, *, P, T, pc):
    c = pl.program_id(1)
    nt = (((1,), (1,)), ((), ()))

    @pl.when(c == 0)
    def _():
        m_sc[...] = jnp.full(m_sc.shape, NEG, F32)
        l_sc[...] = jnp.zeros(l_sc.shape, F32)
        acc_sc[...] = jnp.zeros(acc_sc.shape, F32)

    def qa_of(h):
        return jnp.concatenate([z_ref[:, h * HEAD_DIM:(h + 1) * HEAD_DIM], _forget_query_cols(h, T)], axis=1)

    def absorb(keys, vals, fa_blk, mask):
        s = jnp.concatenate(
            [lax.dot_general(qa_of(h), jnp.concatenate([keys(h), fa_blk], axis=1), nt,
                             preferred_element_type=F32) for h in range(N_HEADS)], axis=0)
        if mask is not None:
            s = jnp.where(mask, s, NEG)
        m_prev = m_sc[...]
        m_new = jnp.maximum(m_prev, jnp.max(s, axis=-1, keepdims=True))
        alpha = jnp.exp2(m_prev - m_new)
        p = jnp.exp2(s - m_new)
        l_sc[...] = alpha * l_sc[...] + jnp.sum(p, axis=-1, keepdims=True)
        pb = p.astype(BF16)
        pv = jnp.concatenate(
            [jnp.dot(pb[h * T:(h + 1) * T, :], vals(h), preferred_element_type=F32) for h in range(N_HEADS)],
            axis=0)
        acc_sc[...] = alpha * acc_sc[...] + pv
        m_sc[...] = m_new

    absorb(lambda h: kc_ref[0, pl.ds(h, pc, stride=N_HEADS), :].astype(BF16),
           lambda h: vc_ref[0, pl.ds(h, pc, stride=N_HEADS), :].astype(BF16),
           fa_ref[0, pl.ds(pl.multiple_of(c * pc, pc), pc), :], None)

    @pl.when(c == pl.num_programs(1) - 1)
    def _():
        rows = lax.broadcasted_iota(jnp.int32, (N_HEADS * T, T), 0) % T
        cols = lax.broadcasted_iota(jnp.int32, (N_HEADS * T, T), 1)
        absorb(lambda h: kn_ref[:, h * HEAD_DIM:(h + 1) * HEAD_DIM],
               lambda h: vn_ref[:, h * HEAD_DIM:(h + 1) * HEAD_DIM],
               fa_ref[0, P:, :], cols <= rows)
        o = acc_sc[...] * (1.0 / l_sc[...])
        for h in range(N_HEADS):
            hc = slice(h * HEAD_DIM, (h + 1) * HEAD_DIM)
            gate = z_ref[:, W_B + h * HEAD_DIM:W_B + (h + 1) * HEAD_DIM].astype(F32)
            o_ref[:, hc] = (o[h * T:(h + 1) * T, :] * _silu(gate)).astype(BF16)


def _fox_decode(zb, k16, v16, cache_k, cache_v, fa, *, B, T, P, pc, name):
    c_spec = pl.BlockSpec((1, pc * N_HEADS, HEAD_DIM), lambda b, c: (b, c, 0))
    n_spec = pl.BlockSpec((T, W_B), lambda b, c: (b, 0))
    return pl.pallas_call(
        functools.partial(_fox_decode_kernel, P=P, T=T, pc=pc),
        out_shape=jax.ShapeDtypeStruct((B * T, W_B), BF16),
        grid=(B, P // pc),
        in_specs=[
            pl.BlockSpec((T, 2 * W_B), lambda b, c: (b, 0)),
            c_spec, c_spec, n_spec, n_spec,
            pl.BlockSpec((1, P + T, LANES), lambda b, c: (b, 0, 0)),
        ],
        out_specs=n_spec,
        scratch_shapes=[
            pltpu.VMEM((N_HEADS * T, 1), F32),
            pltpu.VMEM((N_HEADS * T, 1), F32),
            pltpu.VMEM((N_HEADS * T, HEAD_DIM), F32),
        ],
        compiler_params=_params("parallel", "arbitrary"),
        name=name,
    )(zb, cache_k, cache_v, k16, v16, fa)


def _trunk(x3, pos0, pool_prev, past, wts, tag):
    B, T, D = x3.shape
    M = B * T
    x = x3.reshape(M, D)
    prompt = pool_prev is None
    tm = 1024 if prompt else M
    tm_res = 512 if prompt else M
    E = wts["w_out_a"][0].shape[0]
    q_scale = HEAD_DIM ** -0.5 * LOG2E

    new_pool = []
    for l in range(len(wts["w_in_a"])):
        z = _norm_proj(x, wts["norm_a"][l], wts["w_in_a"][l], tm=tm, tn=1024, name=f"in_a{l}_{tag}")
        u3 = z[:, :E].reshape(B, T, E) if not prompt else None
        if prompt:
            t = _pool_prompt(z, wts["w_grp_a"][l], wts["scale_a"][l], T=T, tm=256, name=f"pool{l}_{tag}")
            new_pool.append(z.reshape(B, T, 2 * E)[:, T - POOL_PAD:, :E].astype(F32))
        else:
            hist = jnp.pad(pool_prev[l].astype(F32), ((0, 0), (HALO - POOL_PAD, 0), (0, 0)))
            full = jnp.concatenate([hist, u3.astype(F32)], axis=1)
            t = _pool_sample(full.reshape(B * (HALO + T), E), z, wts["w_grp_a"][l], wts["scale_a"][l],
                             nseq=B, T=T, pos0=pos0, name=f"pool{l}_{tag}")
            new_pool.append(full[:, HALO + T - POOL_PAD:, :])
        x = _proj_res(t, wts["w_out_a"][l], x, tm=tm_res, tn=1024, name=f"out_a{l}_{tag}")

    k32, v32, k16, v16, logf_rep = _kv_proj(x, wts["norm_kv"], wts["w_k"], wts["w_v"], wts["w_f_rep"],
                                            wts["b_f_rep"], tm=256, name=f"kv_{tag}")
    logf3 = logf_rep[:, :N_HEADS].reshape(B, T, N_HEADS)
    logf_rep = logf_rep.reshape(B, T, LANES)
    if prompt:
        tq = tk = 512
    else:
        past_k, past_v, past_logf = past
        P = past_k.shape[1]
        past_rep = jnp.pad(jnp.tile(past_logf.astype(F32), (1, 1, 3)), ((0, 0), (0, 0), (0, LANES - 3 * N_HEADS)))
        logf_rep = jnp.concatenate([past_rep, logf_rep], axis=1)
        ck = past_k.reshape(B, P * N_HEADS, HEAD_DIM)
        cv = past_v.reshape(B, P * N_HEADS, HEAD_DIM)
    fa = _forget_cols(logf_rep, name=f"fcols_{tag}")

    n_b = len(wts["w_in_b"])
    for l in range(n_b):
        zb = _norm_proj(x, wts["norm_b"][l], wts["w_in_b"][l], tm=tm, tn=1024,
                        n_scaled=W_B // 1024, scale=q_scale, name=f"in_b{l}_{tag}")
        if prompt:
            og = _fox_prompt(zb.reshape(B, T, 2 * W_B), k16.reshape(B, T, W_B), v16.reshape(B, T, W_B), fa,
                             B=B, T=T, tq=tq, tk=tk, hps=4, name=f"attn{l}_{tag}").reshape(M, W_B)
        else:
            og = _fox_decode(zb, k16, v16, ck, cv, fa, B=B, T=T, P=P, pc=512, name=f"attn{l}_{tag}")
        if l + 1 < n_b:
            x = _proj_res(og, wts["w_out_b"][l], x, tm=tm_res, tn=1024, name=f"out_b{l}_{tag}")
        else:
            y = _proj_res_norm(og, wts["w_out_b"][l], x, wts["norm_f"], tm=min(tm_res, 512),
                               name=f"out_b{l}_{tag}")
    return (y.reshape(B, T, D), k32.reshape(B, T, N_HEADS, HEAD_DIM), v32.reshape(B, T, N_HEADS, HEAD_DIM),
            logf3, jnp.stack(new_pool))


def kernel(x_prompt, x_sample, cache_k, cache_v, cache_logf, state_pool, norm_a, w_in_a, w_grp_a, scale_a,
           w_out_a, norm_kv, w_kv, b_f, norm_b, w_in_b, w_out_b, norm_f):
    def per_layer(w):
        return [w[l].astype(BF16) for l in range(w.shape[0])]

    wts = dict(
        norm_a=norm_a, w_in_a=per_layer(w_in_a), w_grp_a=per_layer(w_grp_a), scale_a=scale_a,
        w_out_a=per_layer(w_out_a), norm_kv=norm_kv,
        w_k=w_kv[:, :W_B].astype(BF16), w_v=w_kv[:, W_B:2 * W_B].astype(BF16),
        w_f_rep=jnp.pad(jnp.tile(w_kv[:, 2 * W_B:], (1, 3)), ((0, 0), (0, LANES - 3 * N_HEADS))).astype(BF16),
        b_f_rep=jnp.pad(jnp.tile(b_f, 3), (0, LANES - 3 * N_HEADS)),
        norm_b=norm_b, w_in_b=per_layer(w_in_b), w_out_b=per_layer(w_out_b), norm_f=norm_f,
    )
    y_p, k_p, v_p, lf_p, pool_p = _trunk(x_prompt, 0, None, None, wts, "p")
    y_s, k_s, v_s, lf_s, pool_s = _trunk(x_sample, cache_k.shape[1], state_pool,
                                         (cache_k, cache_v, cache_logf), wts, "s")
    return (y_p, y_s, k_p, v_p, lf_p, pool_p, k_s, v_s, lf_s, pool_s)
```

```python
import functools

import jax
import jax.numpy as jnp
from jax import lax
from jax.experimental import pallas as pl
from jax.experimental.pallas import tpu as pltpu

F32 = jnp.float32
BF16 = jnp.bfloat16

EPS = 1e-6
N_HEADS = 16
HEAD_DIM = 128
W_B = N_HEADS * HEAD_DIM
POOL_WINDOWS = (2, 4, 8, 16)
POOL_PAD = max(POOL_WINDOWS) - 1
HALO = 16
NEG = -1e30
LOG2E = 1.4426950408889634
ONES_ROWS = 16
LANES = 128
VMEM_LIMIT = 56 * 1024 * 1024


def _params(*sem):
    return pltpu.CompilerParams(dimension_semantics=sem, vmem_limit_bytes=VMEM_LIMIT)


def _silu(g):
    return g * (1.0 / (1.0 + jnp.exp(-g)))


def _rms_rows(xf, g):
    r = lax.rsqrt(jnp.mean(xf * xf, axis=-1, keepdims=True) + EPS)
    return (xf * r) * g


def _norm_proj_kernel(x_ref, g_ref, w_ref, *refs, n_scaled, scale, dual):
    h_ref = refs[-1]
    j = pl.program_id(1)

    @pl.when(j == 0)
    def _():
        h_ref[...] = _rms_rows(x_ref[...], g_ref[...]).astype(BF16)

    acc = jnp.dot(h_ref[...], w_ref[...], preferred_element_type=F32)
    if n_scaled:
        acc = acc * jnp.where(j < n_scaled, scale, 1.0)
    if dual:
        refs[0][...] = acc
        refs[1][...] = acc.astype(BF16)
    else:
        refs[0][...] = acc.astype(refs[0].dtype)


def _norm_proj(x, g, w, *, tm, tn, dual=False, n_scaled=0, scale=1.0, name):
    M, D = x.shape
    N = w.shape[1]
    o_spec = pl.BlockSpec((tm, tn), lambda i, j: (i, j))
    if dual:
        out_shape = (jax.ShapeDtypeStruct((M, N), F32), jax.ShapeDtypeStruct((M, N), BF16))
        out_specs = (o_spec, o_spec)
    else:
        out_shape = jax.ShapeDtypeStruct((M, N), BF16)
        out_specs = o_spec
    return pl.pallas_call(
        functools.partial(_norm_proj_kernel, n_scaled=n_scaled, scale=scale, dual=dual),
        out_shape=out_shape,
        grid=(M // tm, N // tn),
        in_specs=[
            pl.BlockSpec((tm, D), lambda i, j: (i, 0)),
            pl.BlockSpec((1, D), lambda i, j: (0, 0)),
            pl.BlockSpec((D, tn), lambda i, j: (0, j)),
        ],
        out_specs=out_specs,
        scratch_shapes=[pltpu.VMEM((tm, D), BF16)],
        compiler_params=_params("parallel", "arbitrary"),
        name=name,
    )(x, g.reshape(1, D), w)


def _proj_res_kernel(a_ref, w_ref, x_ref, o_ref):
    o_ref[...] = x_ref[...] + jnp.dot(a_ref[...], w_ref[...], preferred_element_type=F32)


def _proj_res(a, w, x, *, tm, tn, name):
    M, K = a.shape
    N = w.shape[1]
    return pl.pallas_call(
        _proj_res_kernel,
        out_shape=jax.ShapeDtypeStruct((M, N), F32),
        grid=(N // tn, M // tm),
        in_specs=[
            pl.BlockSpec((tm, K), lambda j, i: (i, 0)),
            pl.BlockSpec((K, tn), lambda j, i: (0, j)),
            pl.BlockSpec((tm, tn), lambda j, i: (i, j)),
        ],
        out_specs=pl.BlockSpec((tm, tn), lambda j, i: (i, j)),
        compiler_params=_params("parallel", "parallel"),
        name=name,
    )(a, w, x)


def _proj_res_norm_kernel(a_ref, w_ref, x_ref, g_ref, o_ref):
    x = x_ref[...] + jnp.dot(a_ref[...], w_ref[...], preferred_element_type=F32)
    o_ref[...] = _rms_rows(x, g_ref[...])


def _proj_res_norm(a, w, x, g, *, tm, name):
    M, K = a.shape
    N = w.shape[1]
    return pl.pallas_call(
        _proj_res_norm_kernel,
        out_shape=jax.ShapeDtypeStruct((M, N), F32),
        grid=(M // tm,),
        in_specs=[
            pl.BlockSpec((tm, K), lambda i: (i, 0)),
            pl.BlockSpec((K, N), lambda i: (0, 0)),
            pl.BlockSpec((tm, N), lambda i: (i, 0)),
            pl.BlockSpec((1, N), lambda i: (0, 0)),
        ],
        out_specs=pl.BlockSpec((tm, N), lambda i: (i, 0)),
        compiler_params=_params("parallel"),
        name=name,
    )(a, w, x, g.reshape(1, N))


def _window_sum(full, w):
    s = full
    k = 1
    while k < w:
        s = s + pltpu.roll(s, k, 0)
        k *= 2
    return s


def _pool_finish(s, uf, inv_cnt, gate, wg, sc):
    d = s * inv_cnt - uf
    y = jnp.dot(d.astype(BF16), wg, preferred_element_type=F32) * sc
    return (y * _silu(gate)).astype(BF16)


def _pool_prompt_kernel(u_ref, halo_ref, gate_ref, wg_ref, sc_ref, o_ref, *, tm, tiles_per_seq, pos0):
    G = wg_ref.shape[-1]
    ti = pl.program_id(0) % tiles_per_seq
    t = pos0 + ti * tm + lax.broadcasted_iota(jnp.int32, (tm, 1), 0)
    for g, w in enumerate(POOL_WINDOWS):
        c = slice(g * G, (g + 1) * G)
        uf = u_ref[:, c].astype(F32)
        hf = jnp.where(ti == 0, 0.0, halo_ref[:, c].astype(F32))
        s = _window_sum(jnp.concatenate([hf, uf], axis=0), w)[HALO:, :]
        inv_cnt = 1.0 / jnp.minimum(t + 1, w).astype(F32)
        o_ref[:, c] = _pool_finish(s, uf, inv_cnt, gate_ref[:, c].astype(F32), wg_ref[g], sc_ref[:, c])


def _pool_prompt(z, wg, sc, *, T, tm, name):
    M = z.shape[0]
    E = z.shape[1] // 2
    hb = tm // HALO
    return pl.pallas_call(
        functools.partial(_pool_prompt_kernel, tm=tm, tiles_per_seq=T // tm, pos0=0),
        out_shape=jax.ShapeDtypeStruct((M, E), BF16),
        grid=(M // tm,),
        in_specs=[
            pl.BlockSpec((tm, E), lambda i: (i, 0)),
            pl.BlockSpec((HALO, E), lambda i: (jnp.maximum(i * hb - 1, 0), 0)),
            pl.BlockSpec((tm, E), lambda i: (i, 1)),
            pl.BlockSpec(wg.shape, lambda i: (0, 0, 0)),
            pl.BlockSpec((1, E), lambda i: (0, 0)),
        ],
        out_specs=pl.BlockSpec((tm, E), lambda i: (i, 0)),
        compiler_params=_params("parallel"),
        name=name,
    )(z, z, z, wg, sc.reshape(1, E))


def _pool_sample_kernel(full_ref, gate_ref, wg_ref, sc_ref, o_ref, *, nseq, T, pos0):
    G = wg_ref.shape[-1]
    seg = HALO + T
    t = pos0 + lax.broadcasted_iota(jnp.int32, (nseq * T, 1), 0) % T

    def tail(a):
        return a.reshape(nseq, seg, G)[:, HALO:, :].reshape(nseq * T, G)

    for g, w in enumerate(POOL_WINDOWS):
        c = slice(g * G, (g + 1) * G)
        full = full_ref[:, c]
        inv_cnt = 1.0 / jnp.minimum(t + 1, w).astype(F32)
        o_ref[:, c] = _pool_finish(tail(_window_sum(full, w)), tail(full), inv_cnt,
                                   gate_ref[:, c].astype(F32), wg_ref[g], sc_ref[:, c])


def _pool_sample(full, z, wg, sc, *, nseq, T, pos0, name):
    E = full.shape[1]
    return pl.pallas_call(
        functools.partial(_pool_sample_kernel, nseq=nseq, T=T, pos0=pos0),
        out_shape=jax.ShapeDtypeStruct((nseq * T, E), BF16),
        grid=(1,),
        in_specs=[
            pl.BlockSpec(full.shape, lambda i: (0, 0)),
            pl.BlockSpec((nseq * T, E), lambda i: (0, 1)),
            pl.BlockSpec(wg.shape, lambda i: (0, 0, 0)),
            pl.BlockSpec((1, E), lambda i: (0, 0)),
        ],
        out_specs=pl.BlockSpec((nseq * T, E), lambda i: (0, 0)),
        compiler_params=_params("arbitrary"),
        name=name,
    )(full, z, wg, sc.reshape(1, E))


def _kv_kernel(x_ref, g_ref, wk_ref, wv_ref, wf_ref, bf_ref, k32_ref, v32_ref, k16_ref, v16_ref, lf_ref, *, tn):
    tm = x_ref.shape[0]
    h = _rms_rows(x_ref[...], g_ref[...]).astype(BF16)
    for w_ref, o32_ref, o16_ref in ((wk_ref, k32_ref, k16_ref), (wv_ref, v32_ref, v16_ref)):
        for c in range(w_ref.shape[1] // tn):
            cs = slice(c * tn, (c + 1) * tn)
            acc = jnp.dot(h, w_ref[:, cs], preferred_element_type=F32)
            o16_ref[:, cs] = acc.astype(BF16)
            for hd in range(tn // HEAD_DIM):
                head = c * (tn // HEAD_DIM) + hd
                o32_ref[pl.ds(head, tm, stride=N_HEADS), :] = acc[:, hd * HEAD_DIM:(hd + 1) * HEAD_DIM]
    a = jnp.dot(h, wf_ref[...], preferred_element_type=F32) + bf_ref[...]
    lf_ref[...] = jnp.minimum(a, 0.0) - jnp.log1p(jnp.exp(-jnp.abs(a)))


def _kv_proj(x, g, wk, wv, wf, bf, *, tm, name):
    M, D = x.shape
    N = wk.shape[1]
    L = wf.shape[1]
    resident = dict(pipeline_mode=pl.Buffered(1))
    row = lambda i: (i, 0)
    fixed = lambda i: (0, 0)
    return pl.pallas_call(
        functools.partial(_kv_kernel, tn=512),
        out_shape=(jax.ShapeDtypeStruct((M * N_HEADS, HEAD_DIM), F32),
                   jax.ShapeDtypeStruct((M * N_HEADS, HEAD_DIM), F32),
                   jax.ShapeDtypeStruct((M, N), BF16), jax.ShapeDtypeStruct((M, N), BF16),
                   jax.ShapeDtypeStruct((M, L), F32)),
        grid=(M // tm,),
        in_specs=[
            pl.BlockSpec((tm, D), row),
            pl.BlockSpec((1, D), fixed),
            pl.BlockSpec((D, N), fixed, **resident),
            pl.BlockSpec((D, N), fixed, **resident),
            pl.BlockSpec((D, L), fixed, **resident),
            pl.BlockSpec((1, L), fixed),
        ],
        out_specs=(pl.BlockSpec((tm * N_HEADS, HEAD_DIM), row), pl.BlockSpec((tm * N_HEADS, HEAD_DIM), row),
                   pl.BlockSpec((tm, N), row), pl.BlockSpec((tm, N), row), pl.BlockSpec((tm, L), row)),
        compiler_params=_params("parallel"),
        name=name,
    )(x, g.reshape(1, D), wk, wv, wf, bf.reshape(1, L))


def _forget_cols_kernel(x_ref, o_ref):
    x = x_ref[0]
    n = x.shape[0]
    row = lax.broadcasted_iota(jnp.int32, x.shape, 0)
    lane = lax.broadcasted_iota(jnp.int32, x.shape, 1)
    k = 1
    while k < n:
        x = x + jnp.where(row >= k, pltpu.roll(x, k, 0), 0.0)
        k *= 2
    x = x * LOG2E
    r1 = x - x.astype(BF16).astype(F32)
    r2 = r1 - r1.astype(BF16).astype(F32)
    piece = jnp.where(lane < N_HEADS, x, jnp.where(lane < 2 * N_HEADS, r1, r2))
    o_ref[0] = jnp.where(lane < 3 * N_HEADS, piece, 0.0).astype(BF16)


def _forget_cols(logf_rep, *, name):
    B, n, L = logf_rep.shape
    return pl.pallas_call(
        _forget_cols_kernel,
        out_shape=jax.ShapeDtypeStruct((B, n, L), BF16),
        grid=(B,),
        in_specs=[pl.BlockSpec((1, n, L), lambda b: (b, 0, 0))],
        out_specs=pl.BlockSpec((1, n, L), lambda b: (b, 0, 0)),
        compiler_params=_params("parallel"),
        name=name,
    )(logf_rep)


def _forget_query_cols(h, rows):
    lane = lax.broadcasted_iota(jnp.int32, (rows, LANES), 1)
    return jnp.where((lane % N_HEADS == h) & (lane < 3 * N_HEADS), -1.0, 0.0).astype(BF16)


def _fox_prompt_kernel(q_ref, k_ref, v_ref, fa_ref, gate_ref, o_ref, m_sc, acc_sc, s0_sc, s1_sc, *,
                       tq, tk, hps):
    qi = pl.program_id(2)
    heads = range(hps)
    cols = [slice(hh * HEAD_DIM, (hh + 1) * HEAD_DIM) for hh in heads]
    m_sc[...] = jnp.full(m_sc.shape, NEG, F32)
    acc_sc[...] = jnp.zeros(acc_sc.shape, F32)
    ones = jnp.ones((ONES_ROWS, tk), BF16)
    qa = [jnp.concatenate([q_ref[0, :, cols[hh]], _forget_query_cols(pl.program_id(1) * hps + hh, tq)], axis=1)
          for hh in heads]

    def scores(j, s_ref):
        off = pl.multiple_of(j * tk, tk)
        fa = fa_ref[0, pl.ds(off, tk), :]
        for hh in heads:
            ka = jnp.concatenate([k_ref[0, pl.ds(off, tk), cols[hh]], fa], axis=1)
            s_ref[hh] = lax.dot_general(ka, qa[hh], (((1,), (1,)), ((), ())),
                                        preferred_element_type=F32)

    def absorb(j, s_ref, masked):
        off = pl.multiple_of(j * tk, tk)
        for hh in heads:
            s = s_ref[hh]
            if masked:
                kpos = off + lax.broadcasted_iota(jnp.int32, (tk, tq), 0)
                qpos = qi * tq + lax.broadcasted_iota(jnp.int32, (tk, tq), 1)
                s = jnp.where(kpos <= qpos, s, NEG)
            m_prev = m_sc[hh]
            m_new = jnp.maximum(m_prev, jnp.max(s, axis=0, keepdims=True))
            alpha = jnp.exp2(m_prev - m_new)
            p = jnp.exp2(s - m_new).astype(BF16)
            vt = jnp.concatenate([v_ref[0, pl.ds(off, tk), cols[hh]].T, ones], axis=0)
            acc_sc[hh] = alpha * acc_sc[hh] + jnp.dot(vt, p, preferred_element_type=F32)
            m_sc[hh] = m_new

    scores(0, s0_sc)

    def body(i, c):
        j = 2 * i
        scores(j + 1, s1_sc)
        absorb(j, s0_sc, False)
        scores(j + 2, s0_sc)
        absorb(j + 1, s1_sc, False)
        return c

    lax.fori_loop(0, qi // 2, body, 0)

    @pl.when(qi % 2 == 0)
    def _():
        absorb(qi, s0_sc, True)

    @pl.when(qi % 2 == 1)
    def _():
        scores(qi, s1_sc)
        absorb(qi - 1, s0_sc, False)
        absorb(qi, s1_sc, True)

    for hh in heads:
        acc = acc_sc[hh]
        o = (acc[:HEAD_DIM] * (1.0 / acc[HEAD_DIM:HEAD_DIM + 1])).T
        o_ref[0, :, cols[hh]] = (o * _silu(gate_ref[0, :, cols[hh]].astype(F32))).astype(BF16)


def _fox_prompt(zb, k16, v16, fa, *, B, T, tq, tk, hps, name):
    assert tq == tk and T % tq == 0 and N_HEADS % hps == 0
    w = hps * HEAD_DIM
    return pl.pallas_call(
        functools.partial(_fox_prompt_kernel, tq=tq, tk=tk, hps=hps),
        out_shape=jax.ShapeDtypeStruct((B, T, W_B), BF16),
        grid=(B, N_HEADS // hps, T // tq),
        in_specs=[
            pl.BlockSpec((1, tq, w), lambda b, h, i: (b, i, h)),
            pl.BlockSpec((1, T, w), lambda b, h, i: (b, 0, h)),
            pl.BlockSpec((1, T, w), lambda b, h, i: (b, 0, h)),
            pl.BlockSpec((1, T, LANES), lambda b, h, i: (b, 0, 0)),
            pl.BlockSpec((1, tq, w), lambda b, h, i: (b, i, N_HEADS // hps + h)),
        ],
        out_specs=pl.BlockSpec((1, tq, w), lambda b, h, i: (b, i, h)),
        scratch_shapes=[
            pltpu.VMEM((hps, 1, tq), F32),
            pltpu.VMEM((hps, HEAD_DIM + ONES_ROWS, tq), F32),
            pltpu.VMEM((hps, tk, tq), F32),
            pltpu.VMEM((hps, tk, tq), F32),
        ],
        compiler_params=_params("parallel", "parallel", "arbitrary"),
        name=name,
    )(zb, k16, v16, fa, zb)


DEC_STRIDE = 4
DEC_GROUP = N_HEADS // DEC_STRIDE


def _fox_decode_kernel(z_ref, kc_ref, vc_ref, kn_ref, vn_ref, fa_ref, far_ref, o_ref, m_sc, l_sc, acc_sc, *,
                       P, T, pc):
    c = pl.program_id(1)
    nt = (((1,), (1,)), ((), ()))
    heads = [(g, i, g + DEC_STRIDE * i) for g in range(DEC_STRIDE) for i in range(DEC_GROUP)]

    @pl.when(c == 0)
    def _():
        m_sc[...] = jnp.full(m_sc.shape, NEG, F32)
        l_sc[...] = jnp.zeros(l_sc.shape, F32)
        acc_sc[...] = jnp.zeros(acc_sc.shape, F32)

    def qa_of(h):
        return jnp.concatenate([z_ref[:, h * HEAD_DIM:(h + 1) * HEAD_DIM], _forget_query_cols(h, T)], axis=1)

    def update(s, pv_of):
        m_prev = m_sc[...]
        m_new = jnp.maximum(m_prev, jnp.max(s, axis=-1, keepdims=True))
        alpha = jnp.exp2(m_prev - m_new)
        p = jnp.exp2(s - m_new)
        l_sc[...] = alpha * l_sc[...] + jnp.sum(p, axis=-1, keepdims=True)
        acc_sc[...] = alpha * acc_sc[...] + pv_of(p.astype(BF16))
        m_sc[...] = m_new

    n = DEC_GROUP * pc
    gq = DEC_GROUP * T
    far = far_ref[0, pl.ds(pl.multiple_of(c * n, n), n), :]
    s = jnp.concatenate(
        [lax.dot_general(jnp.concatenate([qa_of(g + DEC_STRIDE * i) for i in range(DEC_GROUP)], axis=0),
                         jnp.concatenate([kc_ref[0, pl.ds(g, n, stride=DEC_STRIDE), :].astype(BF16), far], axis=1),
                         nt, preferred_element_type=F32) for g in range(DEC_STRIDE)], axis=0)
    row_i = (lax.broadcasted_iota(jnp.int32, s.shape, 0) // T) % DEC_GROUP
    col_i = lax.broadcasted_iota(jnp.int32, s.shape, 1) % DEC_GROUP
    update(jnp.where(row_i == col_i, s, NEG),
           lambda pb: jnp.concatenate(
               [jnp.dot(pb[g * gq:(g + 1) * gq, :], vc_ref[0, pl.ds(g, n, stride=DEC_STRIDE), :].astype(BF16),
                        preferred_element_type=F32) for g in range(DEC_STRIDE)], axis=0))

    @pl.when(c == pl.num_programs(1) - 1)
    def _():
        rows = lax.broadcasted_iota(jnp.int32, (N_HEADS * T, T), 0) % T
        cols = lax.broadcasted_iota(jnp.int32, (N_HEADS * T, T), 1)
        fa_n = fa_ref[0, P:, :]
        s_n = jnp.concatenate(
            [lax.dot_general(qa_of(h), jnp.concatenate([kn_ref[:, h * HEAD_DIM:(h + 1) * HEAD_DIM], fa_n], axis=1),
                             nt, preferred_element_type=F32) for _, _, h in heads], axis=0)
        update(jnp.where(cols <= rows, s_n, NEG),
               lambda pb: jnp.concatenate(
                   [jnp.dot(pb[r * T:(r + 1) * T, :], vn_ref[:, h * HEAD_DIM:(h + 1) * HEAD_DIM],
                            preferred_element_type=F32) for r, (_, _, h) in enumerate(heads)], axis=0))
        o = acc_sc[...] * (1.0 / l_sc[...])
        for r, (_, _, h) in enumerate(heads):
            hc = slice(h * HEAD_DIM, (h + 1) * HEAD_DIM)
            gate = z_ref[:, W_B + h * HEAD_DIM:W_B + (h + 1) * HEAD_DIM].astype(F32)
            o_ref[:, hc] = (o[r * T:(r + 1) * T, :] * _silu(gate)).astype(BF16)


def _fox_decode(zb, k16, v16, cache_k, cache_v, fa, fa_rep, *, B, T, P, pc, name):
    c_spec = pl.BlockSpec((1, pc * N_HEADS, HEAD_DIM), lambda b, c: (b, c, 0))
    n_spec = pl.BlockSpec((T, W_B), lambda b, c: (b, 0))
    return pl.pallas_call(
        functools.partial(_fox_decode_kernel, P=P, T=T, pc=pc),
        out_shape=jax.ShapeDtypeStruct((B * T, W_B), BF16),
        grid=(B, P // pc),
        in_specs=[
            pl.BlockSpec((T, 2 * W_B), lambda b, c: (b, 0)),
            c_spec, c_spec, n_spec, n_spec,
            pl.BlockSpec((1, P + T, LANES), lambda b, c: (b, 0, 0)),
            pl.BlockSpec((1, P * DEC_GROUP, LANES), lambda b, c: (b, 0, 0)),
        ],
        out_specs=n_spec,
        scratch_shapes=[
            pltpu.VMEM((N_HEADS * T, 1), F32),
            pltpu.VMEM((N_HEADS * T, 1), F32),
            pltpu.VMEM((N_HEADS * T, HEAD_DIM), F32),
        ],
        compiler_params=_params("parallel", "arbitrary"),
        name=name,
    )(zb, cache_k, cache_v, k16, v16, fa, fa_rep)


def _trunk(x3, pos0, pool_prev, past, wts, tag):
    B, T, D = x3.shape
    M = B * T
    x = x3.reshape(M, D)
    prompt = pool_prev is None
    tm = 1024 if prompt else M
    tm_res = 512 if prompt else M
    E = wts["w_out_a"][0].shape[0]
    q_scale = HEAD_DIM ** -0.5 * LOG2E

    new_pool = []
    for l in range(len(wts["w_in_a"])):
        z = _norm_proj(x, wts["norm_a"][l], wts["w_in_a"][l], tm=tm, tn=1024, name=f"in_a{l}_{tag}")
        u3 = z[:, :E].reshape(B, T, E) if not prompt else None
        if prompt:
            t = _pool_prompt(z, wts["w_grp_a"][l], wts["scale_a"][l], T=T, tm=256, name=f"pool{l}_{tag}")
            new_pool.append(z.reshape(B, T, 2 * E)[:, T - POOL_PAD:, :E].astype(F32))
        else:
            hist = jnp.pad(pool_prev[l].astype(F32), ((0, 0), (HALO - POOL_PAD, 0), (0, 0)))
            full = jnp.concatenate([hist, u3.astype(F32)], axis=1)
            t = _pool_sample(full.reshape(B * (HALO + T), E), z, wts["w_grp_a"][l], wts["scale_a"][l],
                             nseq=B, T=T, pos0=pos0, name=f"pool{l}_{tag}")
            new_pool.append(full[:, HALO + T - POOL_PAD:, :])
        x = _proj_res(t, wts["w_out_a"][l], x, tm=tm_res, tn=1024, name=f"out_a{l}_{tag}")

    k32, v32, k16, v16, logf_rep = _kv_proj(x, wts["norm_kv"], wts["w_k"], wts["w_v"], wts["w_f_rep"],
                                            wts["b_f_rep"], tm=256, name=f"kv_{tag}")
    logf3 = logf_rep[:, :N_HEADS].reshape(B, T, N_HEADS)
    logf_rep = logf_rep.reshape(B, T, LANES)
    if prompt:
        tq = tk = 512
    else:
        past_k, past_v, past_logf = past
        P = past_k.shape[1]
        past_rep = jnp.pad(jnp.tile(past_logf.astype(F32), (1, 1, 3)), ((0, 0), (0, 0), (0, LANES - 3 * N_HEADS)))
        logf_rep = jnp.concatenate([past_rep, logf_rep], axis=1)
        ck = past_k.reshape(B, P * N_HEADS, HEAD_DIM)
        cv = past_v.reshape(B, P * N_HEADS, HEAD_DIM)
    fa = _forget_cols(logf_rep, name=f"fcols_{tag}")
    if not prompt:
        fa_rep = jnp.repeat(fa[:, :P], DEC_GROUP, axis=1)

    n_b = len(wts["w_in_b"])
    for l in range(n_b):
        zb = _norm_proj(x, wts["norm_b"][l], wts["w_in_b"][l], tm=tm, tn=1024,
                        n_scaled=W_B // 1024, scale=q_scale, name=f"in_b{l}_{tag}")
        if prompt:
            og = _fox_prompt(zb.reshape(B, T, 2 * W_B), k16.reshape(B, T, W_B), v16.reshape(B, T, W_B), fa,
                             B=B, T=T, tq=tq, tk=tk, hps=4, name=f"attn{l}_{tag}").reshape(M, W_B)
        else:
            og = _fox_decode(zb, k16, v16, ck, cv, fa, fa_rep, B=B, T=T, P=P, pc=512, name=f"attn{l}_{tag}")
        if l + 1 < n_b:
            x = _proj_res(og, wts["w_out_b"][l], x, tm=tm_res, tn=1024, name=f"out_b{l}_{tag}")
        else:
            y = _proj_res_norm(og, wts["w_out_b"][l], x, wts["norm_f"], tm=min(tm_res, 512),
                               name=f"out_b{l}_{tag}")
    return (y.reshape(B, T, D), k32.reshape(B, T, N_HEADS, HEAD_DIM), v32.reshape(B, T, N_HEADS, HEAD_DIM),
            logf3, jnp.stack(new_pool))


def kernel(x_prompt, x_sample, cache_k, cache_v, cache_logf, state_pool, norm_a, w_in_a, w_grp_a, scale_a,
           w_out_a, norm_kv, w_kv, b_f, norm_b, w_in_b, w_out_b, norm_f):
    def per_layer(w):
        return [w[l].astype(BF16) for l in range(w.shape[0])]

    wts = dict(
        norm_a=norm_a, w_in_a=per_layer(w_in_a), w_grp_a=per_layer(w_grp_a), scale_a=scale_a,
        w_out_a=per_layer(w_out_a), norm_kv=norm_kv,
        w_k=w_kv[:, :W_B].astype(BF16), w_v=w_kv[:, W_B:2 * W_B].astype(BF16),
        w_f_rep=jnp.pad(jnp.tile(w_kv[:, 2 * W_B:], (1, 3)), ((0, 0), (0, LANES - 3 * N_HEADS))).astype(BF16),
        b_f_rep=jnp.pad(jnp.tile(b_f, 3), (0, LANES - 3 * N_HEADS)),
        norm_b=norm_b, w_in_b=per_layer(w_in_b), w_out_b=per_layer(w_out_b), norm_f=norm_f,
    )
    y_p, k_p, v_p, lf_p, pool_p = _trunk(x_prompt, 0, None, None, wts, "p")
    y_s, k_s, v_s, lf_s, pool_s = _trunk(x_sample, cache_k.shape[1], state_pool,
                                         (cache_k, cache_v, cache_logf), wts, "s")
    return (y_p, y_s, k_p, v_p, lf_p, pool_p, k_s, v_s, lf_s, pool_s)
```

```python
import functools

import jax
import jax.numpy as jnp
from jax import lax
from jax.experimental import pallas as pl
from jax.experimental.pallas import tpu as pltpu

F32 = jnp.float32
BF16 = jnp.bfloat16

EPS = 1e-6
N_HEADS = 16
HEAD_DIM = 128
W_B = N_HEADS * HEAD_DIM
POOL_WINDOWS = (2, 4, 8, 16)
POOL_PAD = max(POOL_WINDOWS) - 1
HALO = 16
NEG = -1e30
LOG2E = 1.4426950408889634
ONES_ROWS = 16
LANES = 128
VMEM_LIMIT = 56 * 1024 * 1024


def _params(*sem):
    return pltpu.CompilerParams(dimension_semantics=sem, vmem_limit_bytes=VMEM_LIMIT)


def _silu(g):
    return g * (1.0 / (1.0 + jnp.exp(-g)))


def _rms_rows(xf, g):
    r = lax.rsqrt(jnp.mean(xf * xf, axis=-1, keepdims=True) + EPS)
    return (xf * r) * g


def _norm_proj_kernel(x_ref, g_ref, w_ref, *refs, n_scaled, scale, dual):
    h_ref = refs[-1]
    j = pl.program_id(1)

    @pl.when(j == 0)
    def _():
        h_ref[...] = _rms_rows(x_ref[...], g_ref[...]).astype(BF16)

    acc = jnp.dot(h_ref[...], w_ref[...], preferred_element_type=F32)
    if n_scaled:
        acc = acc * jnp.where(j < n_scaled, scale, 1.0)
    if dual:
        refs[0][...] = acc
        refs[1][...] = acc.astype(BF16)
    else:
        refs[0][...] = acc.astype(refs[0].dtype)


def _norm_proj(x, g, w, layer, *, tm, tn, dual=False, n_scaled=0, scale=1.0, name):
    M, D = x.shape
    N = w.shape[2]
    o_spec = pl.BlockSpec((tm, tn), lambda i, j: (i, j))
    if dual:
        out_shape = (jax.ShapeDtypeStruct((M, N), F32), jax.ShapeDtypeStruct((M, N), BF16))
        out_specs = (o_spec, o_spec)
    else:
        out_shape = jax.ShapeDtypeStruct((M, N), BF16)
        out_specs = o_spec
    return pl.pallas_call(
        functools.partial(_norm_proj_kernel, n_scaled=n_scaled, scale=scale, dual=dual),
        out_shape=out_shape,
        grid=(M // tm, N // tn),
        in_specs=[
            pl.BlockSpec((tm, D), lambda i, j: (i, 0)),
            pl.BlockSpec((1, D), lambda i, j: (0, 0)),
            pl.BlockSpec((None, D, tn), lambda i, j: (layer, 0, j)),
        ],
        out_specs=out_specs,
        scratch_shapes=[pltpu.VMEM((tm, D), BF16)],
        compiler_params=_params("parallel", "arbitrary"),
        name=name,
    )(x, g.reshape(1, D), w)


def _proj_res_kernel(a_ref, w_ref, x_ref, o_ref):
    o_ref[...] = x_ref[...] + jnp.dot(a_ref[...], w_ref[...], preferred_element_type=F32)


def _proj_res(a, w, layer, x, *, tm, tn, name):
    M, K = a.shape
    N = w.shape[2]
    return pl.pallas_call(
        _proj_res_kernel,
        out_shape=jax.ShapeDtypeStruct((M, N), F32),
        grid=(N // tn, M // tm),
        in_specs=[
            pl.BlockSpec((tm, K), lambda j, i: (i, 0)),
            pl.BlockSpec((None, K, tn), lambda j, i: (layer, 0, j)),
            pl.BlockSpec((tm, tn), lambda j, i: (i, j)),
        ],
        out_specs=pl.BlockSpec((tm, tn), lambda j, i: (i, j)),
        compiler_params=_params("parallel", "parallel"),
        name=name,
    )(a, w, x)


def _proj_res_norm_kernel(a_ref, w_ref, x_ref, g_ref, o_ref):
    x = x_ref[...] + jnp.dot(a_ref[...], w_ref[...], preferred_element_type=F32)
    o_ref[...] = _rms_rows(x, g_ref[...])


def _proj_res_norm(a, w, layer, x, g, *, tm, name):
    M, K = a.shape
    N = w.shape[2]
    return pl.pallas_call(
        _proj_res_norm_kernel,
        out_shape=jax.ShapeDtypeStruct((M, N), F32),
        grid=(M // tm,),
        in_specs=[
            pl.BlockSpec((tm, K), lambda i: (i, 0)),
            pl.BlockSpec((None, K, N), lambda i: (layer, 0, 0)),
            pl.BlockSpec((tm, N), lambda i: (i, 0)),
            pl.BlockSpec((1, N), lambda i: (0, 0)),
        ],
        out_specs=pl.BlockSpec((tm, N), lambda i: (i, 0)),
        compiler_params=_params("parallel"),
        name=name,
    )(a, w, x, g.reshape(1, N))


def _window_sum(full, w):
    s = full
    k = 1
    while k < w:
        s = s + pltpu.roll(s, k, 0)
        k *= 2
    return s


def _pool_finish(s, uf, inv_cnt, gate, wg, sc):
    d = s * inv_cnt - uf
    y = jnp.dot(d.astype(BF16), wg, preferred_element_type=F32) * sc
    return (y * _silu(gate)).astype(BF16)


def _pool_prompt_kernel(u_ref, halo_ref, gate_ref, wg_ref, sc_ref, o_ref, *, tm, tiles_per_seq, pos0):
    G = wg_ref.shape[-1]
    ti = pl.program_id(0) % tiles_per_seq
    t = pos0 + ti * tm + lax.broadcasted_iota(jnp.int32, (tm, 1), 0)
    for g, w in enumerate(POOL_WINDOWS):
        c = slice(g * G, (g + 1) * G)
        uf = u_ref[:, c].astype(F32)
        hf = jnp.where(ti == 0, 0.0, halo_ref[:, c].astype(F32))
        s = _window_sum(jnp.concatenate([hf, uf], axis=0), w)[HALO:, :]
        inv_cnt = 1.0 / jnp.minimum(t + 1, w).astype(F32)
        o_ref[:, c] = _pool_finish(s, uf, inv_cnt, gate_ref[:, c].astype(F32), wg_ref[g], sc_ref[:, c])


def _pool_prompt(z, wg, layer, sc, *, T, tm, name):
    M = z.shape[0]
    E = z.shape[1] // 2
    hb = tm // HALO
    return pl.pallas_call(
        functools.partial(_pool_prompt_kernel, tm=tm, tiles_per_seq=T // tm, pos0=0),
        out_shape=jax.ShapeDtypeStruct((M, E), BF16),
        grid=(M // tm,),
        in_specs=[
            pl.BlockSpec((tm, E), lambda i: (i, 0)),
            pl.BlockSpec((HALO, E), lambda i: (jnp.maximum(i * hb - 1, 0), 0)),
            pl.BlockSpec((tm, E), lambda i: (i, 1)),
            pl.BlockSpec((None,) + wg.shape[1:], lambda i: (layer, 0, 0, 0)),
            pl.BlockSpec((1, E), lambda i: (0, 0)),
        ],
        out_specs=pl.BlockSpec((tm, E), lambda i: (i, 0)),
        compiler_params=_params("parallel"),
        name=name,
    )(z, z, z, wg, sc.reshape(1, E))


def _pool_sample_kernel(full_ref, gate_ref, wg_ref, sc_ref, o_ref, *, nseq, T, pos0):
    G = wg_ref.shape[-1]
    seg = HALO + T
    t = pos0 + lax.broadcasted_iota(jnp.int32, (nseq * T, 1), 0) % T

    def tail(a):
        return a.reshape(nseq, seg, G)[:, HALO:, :].reshape(nseq * T, G)

    for g, w in enumerate(POOL_WINDOWS):
        c = slice(g * G, (g + 1) * G)
        full = full_ref[:, c]
        inv_cnt = 1.0 / jnp.minimum(t + 1, w).astype(F32)
        o_ref[:, c] = _pool_finish(tail(_window_sum(full, w)), tail(full), inv_cnt,
                                   gate_ref[:, c].astype(F32), wg_ref[g], sc_ref[:, c])


def _pool_sample(full, z, wg, layer, sc, *, nseq, T, pos0, name):
    E = full.shape[1]
    return pl.pallas_call(
        functools.partial(_pool_sample_kernel, nseq=nseq, T=T, pos0=pos0),
        out_shape=jax.ShapeDtypeStruct((nseq * T, E), BF16),
        grid=(1,),
        in_specs=[
            pl.BlockSpec(full.shape, lambda i: (0, 0)),
            pl.BlockSpec((nseq * T, E), lambda i: (0, 1)),
            pl.BlockSpec((None,) + wg.shape[1:], lambda i: (layer, 0, 0, 0)),
            pl.BlockSpec((1, E), lambda i: (0, 0)),
        ],
        out_specs=pl.BlockSpec((nseq * T, E), lambda i: (0, 0)),
        compiler_params=_params("arbitrary"),
        name=name,
    )(full, z, wg, sc.reshape(1, E))


def _kv_kernel(x_ref, g_ref, wk_ref, wv_ref, wf_ref, bf_ref, k32_ref, v32_ref, k16_ref, v16_ref, lf_ref, *, tn):
    tm = x_ref.shape[0]
    h = _rms_rows(x_ref[...], g_ref[...]).astype(BF16)
    for w_ref, o32_ref, o16_ref in ((wk_ref, k32_ref, k16_ref), (wv_ref, v32_ref, v16_ref)):
        for c in range(w_ref.shape[1] // tn):
            cs = slice(c * tn, (c + 1) * tn)
            acc = jnp.dot(h, w_ref[:, cs], preferred_element_type=F32)
            o16_ref[:, cs] = acc.astype(BF16)
            for hd in range(tn // HEAD_DIM):
                head = c * (tn // HEAD_DIM) + hd
                o32_ref[pl.ds(head, tm, stride=N_HEADS), :] = acc[:, hd * HEAD_DIM:(hd + 1) * HEAD_DIM]
    a = jnp.dot(h, wf_ref[...], preferred_element_type=F32) + bf_ref[...]
    lf_ref[...] = jnp.minimum(a, 0.0) - jnp.log1p(jnp.exp(-jnp.abs(a)))


def _kv_proj(x, g, wk, wv, wf, bf, *, tm, name):
    M, D = x.shape
    N = wk.shape[1]
    L = wf.shape[1]
    resident = dict(pipeline_mode=pl.Buffered(1))
    row = lambda i: (i, 0)
    fixed = lambda i: (0, 0)
    return pl.pallas_call(
        functools.partial(_kv_kernel, tn=512),
        out_shape=(jax.ShapeDtypeStruct((M * N_HEADS, HEAD_DIM), F32),
                   jax.ShapeDtypeStruct((M * N_HEADS, HEAD_DIM), F32),
                   jax.ShapeDtypeStruct((M, N), BF16), jax.ShapeDtypeStruct((M, N), BF16),
                   jax.ShapeDtypeStruct((M, L), F32)),
        grid=(M // tm,),
        in_specs=[
            pl.BlockSpec((tm, D), row),
            pl.BlockSpec((1, D), fixed),
            pl.BlockSpec((D, N), fixed, **resident),
            pl.BlockSpec((D, N), fixed, **resident),
            pl.BlockSpec((D, L), fixed, **resident),
            pl.BlockSpec((1, L), fixed),
        ],
        out_specs=(pl.BlockSpec((tm * N_HEADS, HEAD_DIM), row), pl.BlockSpec((tm * N_HEADS, HEAD_DIM), row),
                   pl.BlockSpec((tm, N), row), pl.BlockSpec((tm, N), row), pl.BlockSpec((tm, L), row)),
        compiler_params=_params("parallel"),
        name=name,
    )(x, g.reshape(1, D), wk, wv, wf, bf.reshape(1, L))


def _forget_cols_kernel(x_ref, o_ref):
    x = x_ref[0]
    n = x.shape[0]
    row = lax.broadcasted_iota(jnp.int32, x.shape, 0)
    lane = lax.broadcasted_iota(jnp.int32, x.shape, 1)
    k = 1
    while k < n:
        x = x + jnp.where(row >= k, pltpu.roll(x, k, 0), 0.0)
        k *= 2
    x = x * LOG2E
    r1 = x - x.astype(BF16).astype(F32)
    r2 = r1 - r1.astype(BF16).astype(F32)
    piece = jnp.where(lane < N_HEADS, x, jnp.where(lane < 2 * N_HEADS, r1, r2))
    o_ref[0] = jnp.where(lane < 3 * N_HEADS, piece, 0.0).astype(BF16)


def _forget_cols(logf_rep, *, name):
    B, n, L = logf_rep.shape
    return pl.pallas_call(
        _forget_cols_kernel,
        out_shape=jax.ShapeDtypeStruct((B, n, L), BF16),
        grid=(B,),
        in_specs=[pl.BlockSpec((1, n, L), lambda b: (b, 0, 0))],
        out_specs=pl.BlockSpec((1, n, L), lambda b: (b, 0, 0)),
        compiler_params=_params("parallel"),
        name=name,
    )(logf_rep)


def _forget_query_cols(h, rows):
    lane = lax.broadcasted_iota(jnp.int32, (rows, LANES), 1)
    return jnp.where((lane % N_HEADS == h) & (lane < 3 * N_HEADS), -1.0, 0.0).astype(BF16)


UNROLL = 4


def _block_pairs(n):
    low = [(i, j) for i in range(n) for j in range(i)]
    diag = [(i, i) for i in range(n)]
    assert len(low) % UNROLL == 0 and len(diag) % UNROLL == 0
    pairs = low + diag + [diag[-1]]
    return len(low), jnp.asarray([p[0] for p in pairs], jnp.int32), jnp.asarray([p[1] for p in pairs], jnp.int32)


def _fox_stream_kernel(qt_ref, kb_ref, q_ref, k_ref, v_ref, fa_ref, gate_ref, o_ref, m_sc, acc_sc, s0_sc, s1_sc, *,
                       tq, tk, hps, n_low, n_pairs):
    heads = range(hps)
    cols = [slice(hh * HEAD_DIM, (hh + 1) * HEAD_DIM) for hh in heads]
    m_sc[...] = jnp.full(m_sc.shape, NEG, F32)
    acc_sc[...] = jnp.zeros(acc_sc.shape, F32)
    ones = jnp.ones((ONES_ROWS, tk), BF16)
    fq = [_forget_query_cols(pl.program_id(1) * hps + hh, tq) for hh in heads]

    def scores(p, s_ref):
        qoff = pl.multiple_of(qt_ref[p] * tq, tq)
        koff = pl.multiple_of(kb_ref[p] * tk, tk)
        fa = fa_ref[0, pl.ds(koff, tk), :]
        for hh in heads:
            qa = jnp.concatenate([q_ref[0, pl.ds(qoff, tq), cols[hh]], fq[hh]], axis=1)
            ka = jnp.concatenate([k_ref[0, pl.ds(koff, tk), cols[hh]], fa], axis=1)
            s_ref[hh] = lax.dot_general(ka, qa, (((1,), (1,)), ((), ())),
                                        preferred_element_type=F32)

    def absorb(p, s_ref, diagonal):
        i = qt_ref[p]
        koff = pl.multiple_of(kb_ref[p] * tk, tk)
        for hh in heads:
            s = s_ref[hh]
            if diagonal:
                s = jnp.where(lax.broadcasted_iota(jnp.int32, (tk, tq), 0)
                              <= lax.broadcasted_iota(jnp.int32, (tk, tq), 1), s, NEG)
            m_prev = m_sc[hh, i]
            m_new = jnp.maximum(m_prev, jnp.max(s, axis=0, keepdims=True))
            alpha = jnp.exp2(m_prev - m_new)
            pr = jnp.exp2(s - m_new).astype(BF16)
            vt = jnp.concatenate([v_ref[0, pl.ds(koff, tk), cols[hh]].T, ones], axis=0)
            acc = alpha * acc_sc[hh, i] + jnp.dot(vt, pr, preferred_element_type=F32)
            m_sc[hh, i] = m_new
            if diagonal:
                qoff = pl.multiple_of(i * tq, tq)
                o = (acc[:HEAD_DIM] * (1.0 / acc[HEAD_DIM:HEAD_DIM + 1])).T
                gate = gate_ref[0, pl.ds(qoff, tq), cols[hh]].astype(F32)
                o_ref[0, pl.ds(qoff, tq), cols[hh]] = (o * _silu(gate)).astype(BF16)
            else:
                acc_sc[hh, i] = acc

    def trip(diagonal):
        def body(t, c):
            p = t * UNROLL
            for u in range(UNROLL):
                cur, nxt = (s0_sc, s1_sc) if u % 2 == 0 else (s1_sc, s0_sc)
                scores(p + u + 1, nxt)
                absorb(p + u, cur, diagonal)
            return c
        return body

    scores(0, s0_sc)
    lax.fori_loop(0, n_low // UNROLL, trip(False), 0)
    lax.fori_loop(n_low // UNROLL, n_pairs // UNROLL, trip(True), 0)


def _fox_stream(zb, k16, v16, fa, *, B, T, tq, hps, name):
    assert T % tq == 0 and N_HEADS % hps == 0
    nq = T // tq
    n_low, qt, kb = _block_pairs(nq)
    w = hps * HEAD_DIM
    seq = lambda b, h, *_: (b, 0, h)
    return pl.pallas_call(
        functools.partial(_fox_stream_kernel, tq=tq, tk=tq, hps=hps, n_low=n_low, n_pairs=n_low + nq),
        out_shape=jax.ShapeDtypeStruct((B, T, W_B), BF16),
        grid_spec=pltpu.PrefetchScalarGridSpec(
            num_scalar_prefetch=2,
            grid=(B, N_HEADS // hps),
            in_specs=[
                pl.BlockSpec((1, T, w), seq),
                pl.BlockSpec((1, T, w), seq),
                pl.BlockSpec((1, T, w), seq),
                pl.BlockSpec((1, T, LANES), lambda b, h, *_: (b, 0, 0)),
                pl.BlockSpec((1, T, w), lambda b, h, *_: (b, 0, N_HEADS // hps + h)),
            ],
            out_specs=pl.BlockSpec((1, T, w), seq),
            scratch_shapes=[
                pltpu.VMEM((hps, nq, 1, tq), F32),
                pltpu.VMEM((hps, nq, HEAD_DIM + ONES_ROWS, tq), F32),
                pltpu.VMEM((hps, tq, tq), F32),
                pltpu.VMEM((hps, tq, tq), F32),
            ],
        ),
        compiler_params=_params("parallel", "parallel"),
        name=name,
    )(qt, kb, zb, k16, v16, fa, zb)


DEC_STRIDE = 4
DEC_GROUP = N_HEADS // DEC_STRIDE


def _fox_decode_kernel(z_ref, kc_ref, vc_ref, kn_ref, vn_ref, fa_ref, far_ref, o_ref, m_sc, l_sc, acc_sc, *,
                       P, T, pc):
    c = pl.program_id(1)
    nt = (((1,), (1,)), ((), ()))
    heads = [(g, i, g + DEC_STRIDE * i) for g in range(DEC_STRIDE) for i in range(DEC_GROUP)]

    @pl.when(c == 0)
    def _():
        m_sc[...] = jnp.full(m_sc.shape, NEG, F32)
        l_sc[...] = jnp.zeros(l_sc.shape, F32)
        acc_sc[...] = jnp.zeros(acc_sc.shape, F32)

    def qa_of(h):
        return jnp.concatenate([z_ref[:, h * HEAD_DIM:(h + 1) * HEAD_DIM], _forget_query_cols(h, T)], axis=1)

    def update(s, pv_of):
        m_prev = m_sc[...]
        m_new = jnp.maximum(m_prev, jnp.max(s, axis=-1, keepdims=True))
        alpha = jnp.exp2(m_prev - m_new)
        p = jnp.exp2(s - m_new)
        l_sc[...] = alpha * l_sc[...] + jnp.sum(p, axis=-1, keepdims=True)
        acc_sc[...] = alpha * acc_sc[...] + pv_of(p.astype(BF16))
        m_sc[...] = m_new

    n = DEC_GROUP * pc
    gq = DEC_GROUP * T
    far = far_ref[0, pl.ds(pl.multiple_of(c * n, n), n), :]
    s = jnp.concatenate(
        [lax.dot_general(jnp.concatenate([qa_of(g + DEC_STRIDE * i) for i in range(DEC_GROUP)], axis=0),
                         jnp.concatenate([kc_ref[0, pl.ds(g, n, stride=DEC_STRIDE), :].astype(BF16), far], axis=1),
                         nt, preferred_element_type=F32) for g in range(DEC_STRIDE)], axis=0)
    row_i = (lax.broadcasted_iota(jnp.int32, s.shape, 0) // T) % DEC_GROUP
    col_i = lax.broadcasted_iota(jnp.int32, s.shape, 1) % DEC_GROUP
    update(jnp.where(row_i == col_i, s, NEG),
           lambda pb: jnp.concatenate(
               [jnp.dot(pb[g * gq:(g + 1) * gq, :], vc_ref[0, pl.ds(g, n, stride=DEC_STRIDE), :].astype(BF16),
                        preferred_element_type=F32) for g in range(DEC_STRIDE)], axis=0))

    @pl.when(c == pl.num_programs(1) - 1)
    def _():
        rows = lax.broadcasted_iota(jnp.int32, (N_HEADS * T, T), 0) % T
        cols = lax.broadcasted_iota(jnp.int32, (N_HEADS * T, T), 1)
        fa_n = fa_ref[0, P:, :]
        s_n = jnp.concatenate(
            [lax.dot_general(qa_of(h), jnp.concatenate([kn_ref[:, h * HEAD_DIM:(h + 1) * HEAD_DIM], fa_n], axis=1),
                             nt, preferred_element_type=F32) for _, _, h in heads], axis=0)
        update(jnp.where(cols <= rows, s_n, NEG),
               lambda pb: jnp.concatenate(
                   [jnp.dot(pb[r * T:(r + 1) * T, :], vn_ref[:, h * HEAD_DIM:(h + 1) * HEAD_DIM],
                            preferred_element_type=F32) for r, (_, _, h) in enumerate(heads)], axis=0))
        o = acc_sc[...] * (1.0 / l_sc[...])
        for r, (_, _, h) in enumerate(heads):
            hc = slice(h * HEAD_DIM, (h + 1) * HEAD_DIM)
            gate = z_ref[:, W_B + h * HEAD_DIM:W_B + (h + 1) * HEAD_DIM].astype(F32)
            o_ref[:, hc] = (o[r * T:(r + 1) * T, :] * _silu(gate)).astype(BF16)


def _fox_decode(zb, k16, v16, cache_k, cache_v, fa, fa_rep, *, B, T, P, pc, name):
    c_spec = pl.BlockSpec((1, pc * N_HEADS, HEAD_DIM), lambda b, c: (b, c, 0))
    n_spec = pl.BlockSpec((T, W_B), lambda b, c: (b, 0))
    return pl.pallas_call(
        functools.partial(_fox_decode_kernel, P=P, T=T, pc=pc),
        out_shape=jax.ShapeDtypeStruct((B * T, W_B), BF16),
        grid=(B, P // pc),
        in_specs=[
            pl.BlockSpec((T, 2 * W_B), lambda b, c: (b, 0)),
            c_spec, c_spec, n_spec, n_spec,
            pl.BlockSpec((1, P + T, LANES), lambda b, c: (b, 0, 0)),
            pl.BlockSpec((1, P * DEC_GROUP, LANES), lambda b, c: (b, 0, 0)),
        ],
        out_specs=n_spec,
        scratch_shapes=[
            pltpu.VMEM((N_HEADS * T, 1), F32),
            pltpu.VMEM((N_HEADS * T, 1), F32),
            pltpu.VMEM((N_HEADS * T, HEAD_DIM), F32),
        ],
        compiler_params=_params("parallel", "arbitrary"),
        name=name,
    )(zb, cache_k, cache_v, k16, v16, fa, fa_rep)


def _trunk(x3, pos0, pool_prev, past, wts, tag):
    B, T, D = x3.shape
    M = B * T
    x = x3.reshape(M, D)
    prompt = pool_prev is None
    tm = 1024 if prompt else M
    tm_res = 512 if prompt else M
    E = wts["w_out_a"].shape[1]
    q_scale = HEAD_DIM ** -0.5 * LOG2E

    new_pool = []
    for l in range(wts["w_in_a"].shape[0]):
        z = _norm_proj(x, wts["norm_a"][l], wts["w_in_a"], l, tm=tm, tn=1024, name=f"in_a{l}_{tag}")
        u3 = z[:, :E].reshape(B, T, E) if not prompt else None
        if prompt:
            t = _pool_prompt(z, wts["w_grp_a"], l, wts["scale_a"][l], T=T, tm=256, name=f"pool{l}_{tag}")
            new_pool.append(z.reshape(B, T, 2 * E)[:, T - POOL_PAD:, :E].astype(F32))
        else:
            hist = jnp.pad(pool_prev[l].astype(F32), ((0, 0), (HALO - POOL_PAD, 0), (0, 0)))
            full = jnp.concatenate([hist, u3.astype(F32)], axis=1)
            t = _pool_sample(full.reshape(B * (HALO + T), E), z, wts["w_grp_a"], l, wts["scale_a"][l],
                             nseq=B, T=T, pos0=pos0, name=f"pool{l}_{tag}")
            new_pool.append(full[:, HALO + T - POOL_PAD:, :])
        x = _proj_res(t, wts["w_out_a"], l, x, tm=tm_res, tn=1024, name=f"out_a{l}_{tag}")

    k32, v32, k16, v16, logf_rep = _kv_proj(x, wts["norm_kv"], wts["w_k"], wts["w_v"], wts["w_f_rep"],
                                            wts["b_f_rep"], tm=256, name=f"kv_{tag}")
    logf3 = logf_rep[:, :N_HEADS].reshape(B, T, N_HEADS)
    logf_rep = logf_rep.reshape(B, T, LANES)
    if not prompt:
        past_k, past_v, past_logf = past
        P = past_k.shape[1]
        past_rep = jnp.pad(jnp.tile(past_logf.astype(F32), (1, 1, 3)), ((0, 0), (0, 0), (0, LANES - 3 * N_HEADS)))
        logf_rep = jnp.concatenate([past_rep, logf_rep], axis=1)
        ck = past_k.reshape(B, P * N_HEADS, HEAD_DIM)
        cv = past_v.reshape(B, P * N_HEADS, HEAD_DIM)
    fa = _forget_cols(logf_rep, name=f"fcols_{tag}")
    if not prompt:
        fa_rep = jnp.repeat(fa[:, :P], DEC_GROUP, axis=1)

    n_b = wts["w_in_b"].shape[0]
    for l in range(n_b):
        zb = _norm_proj(x, wts["norm_b"][l], wts["w_in_b"], l, tm=tm, tn=1024,
                        n_scaled=W_B // 1024, scale=q_scale, name=f"in_b{l}_{tag}")
        if prompt:
            og = _fox_stream(zb.reshape(B, T, 2 * W_B), k16.reshape(B, T, W_B), v16.reshape(B, T, W_B), fa,
                             B=B, T=T, tq=512, hps=2, name=f"attn{l}_{tag}").reshape(M, W_B)
        else:
            og = _fox_decode(zb, k16, v16, ck, cv, fa, fa_rep, B=B, T=T, P=P, pc=512, name=f"attn{l}_{tag}")
        if l + 1 < n_b:
            x = _proj_res(og, wts["w_out_b"], l, x, tm=tm_res, tn=1024, name=f"out_b{l}_{tag}")
        else:
            y = _proj_res_norm(og, wts["w_out_b"], l, x, wts["norm_f"], tm=min(tm_res, 512),
                               name=f"out_b{l}_{tag}")
    return (y.reshape(B, T, D), k32.reshape(B, T, N_HEADS, HEAD_DIM), v32.reshape(B, T, N_HEADS, HEAD_DIM),
            logf3, jnp.stack(new_pool))


def kernel(x_prompt, x_sample, cache_k, cache_v, cache_logf, state_pool, norm_a, w_in_a, w_grp_a, scale_a,
           w_out_a, norm_kv, w_kv, b_f, norm_b, w_in_b, w_out_b, norm_f):
    wts = dict(
        norm_a=norm_a, w_in_a=w_in_a.astype(BF16), w_grp_a=w_grp_a.astype(BF16), scale_a=scale_a,
        w_out_a=w_out_a.astype(BF16), norm_kv=norm_kv,
        w_k=w_kv[:, :W_B].astype(BF16), w_v=w_kv[:, W_B:2 * W_B].astype(BF16),
        w_f_rep=jnp.pad(jnp.tile(w_kv[:, 2 * W_B:], (1, 3)), ((0, 0), (0, LANES - 3 * N_HEADS))).astype(BF16),
        b_f_rep=jnp.pad(jnp.tile(b_f, 3), (0, LANES - 3 * N_HEADS)),
        norm_b=norm_b, w_in_b=w_in_b.astype(BF16), w_out_b=w_out_b.astype(BF16), norm_f=norm_f,
    )
    y_p, k_p, v_p, lf_p, pool_p = _trunk(x_prompt, 0, None, None, wts, "p")
    y_s, k_s, v_s, lf_s, pool_s = _trunk(x_sample, cache_k.shape[1], state_pool,
                                         (cache_k, cache_v, cache_logf), wts, "s")
    return (y_p, y_s, k_p, v_p, lf_p, pool_p, k_s, v_s, lf_s, pool_s)
```

```python
import functools

import jax
import jax.numpy as jnp
from jax import lax
from jax.experimental import pallas as pl
from jax.experimental.pallas import tpu as pltpu

F32 = jnp.float32
BF16 = jnp.bfloat16

EPS = 1e-6
N_HEADS = 16
HEAD_DIM = 128
W_B = N_HEADS * HEAD_DIM
POOL_WINDOWS = (2, 4, 8, 16)
POOL_PAD = max(POOL_WINDOWS) - 1
HALO = 16
NEG = -1e30
LOG2E = 1.4426950408889634
ONES_ROWS = 16
LANES = 128
VMEM_LIMIT = 56 * 1024 * 1024


def _params(*sem):
    return pltpu.CompilerParams(dimension_semantics=sem, vmem_limit_bytes=VMEM_LIMIT)


def _silu(g):
    return g * (1.0 / (1.0 + jnp.exp(-g)))


def _rms_rows(xf, g):
    r = lax.rsqrt(jnp.mean(xf * xf, axis=-1, keepdims=True) + EPS)
    return (xf * r) * g


def _norm_proj_kernel(x_ref, g_ref, w_ref, *refs, n_scaled, scale, dual):
    h_ref = refs[-1]
    j = pl.program_id(1)

    @pl.when(j == 0)
    def _():
        h_ref[...] = _rms_rows(x_ref[...], g_ref[...]).astype(BF16)

    acc = jnp.dot(h_ref[...], w_ref[...], preferred_element_type=F32)
    if n_scaled:
        acc = acc * jnp.where(j < n_scaled, scale, 1.0)
    if dual:
        refs[0][...] = acc
        refs[1][...] = acc.astype(BF16)
    else:
        refs[0][...] = acc.astype(refs[0].dtype)


def _norm_proj(x, g, w, layer, *, tm, tn, dual=False, n_scaled=0, scale=1.0, name):
    M, D = x.shape
    N = w.shape[2]
    o_spec = pl.BlockSpec((tm, tn), lambda i, j: (i, j))
    if dual:
        out_shape = (jax.ShapeDtypeStruct((M, N), F32), jax.ShapeDtypeStruct((M, N), BF16))
        out_specs = (o_spec, o_spec)
    else:
        out_shape = jax.ShapeDtypeStruct((M, N), BF16)
        out_specs = o_spec
    return pl.pallas_call(
        functools.partial(_norm_proj_kernel, n_scaled=n_scaled, scale=scale, dual=dual),
        out_shape=out_shape,
        grid=(M // tm, N // tn),
        in_specs=[
            pl.BlockSpec((tm, D), lambda i, j: (i, 0)),
            pl.BlockSpec((1, D), lambda i, j: (0, 0)),
            pl.BlockSpec((None, D, tn), lambda i, j: (layer, 0, j)),
        ],
        out_specs=out_specs,
        scratch_shapes=[pltpu.VMEM((tm, D), BF16)],
        compiler_params=_params("parallel", "arbitrary"),
        name=name,
    )(x, g.reshape(1, D), w)


def _proj_res_kernel(a_ref, w_ref, x_ref, o_ref):
    o_ref[...] = x_ref[...] + jnp.dot(a_ref[...], w_ref[...], preferred_element_type=F32)


def _proj_res(a, w, layer, x, *, tm, tn, name):
    M, K = a.shape
    N = w.shape[2]
    return pl.pallas_call(
        _proj_res_kernel,
        out_shape=jax.ShapeDtypeStruct((M, N), F32),
        grid=(N // tn, M // tm),
        in_specs=[
            pl.BlockSpec((tm, K), lambda j, i: (i, 0)),
            pl.BlockSpec((None, K, tn), lambda j, i: (layer, 0, j)),
            pl.BlockSpec((tm, tn), lambda j, i: (i, j)),
        ],
        out_specs=pl.BlockSpec((tm, tn), lambda j, i: (i, j)),
        compiler_params=_params("parallel", "parallel"),
        name=name,
    )(a, w, x)


def _proj_res_norm_kernel(a_ref, w_ref, x_ref, g_ref, o_ref):
    x = x_ref[...] + jnp.dot(a_ref[...], w_ref[...], preferred_element_type=F32)
    o_ref[...] = _rms_rows(x, g_ref[...])


def _proj_res_norm(a, w, layer, x, g, *, tm, name):
    M, K = a.shape
    N = w.shape[2]
    return pl.pallas_call(
        _proj_res_norm_kernel,
        out_shape=jax.ShapeDtypeStruct((M, N), F32),
        grid=(M // tm,),
        in_specs=[
            pl.BlockSpec((tm, K), lambda i: (i, 0)),
            pl.BlockSpec((None, K, N), lambda i: (layer, 0, 0)),
            pl.BlockSpec((tm, N), lambda i: (i, 0)),
            pl.BlockSpec((1, N), lambda i: (0, 0)),
        ],
        out_specs=pl.BlockSpec((tm, N), lambda i: (i, 0)),
        compiler_params=_params("parallel"),
        name=name,
    )(a, w, x, g.reshape(1, N))


def _window_sum(full, w):
    s = full
    k = 1
    while k < w:
        s = s + pltpu.roll(s, k, 0)
        k *= 2
    return s


def _pool_finish(s, uf, inv_cnt, gate, wg, sc):
    d = s * inv_cnt - uf
    y = jnp.dot(d.astype(BF16), wg, preferred_element_type=F32) * sc
    return (y * _silu(gate)).astype(BF16)


def _pool_prompt_kernel(u_ref, halo_ref, gate_ref, band_ref, wg_ref, sc_ref, o_ref, *, tm, tiles_per_seq, pos0):
    G = wg_ref.shape[-1]
    ti = pl.program_id(0) % tiles_per_seq
    t = pos0 + ti * tm + lax.broadcasted_iota(jnp.int32, (HALO, 1), 0)
    for g, w in enumerate(POOL_WINDOWS):
        c = slice(g * G, (g + 1) * G)
        u = u_ref[:, c]
        d_main = jnp.dot(band_ref[g], u, preferred_element_type=F32).astype(BF16)
        u0 = u[:HALO].astype(F32)
        h0 = jnp.where(ti == 0, 0.0, halo_ref[:, c].astype(F32))
        s0 = _window_sum(jnp.concatenate([h0, u0], axis=0), w)[HALO:, :]
        d0 = s0 * (1.0 / jnp.minimum(t + 1, w).astype(F32)) - u0
        d = jnp.concatenate([d0.astype(BF16), d_main[HALO:]], axis=0)
        y = jnp.dot(d, wg_ref[g], preferred_element_type=F32) * sc_ref[:, c]
        o_ref[:, c] = (y * _silu(gate_ref[:, c].astype(F32))).astype(BF16)


def _pool_bands(tm):
    r = jnp.arange(tm)[:, None] - jnp.arange(tm)[None, :]
    return jnp.stack([jnp.where(r == 0, 1.0 / w - 1.0, jnp.where((r > 0) & (r < w), 1.0 / w, 0.0))
                      for w in POOL_WINDOWS]).astype(BF16)


def _pool_prompt(z, wg, layer, sc, *, T, tm, name):
    M = z.shape[0]
    E = z.shape[1] // 2
    hb = tm // HALO
    bands = _pool_bands(tm)
    return pl.pallas_call(
        functools.partial(_pool_prompt_kernel, tm=tm, tiles_per_seq=T // tm, pos0=0),
        out_shape=jax.ShapeDtypeStruct((M, E), BF16),
        grid=(M // tm,),
        in_specs=[
            pl.BlockSpec((tm, E), lambda i: (i, 0)),
            pl.BlockSpec((HALO, E), lambda i: (jnp.maximum(i * hb - 1, 0), 0)),
            pl.BlockSpec((tm, E), lambda i: (i, 1)),
            pl.BlockSpec(bands.shape, lambda i: (0, 0, 0)),
            pl.BlockSpec((None,) + wg.shape[1:], lambda i: (layer, 0, 0, 0)),
            pl.BlockSpec((1, E), lambda i: (0, 0)),
        ],
        out_specs=pl.BlockSpec((tm, E), lambda i: (i, 0)),
        compiler_params=_params("parallel"),
        name=name,
    )(z, z, z, bands, wg, sc.reshape(1, E))


def _pool_sample_kernel(full_ref, gate_ref, wg_ref, sc_ref, o_ref, *, nseq, T, pos0):
    G = wg_ref.shape[-1]
    seg = HALO + T
    t = pos0 + lax.broadcasted_iota(jnp.int32, (nseq * T, 1), 0) % T

    def tail(a):
        return a.reshape(nseq, seg, G)[:, HALO:, :].reshape(nseq * T, G)

    for g, w in enumerate(POOL_WINDOWS):
        c = slice(g * G, (g + 1) * G)
        full = full_ref[:, c]
        inv_cnt = 1.0 / jnp.minimum(t + 1, w).astype(F32)
        o_ref[:, c] = _pool_finish(tail(_window_sum(full, w)), tail(full), inv_cnt,
                                   gate_ref[:, c].astype(F32), wg_ref[g], sc_ref[:, c])


def _pool_sample(full, z, wg, layer, sc, *, nseq, T, pos0, name):
    E = full.shape[1]
    return pl.pallas_call(
        functools.partial(_pool_sample_kernel, nseq=nseq, T=T, pos0=pos0),
        out_shape=jax.ShapeDtypeStruct((nseq * T, E), BF16),
        grid=(1,),
        in_specs=[
            pl.BlockSpec(full.shape, lambda i: (0, 0)),
            pl.BlockSpec((nseq * T, E), lambda i: (0, 1)),
            pl.BlockSpec((None,) + wg.shape[1:], lambda i: (layer, 0, 0, 0)),
            pl.BlockSpec((1, E), lambda i: (0, 0)),
        ],
        out_specs=pl.BlockSpec((nseq * T, E), lambda i: (0, 0)),
        compiler_params=_params("arbitrary"),
        name=name,
    )(full, z, wg, sc.reshape(1, E))


def _kv_kernel(x_ref, g_ref, wk_ref, wv_ref, wf_ref, bf_ref, k32_ref, v32_ref, k16_ref, v16_ref, lf_ref, *, tn):
    tm = x_ref.shape[0]
    h = _rms_rows(x_ref[...], g_ref[...]).astype(BF16)
    for w_ref, o32_ref, o16_ref in ((wk_ref, k32_ref, k16_ref), (wv_ref, v32_ref, v16_ref)):
        for c in range(w_ref.shape[1] // tn):
            cs = slice(c * tn, (c + 1) * tn)
            acc = jnp.dot(h, w_ref[:, cs], preferred_element_type=F32)
            o16_ref[:, cs] = acc.astype(BF16)
            for hd in range(tn // HEAD_DIM):
                head = c * (tn // HEAD_DIM) + hd
                o32_ref[pl.ds(head, tm, stride=N_HEADS), :] = acc[:, hd * HEAD_DIM:(hd + 1) * HEAD_DIM]
    a = jnp.dot(h, wf_ref[...], preferred_element_type=F32) + bf_ref[...]
    lf_ref[...] = jnp.minimum(a, 0.0) - jnp.log1p(jnp.exp(-jnp.abs(a)))


def _kv_proj(x, g, wk, wv, wf, bf, *, tm, name):
    M, D = x.shape
    N = wk.shape[1]
    L = wf.shape[1]
    resident = dict(pipeline_mode=pl.Buffered(1))
    row = lambda i: (i, 0)
    fixed = lambda i: (0, 0)
    return pl.pallas_call(
        functools.partial(_kv_kernel, tn=512),
        out_shape=(jax.ShapeDtypeStruct((M * N_HEADS, HEAD_DIM), F32),
                   jax.ShapeDtypeStruct((M * N_HEADS, HEAD_DIM), F32),
                   jax.ShapeDtypeStruct((M, N), BF16), jax.ShapeDtypeStruct((M, N), BF16),
                   jax.ShapeDtypeStruct((M, L), F32)),
        grid=(M // tm,),
        in_specs=[
            pl.BlockSpec((tm, D), row),
            pl.BlockSpec((1, D), fixed),
            pl.BlockSpec((D, N), fixed, **resident),
            pl.BlockSpec((D, N), fixed, **resident),
            pl.BlockSpec((D, L), fixed, **resident),
            pl.BlockSpec((1, L), fixed),
        ],
        out_specs=(pl.BlockSpec((tm * N_HEADS, HEAD_DIM), row), pl.BlockSpec((tm * N_HEADS, HEAD_DIM), row),
                   pl.BlockSpec((tm, N), row), pl.BlockSpec((tm, N), row), pl.BlockSpec((tm, L), row)),
        compiler_params=_params("parallel"),
        name=name,
    )(x, g.reshape(1, D), wk, wv, wf, bf.reshape(1, L))


def _forget_cols_kernel(x_ref, o_ref):
    x = x_ref[0]
    n = x.shape[0]
    row = lax.broadcasted_iota(jnp.int32, x.shape, 0)
    lane = lax.broadcasted_iota(jnp.int32, x.shape, 1)
    k = 1
    while k < n:
        x = x + jnp.where(row >= k, pltpu.roll(x, k, 0), 0.0)
        k *= 2
    x = x * LOG2E
    r1 = x - x.astype(BF16).astype(F32)
    r2 = r1 - r1.astype(BF16).astype(F32)
    piece = jnp.where(lane < N_HEADS, x, jnp.where(lane < 2 * N_HEADS, r1, r2))
    o_ref[0] = jnp.where(lane < 3 * N_HEADS, piece, 0.0).astype(BF16)


def _forget_cols(logf_rep, *, name):
    B, n, L = logf_rep.shape
    return pl.pallas_call(
        _forget_cols_kernel,
        out_shape=jax.ShapeDtypeStruct((B, n, L), BF16),
        grid=(B,),
        in_specs=[pl.BlockSpec((1, n, L), lambda b: (b, 0, 0))],
        out_specs=pl.BlockSpec((1, n, L), lambda b: (b, 0, 0)),
        compiler_params=_params("parallel"),
        name=name,
    )(logf_rep)


def _forget_query_cols(h, rows):
    lane = lax.broadcasted_iota(jnp.int32, (rows, LANES), 1)
    return jnp.where((lane % N_HEADS == h) & (lane < 3 * N_HEADS), -1.0, 0.0).astype(BF16)


UNROLL = 4


def _block_pairs(n):
    low = [(i, j) for i in range(n) for j in range(i)]
    diag = [(i, i) for i in range(n)]
    assert len(low) % UNROLL == 0 and len(diag) % UNROLL == 0
    pairs = low + diag + [diag[-1]]
    return len(low), jnp.asarray([p[0] for p in pairs], jnp.int32), jnp.asarray([p[1] for p in pairs], jnp.int32)


def _fox_stream_kernel(qt_ref, kb_ref, q_ref, k_ref, v_ref, fa_ref, gate_ref, o_ref, m_sc, acc_sc, s0_sc, s1_sc, *,
                       tq, tk, hps, n_low, n_pairs):
    heads = range(hps)
    cols = [slice(hh * HEAD_DIM, (hh + 1) * HEAD_DIM) for hh in heads]
    m_sc[...] = jnp.full(m_sc.shape, NEG, F32)
    acc_sc[...] = jnp.zeros(acc_sc.shape, F32)
    ones = jnp.ones((ONES_ROWS, tk), BF16)
    fq = [_forget_query_cols(pl.program_id(1) * hps + hh, tq) for hh in heads]

    def scores(p, s_ref):
        qoff = pl.multiple_of(qt_ref[p] * tq, tq)
        koff = pl.multiple_of(kb_ref[p] * tk, tk)
        fa = fa_ref[0, pl.ds(koff, tk), :]
        for hh in heads:
            qa = jnp.concatenate([q_ref[0, pl.ds(qoff, tq), cols[hh]], fq[hh]], axis=1)
            ka = jnp.concatenate([k_ref[0, pl.ds(koff, tk), cols[hh]], fa], axis=1)
            s_ref[hh] = lax.dot_general(ka, qa, (((1,), (1,)), ((), ())),
                                        preferred_element_type=F32)

    def absorb(p, s_ref, diagonal):
        i = qt_ref[p]
        koff = pl.multiple_of(kb_ref[p] * tk, tk)
        for hh in heads:
            s = s_ref[hh]
            if diagonal:
                s = jnp.where(lax.broadcasted_iota(jnp.int32, (tk, tq), 0)
                              <= lax.broadcasted_iota(jnp.int32, (tk, tq), 1), s, NEG)
            m_prev = m_sc[hh, i]
            m_new = jnp.maximum(m_prev, jnp.max(s, axis=0, keepdims=True))
            alpha = jnp.exp2(m_prev - m_new)
            pr = jnp.exp2(s - m_new).astype(BF16)
            vt = jnp.concatenate([v_ref[0, pl.ds(koff, tk), cols[hh]].T, ones], axis=0)
            acc = alpha * acc_sc[hh, i] + jnp.dot(vt, pr, preferred_element_type=F32)
            m_sc[hh, i] = m_new
            if diagonal:
                qoff = pl.multiple_of(i * tq, tq)
                o = (acc[:HEAD_DIM] * (1.0 / acc[HEAD_DIM:HEAD_DIM + 1])).T
                gate = gate_ref[0, pl.ds(qoff, tq), cols[hh]].astype(F32)
                o_ref[0, pl.ds(qoff, tq), cols[hh]] = (o * _silu(gate)).astype(BF16)
            else:
                acc_sc[hh, i] = acc

    def trip(diagonal):
        def body(t, c):
            p = t * UNROLL
            for u in range(UNROLL):
                cur, nxt = (s0_sc, s1_sc) if u % 2 == 0 else (s1_sc, s0_sc)
                scores(p + u + 1, nxt)
                absorb(p + u, cur, diagonal)
            return c
        return body

    scores(0, s0_sc)
    lax.fori_loop(0, n_low // UNROLL, trip(False), 0)
    lax.fori_loop(n_low // UNROLL, n_pairs // UNROLL, trip(True), 0)


def _fox_stream(zb, k16, v16, fa, *, B, T, tq, hps, name):
    assert T % tq == 0 and N_HEADS % hps == 0
    nq = T // tq
    n_low, qt, kb = _block_pairs(nq)
    w = hps * HEAD_DIM
    seq = lambda b, h, *_: (b, 0, h)
    return pl.pallas_call(
        functools.partial(_fox_stream_kernel, tq=tq, tk=tq, hps=hps, n_low=n_low, n_pairs=n_low + nq),
        out_shape=jax.ShapeDtypeStruct((B, T, W_B), BF16),
        grid_spec=pltpu.PrefetchScalarGridSpec(
            num_scalar_prefetch=2,
            grid=(B, N_HEADS // hps),
            in_specs=[
                pl.BlockSpec((1, T, w), seq),
                pl.BlockSpec((1, T, w), seq),
                pl.BlockSpec((1, T, w), seq),
                pl.BlockSpec((1, T, LANES), lambda b, h, *_: (b, 0, 0)),
                pl.BlockSpec((1, T, w), lambda b, h, *_: (b, 0, N_HEADS // hps + h)),
            ],
            out_specs=pl.BlockSpec((1, T, w), seq),
            scratch_shapes=[
                pltpu.VMEM((hps, nq, 1, tq), F32),
                pltpu.VMEM((hps, nq, HEAD_DIM + ONES_ROWS, tq), F32),
                pltpu.VMEM((hps, tq, tq), F32),
                pltpu.VMEM((hps, tq, tq), F32),
            ],
        ),
        compiler_params=_params("parallel", "parallel"),
        name=name,
    )(qt, kb, zb, k16, v16, fa, zb)


DEC_STRIDE = 4
DEC_GROUP = N_HEADS // DEC_STRIDE


def _fox_decode_kernel(z_ref, kc_ref, vc_ref, kn_ref, vn_ref, fa_ref, far_ref, o_ref, m_sc, l_sc, acc_sc, *,
                       P, T, pc):
    c = pl.program_id(1)
    nt = (((1,), (1,)), ((), ()))
    heads = [(g, i, g + DEC_STRIDE * i) for g in range(DEC_STRIDE) for i in range(DEC_GROUP)]

    @pl.when(c == 0)
    def _():
        m_sc[...] = jnp.full(m_sc.shape, NEG, F32)
        l_sc[...] = jnp.zeros(l_sc.shape, F32)
        acc_sc[...] = jnp.zeros(acc_sc.shape, F32)

    def qa_of(h):
        return jnp.concatenate([z_ref[:, h * HEAD_DIM:(h + 1) * HEAD_DIM], _forget_query_cols(h, T)], axis=1)

    def update(s, pv_of):
        m_prev = m_sc[...]
        m_new = jnp.maximum(m_prev, jnp.max(s, axis=-1, keepdims=True))
        alpha = jnp.exp2(m_prev - m_new)
        p = jnp.exp2(s - m_new)
        l_sc[...] = alpha * l_sc[...] + jnp.sum(p, axis=-1, keepdims=True)
        acc_sc[...] = alpha * acc_sc[...] + pv_of(p.astype(BF16))
        m_sc[...] = m_new

    n = DEC_GROUP * pc
    gq = DEC_GROUP * T
    far = far_ref[0, pl.ds(pl.multiple_of(c * n, n), n), :]
    s = jnp.concatenate(
        [lax.dot_general(jnp.concatenate([qa_of(g + DEC_STRIDE * i) for i in range(DEC_GROUP)], axis=0),
                         jnp.concatenate([kc_ref[0, pl.ds(g, n, stride=DEC_STRIDE), :].astype(BF16), far], axis=1),
                         nt, preferred_element_type=F32) for g in range(DEC_STRIDE)], axis=0)
    row_i = (lax.broadcasted_iota(jnp.int32, s.shape, 0) // T) % DEC_GROUP
    col_i = lax.broadcasted_iota(jnp.int32, s.shape, 1) % DEC_GROUP
    update(jnp.where(row_i == col_i, s, NEG),
           lambda pb: jnp.concatenate(
               [jnp.dot(pb[g * gq:(g + 1) * gq, :], vc_ref[0, pl.ds(g, n, stride=DEC_STRIDE), :].astype(BF16),
                        preferred_element_type=F32) for g in range(DEC_STRIDE)], axis=0))

    @pl.when(c == pl.num_programs(1) - 1)
    def _():
        rows = lax.broadcasted_iota(jnp.int32, (N_HEADS * T, T), 0) % T
        cols = lax.broadcasted_iota(jnp.int32, (N_HEADS * T, T), 1)
        fa_n = fa_ref[0, P:, :]
        s_n = jnp.concatenate(
            [lax.dot_general(qa_of(h), jnp.concatenate([kn_ref[:, h * HEAD_DIM:(h + 1) * HEAD_DIM], fa_n], axis=1),
                             nt, preferred_element_type=F32) for _, _, h in heads], axis=0)
        update(jnp.where(cols <= rows, s_n, NEG),
               lambda pb: jnp.concatenate(
                   [jnp.dot(pb[r * T:(r + 1) * T, :], vn_ref[:, h * HEAD_DIM:(h + 1) * HEAD_DIM],
                            preferred_element_type=F32) for r, (_, _, h) in enumerate(heads)], axis=0))
        o = acc_sc[...] * (1.0 / l_sc[...])
        for r, (_, _, h) in enumerate(heads):
            hc = slice(h * HEAD_DIM, (h + 1) * HEAD_DIM)
            gate = z_ref[:, W_B + h * HEAD_DIM:W_B + (h + 1) * HEAD_DIM].astype(F32)
            o_ref[:, hc] = (o[r * T:(r + 1) * T, :] * _silu(gate)).astype(BF16)


def _fox_decode(zb, k16, v16, cache_k, cache_v, fa, fa_rep, *, B, T, P, pc, name):
    c_spec = pl.BlockSpec((1, pc * N_HEADS, HEAD_DIM), lambda b, c: (b, c, 0))
    n_spec = pl.BlockSpec((T, W_B), lambda b, c: (b, 0))
    return pl.pallas_call(
        functools.partial(_fox_decode_kernel, P=P, T=T, pc=pc),
        out_shape=jax.ShapeDtypeStruct((B * T, W_B), BF16),
        grid=(B, P // pc),
        in_specs=[
            pl.BlockSpec((T, 2 * W_B), lambda b, c: (b, 0)),
            c_spec, c_spec, n_spec, n_spec,
            pl.BlockSpec((1, P + T, LANES), lambda b, c: (b, 0, 0)),
            pl.BlockSpec((1, P * DEC_GROUP, LANES), lambda b, c: (b, 0, 0)),
        ],
        out_specs=n_spec,
        scratch_shapes=[
            pltpu.VMEM((N_HEADS * T, 1), F32),
            pltpu.VMEM((N_HEADS * T, 1), F32),
            pltpu.VMEM((N_HEADS * T, HEAD_DIM), F32),
        ],
        compiler_params=_params("parallel", "arbitrary"),
        name=name,
    )(zb, cache_k, cache_v, k16, v16, fa, fa_rep)


def _trunk(x3, pos0, pool_prev, past, wts, tag):
    B, T, D = x3.shape
    M = B * T
    x = x3.reshape(M, D)
    prompt = pool_prev is None
    tm = 1024 if prompt else M
    tm_res = 1024 if prompt else M
    tn = 1024 if prompt else 2048
    E = wts["w_out_a"].shape[1]
    q_scale = HEAD_DIM ** -0.5 * LOG2E

    new_pool = []
    for l in range(wts["w_in_a"].shape[0]):
        z = _norm_proj(x, wts["norm_a"][l], wts["w_in_a"], l, tm=tm, tn=tn, name=f"in_a{l}_{tag}")
        u3 = z[:, :E].reshape(B, T, E) if not prompt else None
        if prompt:
            t = _pool_prompt(z, wts["w_grp_a"], l, wts["scale_a"][l], T=T, tm=256, name=f"pool{l}_{tag}")
            new_pool.append(z.reshape(B, T, 2 * E)[:, T - POOL_PAD:, :E].astype(F32))
        else:
            hist = jnp.pad(pool_prev[l].astype(F32), ((0, 0), (HALO - POOL_PAD, 0), (0, 0)))
            full = jnp.concatenate([hist, u3.astype(F32)], axis=1)
            t = _pool_sample(full.reshape(B * (HALO + T), E), z, wts["w_grp_a"], l, wts["scale_a"][l],
                             nseq=B, T=T, pos0=pos0, name=f"pool{l}_{tag}")
            new_pool.append(full[:, HALO + T - POOL_PAD:, :])
        x = _proj_res(t, wts["w_out_a"], l, x, tm=tm_res, tn=tn, name=f"out_a{l}_{tag}")

    k32, v32, k16, v16, logf_rep = _kv_proj(x, wts["norm_kv"], wts["w_k"], wts["w_v"], wts["w_f_rep"],
                                            wts["b_f_rep"], tm=min(M, 512), name=f"kv_{tag}")
    logf3 = logf_rep[:, :N_HEADS].reshape(B, T, N_HEADS)
    logf_rep = logf_rep.reshape(B, T, LANES)
    if not prompt:
        past_k, past_v, past_logf = past
        P = past_k.shape[1]
        past_rep = jnp.pad(jnp.tile(past_logf.astype(F32), (1, 1, 3)), ((0, 0), (0, 0), (0, LANES - 3 * N_HEADS)))
        logf_rep = jnp.concatenate([past_rep, logf_rep], axis=1)
        ck = past_k.reshape(B, P * N_HEADS, HEAD_DIM)
        cv = past_v.reshape(B, P * N_HEADS, HEAD_DIM)
    fa = _forget_cols(logf_rep, name=f"fcols_{tag}")
    if not prompt:
        fa_rep = jnp.repeat(fa[:, :P], DEC_GROUP, axis=1)

    n_b = wts["w_in_b"].shape[0]
    for l in range(n_b):
        zb = _norm_proj(x, wts["norm_b"][l], wts["w_in_b"], l, tm=tm, tn=tn,
                        n_scaled=W_B // tn, scale=q_scale, name=f"in_b{l}_{tag}")
        if prompt:
            og = _fox_stream(zb.reshape(B, T, 2 * W_B), k16.reshape(B, T, W_B), v16.reshape(B, T, W_B), fa,
                             B=B, T=T, tq=512, hps=2, name=f"attn{l}_{tag}").reshape(M, W_B)
        else:
            og = _fox_decode(zb, k16, v16, ck, cv, fa, fa_rep, B=B, T=T, P=P, pc=512, name=f"attn{l}_{tag}")
        if l + 1 < n_b:
            x = _proj_res(og, wts["w_out_b"], l, x, tm=tm_res, tn=tn, name=f"out_b{l}_{tag}")
        else:
            y = _proj_res_norm(og, wts["w_out_b"], l, x, wts["norm_f"], tm=min(tm_res, 512),
                               name=f"out_b{l}_{tag}")
    return (y.reshape(B, T, D), k32.reshape(B, T, N_HEADS, HEAD_DIM), v32.reshape(B, T, N_HEADS, HEAD_DIM),
            logf3, jnp.stack(new_pool))


def kernel(x_prompt, x_sample, cache_k, cache_v, cache_logf, state_pool, norm_a, w_in_a, w_grp_a, scale_a,
           w_out_a, norm_kv, w_kv, b_f, norm_b, w_in_b, w_out_b, norm_f):
    wts = dict(
        norm_a=norm_a, w_in_a=w_in_a.astype(BF16), w_grp_a=w_grp_a.astype(BF16), scale_a=scale_a,
        w_out_a=w_out_a.astype(BF16), norm_kv=norm_kv,
        w_k=w_kv[:, :W_B].astype(BF16), w_v=w_kv[:, W_B:2 * W_B].astype(BF16),
        w_f_rep=jnp.pad(jnp.tile(w_kv[:, 2 * W_B:], (1, 3)), ((0, 0), (0, LANES - 3 * N_HEADS))).astype(BF16),
        b_f_rep=jnp.pad(jnp.tile(b_f, 3), (0, LANES - 3 * N_HEADS)),
        norm_b=norm_b, w_in_b=w_in_b.astype(BF16), w_out_b=w_out_b.astype(BF16), norm_f=norm_f,
    )
    y_p, k_p, v_p, lf_p, pool_p = _trunk(x_prompt, 0, None, None, wts, "p")
    y_s, k_s, v_s, lf_s, pool_s = _trunk(x_sample, cache_k.shape[1], state_pool,
                                         (cache_k, cache_v, cache_logf), wts, "s")
    return (y_p, y_s, k_p, v_p, lf_p, pool_p, k_s, v_s, lf_s, pool_s)
```

```python
import functools

import jax
import jax.numpy as jnp
from jax import lax
from jax.experimental import pallas as pl
from jax.experimental.pallas import tpu as pltpu

F32 = jnp.float32
BF16 = jnp.bfloat16

EPS = 1e-6
N_HEADS = 16
HEAD_DIM = 128
W_B = N_HEADS * HEAD_DIM
POOL_WINDOWS = (2, 4, 8, 16)
POOL_PAD = max(POOL_WINDOWS) - 1
HALO = 16
NEG = -1e30
LOG2E = 1.4426950408889634
ONES_ROWS = 16
LANES = 128
VMEM_LIMIT = 56 * 1024 * 1024


def _params(*sem):
    return pltpu.CompilerParams(dimension_semantics=sem, vmem_limit_bytes=VMEM_LIMIT)


def _silu(g):
    return g * (1.0 / (1.0 + jnp.exp(-g)))


def _rms_rows(xf, g):
    r = lax.rsqrt(jnp.mean(xf * xf, axis=-1, keepdims=True) + EPS)
    return (xf * r) * g


def _norm_proj_kernel(x_ref, g_ref, w_ref, o_ref, h_ref, *, n_scaled, scale):
    j = pl.program_id(1)

    @pl.when(j == 0)
    def _():
        h_ref[...] = _rms_rows(x_ref[...], g_ref[...]).astype(BF16)

    acc = jnp.dot(h_ref[...], w_ref[...], preferred_element_type=F32)
    if n_scaled:
        acc = acc * jnp.where(j < n_scaled, scale, 1.0)
    o_ref[...] = acc.astype(o_ref.dtype)


def _norm_proj(x, g, w, layer, *, tm, tn, n_scaled=0, scale=1.0, name):
    M, D = x.shape
    N = w.shape[2]
    return pl.pallas_call(
        functools.partial(_norm_proj_kernel, n_scaled=n_scaled, scale=scale),
        out_shape=jax.ShapeDtypeStruct((M, N), BF16),
        grid=(M // tm, N // tn),
        in_specs=[
            pl.BlockSpec((tm, D), lambda i, j: (i, 0)),
            pl.BlockSpec((1, D), lambda i, j: (0, 0)),
            pl.BlockSpec((None, D, tn), lambda i, j: (layer, 0, j)),
        ],
        out_specs=pl.BlockSpec((tm, tn), lambda i, j: (i, j)),
        scratch_shapes=[pltpu.VMEM((tm, D), BF16)],
        compiler_params=_params("parallel", "arbitrary"),
        name=name,
    )(x, g.reshape(1, D), w)


def _proj_res_kernel(a_ref, w_ref, x_ref, o_ref):
    o_ref[...] = x_ref[...] + jnp.dot(a_ref[...], w_ref[...], preferred_element_type=F32)


def _proj_res(a, w, layer, x, *, tm, tn, name):
    M, K = a.shape
    N = w.shape[2]
    return pl.pallas_call(
        _proj_res_kernel,
        out_shape=jax.ShapeDtypeStruct((M, N), F32),
        grid=(N // tn, M // tm),
        in_specs=[
            pl.BlockSpec((tm, K), lambda j, i: (i, 0)),
            pl.BlockSpec((None, K, tn), lambda j, i: (layer, 0, j)),
            pl.BlockSpec((tm, tn), lambda j, i: (i, j)),
        ],
        out_specs=pl.BlockSpec((tm, tn), lambda j, i: (i, j)),
        compiler_params=_params("parallel", "parallel"),
        name=name,
    )(a, w, x)


def _proj_res_norm_kernel(a_ref, w_ref, x_ref, g_ref, o_ref):
    x = x_ref[...] + jnp.dot(a_ref[...], w_ref[...], preferred_element_type=F32)
    o_ref[...] = _rms_rows(x, g_ref[...])


def _proj_res_norm(a, w, layer, x, g, *, tm, name):
    M, K = a.shape
    N = w.shape[2]
    return pl.pallas_call(
        _proj_res_norm_kernel,
        out_shape=jax.ShapeDtypeStruct((M, N), F32),
        grid=(M // tm,),
        in_specs=[
            pl.BlockSpec((tm, K), lambda i: (i, 0)),
            pl.BlockSpec((None, K, N), lambda i: (layer, 0, 0)),
            pl.BlockSpec((tm, N), lambda i: (i, 0)),
            pl.BlockSpec((1, N), lambda i: (0, 0)),
        ],
        out_specs=pl.BlockSpec((tm, N), lambda i: (i, 0)),
        compiler_params=_params("parallel"),
        name=name,
    )(a, w, x, g.reshape(1, N))


def _window_sum(full, w):
    s = full
    k = 1
    while k < w:
        s = s + pltpu.roll(s, k, 0)
        k *= 2
    return s


def _pool_finish(s, uf, inv_cnt, gate, wg, sc):
    d = s * inv_cnt - uf
    y = jnp.dot(d.astype(BF16), wg, preferred_element_type=F32) * sc
    return (y * _silu(gate)).astype(BF16)


def _pool_prompt_kernel(u_ref, halo_ref, gate_ref, band_ref, wg_ref, sc_ref, o_ref, *, tm, tiles_per_seq, pos0):
    G = wg_ref.shape[-1]
    ti = pl.program_id(0) % tiles_per_seq
    t = pos0 + ti * tm + lax.broadcasted_iota(jnp.int32, (HALO, 1), 0)
    for g, w in enumerate(POOL_WINDOWS):
        c = slice(g * G, (g + 1) * G)
        u = u_ref[:, c]
        d_main = jnp.dot(band_ref[g], u, preferred_element_type=F32).astype(BF16)
        u0 = u[:HALO].astype(F32)
        h0 = jnp.where(ti == 0, 0.0, halo_ref[:, c].astype(F32))
        s0 = _window_sum(jnp.concatenate([h0, u0], axis=0), w)[HALO:, :]
        d0 = s0 * (1.0 / jnp.minimum(t + 1, w).astype(F32)) - u0
        d = jnp.concatenate([d0.astype(BF16), d_main[HALO:]], axis=0)
        y = jnp.dot(d, wg_ref[g], preferred_element_type=F32) * sc_ref[:, c]
        o_ref[:, c] = (y * _silu(gate_ref[:, c].astype(F32))).astype(BF16)


def _pool_bands(tm):
    r = jnp.arange(tm)[:, None] - jnp.arange(tm)[None, :]
    return jnp.stack([jnp.where(r == 0, 1.0 / w - 1.0, jnp.where((r > 0) & (r < w), 1.0 / w, 0.0))
                      for w in POOL_WINDOWS]).astype(BF16)


def _pool_prompt(z, wg, layer, sc, *, T, tm, name):
    M = z.shape[0]
    E = z.shape[1] // 2
    hb = tm // HALO
    bands = _pool_bands(tm)
    return pl.pallas_call(
        functools.partial(_pool_prompt_kernel, tm=tm, tiles_per_seq=T // tm, pos0=0),
        out_shape=jax.ShapeDtypeStruct((M, E), BF16),
        grid=(M // tm,),
        in_specs=[
            pl.BlockSpec((tm, E), lambda i: (i, 0)),
            pl.BlockSpec((HALO, E), lambda i: (jnp.maximum(i * hb - 1, 0), 0)),
            pl.BlockSpec((tm, E), lambda i: (i, 1)),
            pl.BlockSpec(bands.shape, lambda i: (0, 0, 0)),
            pl.BlockSpec((None,) + wg.shape[1:], lambda i: (layer, 0, 0, 0)),
            pl.BlockSpec((1, E), lambda i: (0, 0)),
        ],
        out_specs=pl.BlockSpec((tm, E), lambda i: (i, 0)),
        compiler_params=_params("parallel"),
        name=name,
    )(z, z, z, bands, wg, sc.reshape(1, E))


def _pool_sample_kernel(full_ref, gate_ref, wg_ref, sc_ref, o_ref, *, nseq, T, pos0):
    G = wg_ref.shape[-1]
    seg = HALO + T
    t = pos0 + lax.broadcasted_iota(jnp.int32, (nseq * T, 1), 0) % T

    def tail(a):
        return a.reshape(nseq, seg, G)[:, HALO:, :].reshape(nseq * T, G)

    for g, w in enumerate(POOL_WINDOWS):
        c = slice(g * G, (g + 1) * G)
        full = full_ref[:, c]
        inv_cnt = 1.0 / jnp.minimum(t + 1, w).astype(F32)
        o_ref[:, c] = _pool_finish(tail(_window_sum(full, w)), tail(full), inv_cnt,
                                   gate_ref[:, c].astype(F32), wg_ref[g], sc_ref[:, c])


def _pool_sample(full, z, wg, layer, sc, *, nseq, T, pos0, name):
    E = full.shape[1]
    return pl.pallas_call(
        functools.partial(_pool_sample_kernel, nseq=nseq, T=T, pos0=pos0),
        out_shape=jax.ShapeDtypeStruct((nseq * T, E), BF16),
        grid=(1,),
        in_specs=[
            pl.BlockSpec(full.shape, lambda i: (0, 0)),
            pl.BlockSpec((nseq * T, E), lambda i: (0, 1)),
            pl.BlockSpec((None,) + wg.shape[1:], lambda i: (layer, 0, 0, 0)),
            pl.BlockSpec((1, E), lambda i: (0, 0)),
        ],
        out_specs=pl.BlockSpec((nseq * T, E), lambda i: (0, 0)),
        compiler_params=_params("arbitrary"),
        name=name,
    )(full, z, wg, sc.reshape(1, E))


def _kv_kernel(x_ref, g_ref, wk_ref, wv_ref, wf_ref, bf_ref, k32_ref, v32_ref, k16_ref, v16_ref, lf_ref, *, tn):
    tm = x_ref.shape[0]
    h = _rms_rows(x_ref[...], g_ref[...]).astype(BF16)
    for w_ref, o32_ref, o16_ref in ((wk_ref, k32_ref, k16_ref), (wv_ref, v32_ref, v16_ref)):
        for c in range(w_ref.shape[1] // tn):
            cs = slice(c * tn, (c + 1) * tn)
            acc = jnp.dot(h, w_ref[:, cs], preferred_element_type=F32)
            o16_ref[:, cs] = acc.astype(BF16)
            for hd in range(tn // HEAD_DIM):
                head = c * (tn // HEAD_DIM) + hd
                o32_ref[pl.ds(head, tm, stride=N_HEADS), :] = acc[:, hd * HEAD_DIM:(hd + 1) * HEAD_DIM]
    a = jnp.dot(h, wf_ref[...], preferred_element_type=F32) + bf_ref[...]
    lf_ref[...] = jnp.minimum(a, 0.0) - jnp.log1p(jnp.exp(-jnp.abs(a)))


def _kv_proj(x, g, wk, wv, wf, bf, *, tm, name):
    M, D = x.shape
    N = wk.shape[1]
    L = wf.shape[1]
    resident = dict(pipeline_mode=pl.Buffered(1))
    row = lambda i: (i, 0)
    fixed = lambda i: (0, 0)
    return pl.pallas_call(
        functools.partial(_kv_kernel, tn=512),
        out_shape=(jax.ShapeDtypeStruct((M * N_HEADS, HEAD_DIM), F32),
                   jax.ShapeDtypeStruct((M * N_HEADS, HEAD_DIM), F32),
                   jax.ShapeDtypeStruct((M, N), BF16), jax.ShapeDtypeStruct((M, N), BF16),
                   jax.ShapeDtypeStruct((M, L), F32)),
        grid=(M // tm,),
        in_specs=[
            pl.BlockSpec((tm, D), row),
            pl.BlockSpec((1, D), fixed),
            pl.BlockSpec((D, N), fixed, **resident),
            pl.BlockSpec((D, N), fixed, **resident),
            pl.BlockSpec((D, L), fixed, **resident),
            pl.BlockSpec((1, L), fixed),
        ],
        out_specs=(pl.BlockSpec((tm * N_HEADS, HEAD_DIM), row), pl.BlockSpec((tm * N_HEADS, HEAD_DIM), row),
                   pl.BlockSpec((tm, N), row), pl.BlockSpec((tm, N), row), pl.BlockSpec((tm, L), row)),
        compiler_params=_params("parallel"),
        name=name,
    )(x, g.reshape(1, D), wk, wv, wf, bf.reshape(1, L))


def _forget_cols_kernel(x_ref, o_ref):
    x = x_ref[0]
    n = x.shape[0]
    row = lax.broadcasted_iota(jnp.int32, x.shape, 0)
    lane = lax.broadcasted_iota(jnp.int32, x.shape, 1)
    k = 1
    while k < n:
        x = x + jnp.where(row >= k, pltpu.roll(x, k, 0), 0.0)
        k *= 2
    x = x * LOG2E
    r1 = x - x.astype(BF16).astype(F32)
    r2 = r1 - r1.astype(BF16).astype(F32)
    piece = jnp.where(lane < N_HEADS, x, jnp.where(lane < 2 * N_HEADS, r1, r2))
    o_ref[0] = jnp.where(lane < 3 * N_HEADS, piece, 0.0).astype(BF16)


def _forget_cols(logf_rep, *, name):
    B, n, L = logf_rep.shape
    return pl.pallas_call(
        _forget_cols_kernel,
        out_shape=jax.ShapeDtypeStruct((B, n, L), BF16),
        grid=(B,),
        in_specs=[pl.BlockSpec((1, n, L), lambda b: (b, 0, 0))],
        out_specs=pl.BlockSpec((1, n, L), lambda b: (b, 0, 0)),
        compiler_params=_params("parallel"),
        name=name,
    )(logf_rep)


def _forget_query_cols(h, rows):
    lane = lax.broadcasted_iota(jnp.int32, (rows, LANES), 1)
    return jnp.where((lane % N_HEADS == h) & (lane < 3 * N_HEADS), -1.0, 0.0).astype(BF16)


N_SCORE_BUFS = 3


def _block_pairs(n, n_trips):
    low = [(i, j) for i in range(n) for j in range(i)]
    diag = [(i, i) for i in range(n)]
    lows, diags = len(low) // n_trips, n // n_trips
    assert lows * n_trips == len(low) and diags * n_trips == n and (lows + diags) % N_SCORE_BUFS == 0
    pairs = []
    for t in range(n_trips):
        pairs += low[t * lows:(t + 1) * lows] + diag[t * diags:(t + 1) * diags]
    for i in range(n):
        assert all(pairs.index((i, j)) < pairs.index((i, i)) for j in range(i))
    pairs.append(pairs[-1])
    return lows, diags, jnp.asarray([p[0] for p in pairs], jnp.int32), jnp.asarray([p[1] for p in pairs], jnp.int32)


def _fox_stream_kernel(qt_ref, kb_ref, q_ref, k_ref, v_ref, fa_ref, gate_ref, o_ref, m_sc, acc_sc, s_sc, *,
                       tq, tk, hps, lows, diags, n_trips):
    heads = range(hps)
    cols = [slice(hh * HEAD_DIM, (hh + 1) * HEAD_DIM) for hh in heads]
    m_sc[...] = jnp.full(m_sc.shape, NEG, F32)
    acc_sc[...] = jnp.zeros(acc_sc.shape, F32)
    ones = jnp.ones((ONES_ROWS, tk), BF16)
    fq = [_forget_query_cols(pl.program_id(1) * hps + hh, tq) for hh in heads]

    def scores(p, s_ref):
        qoff = pl.multiple_of(qt_ref[p] * tq, tq)
        koff = pl.multiple_of(kb_ref[p] * tk, tk)
        fa = fa_ref[0, pl.ds(koff, tk), :]
        for hh in heads:
            qa = jnp.concatenate([q_ref[0, pl.ds(qoff, tq), cols[hh]], fq[hh]], axis=1)
            ka = jnp.concatenate([k_ref[0, pl.ds(koff, tk), cols[hh]], fa], axis=1)
            s_ref[hh] = lax.dot_general(ka, qa, (((1,), (1,)), ((), ())),
                                        preferred_element_type=F32)

    def absorb(p, s_ref, diagonal):
        i = qt_ref[p]
        koff = pl.multiple_of(kb_ref[p] * tk, tk)
        for hh in heads:
            s = s_ref[hh]
            if diagonal:
                s = jnp.where(lax.broadcasted_iota(jnp.int32, (tk, tq), 0)
                              <= lax.broadcasted_iota(jnp.int32, (tk, tq), 1), s, NEG)
            m_prev = m_sc[hh, i]
            m_new = jnp.maximum(m_prev, jnp.max(s, axis=0, keepdims=True))
            alpha = jnp.exp2(m_prev - m_new)
            pr = jnp.exp2(s - m_new).astype(BF16)
            vt = jnp.concatenate([v_ref[0, pl.ds(koff, tk), cols[hh]].T, ones], axis=0)
            acc = alpha * acc_sc[hh, i] + jnp.dot(vt, pr, preferred_element_type=F32)
            m_sc[hh, i] = m_new
            if diagonal:
                qoff = pl.multiple_of(i * tq, tq)
                o = (acc[:HEAD_DIM] * (1.0 / acc[HEAD_DIM:HEAD_DIM + 1])).T
                gate = gate_ref[0, pl.ds(qoff, tq), cols[hh]].astype(F32)
                o_ref[0, pl.ds(qoff, tq), cols[hh]] = (o * _silu(gate)).astype(BF16)
            else:
                acc_sc[hh, i] = acc

    per_trip = lows + diags

    def body(t, c):
        p = t * per_trip
        for u in range(per_trip):
            scores(p + u + 1, s_sc.at[(u + 1) % N_SCORE_BUFS])
            absorb(p + u, s_sc.at[u % N_SCORE_BUFS], u >= lows)
        return c

    scores(0, s_sc.at[0])
    lax.fori_loop(0, n_trips, body, 0)


def _fox_stream(zb, k16, v16, fa, *, B, T, tq, hps, name):
    assert T % tq == 0 and N_HEADS % hps == 0
    nq = T // tq
    n_trips = nq // 2
    lows, diags, qt, kb = _block_pairs(nq, n_trips)
    w = hps * HEAD_DIM
    seq = lambda b, h, *_: (b, 0, h)
    return pl.pallas_call(
        functools.partial(_fox_stream_kernel, tq=tq, tk=tq, hps=hps, lows=lows, diags=diags, n_trips=n_trips),
        out_shape=jax.ShapeDtypeStruct((B, T, W_B), BF16),
        grid_spec=pltpu.PrefetchScalarGridSpec(
            num_scalar_prefetch=2,
            grid=(B, N_HEADS // hps),
            in_specs=[
                pl.BlockSpec((1, T, w), seq),
                pl.BlockSpec((1, T, w), seq),
                pl.BlockSpec((1, T, w), seq),
                pl.BlockSpec((1, T, LANES), lambda b, h, *_: (b, 0, 0)),
                pl.BlockSpec((1, T, w), lambda b, h, *_: (b, 0, N_HEADS // hps + h)),
            ],
            out_specs=pl.BlockSpec((1, T, w), seq),
            scratch_shapes=[
                pltpu.VMEM((hps, nq, 1, tq), F32),
                pltpu.VMEM((hps, nq, HEAD_DIM + ONES_ROWS, tq), F32),
                pltpu.VMEM((N_SCORE_BUFS, hps, tq, tq), F32),
            ],
        ),
        compiler_params=_params("parallel", "parallel"),
        name=name,
    )(qt, kb, zb, k16, v16, fa, zb)


DEC_STRIDE = 4
DEC_GROUP = N_HEADS // DEC_STRIDE


def _fox_decode_kernel(z_ref, kc_ref, vc_ref, kn_ref, vn_ref, fa_ref, far_ref, o_ref, m_sc, l_sc, acc_sc, *,
                       P, T, pc):
    c = pl.program_id(1)
    nt = (((1,), (1,)), ((), ()))
    heads = [(g, i, g + DEC_STRIDE * i) for g in range(DEC_STRIDE) for i in range(DEC_GROUP)]

    @pl.when(c == 0)
    def _():
        m_sc[...] = jnp.full(m_sc.shape, NEG, F32)
        l_sc[...] = jnp.zeros(l_sc.shape, F32)
        acc_sc[...] = jnp.zeros(acc_sc.shape, F32)

    def qa_of(h):
        return jnp.concatenate([z_ref[:, h * HEAD_DIM:(h + 1) * HEAD_DIM], _forget_query_cols(h, T)], axis=1)

    def update(s, pv_of):
        m_prev = m_sc[...]
        m_new = jnp.maximum(m_prev, jnp.max(s, axis=-1, keepdims=True))
        alpha = jnp.exp2(m_prev - m_new)
        p = jnp.exp2(s - m_new)
        l_sc[...] = alpha * l_sc[...] + jnp.sum(p, axis=-1, keepdims=True)
        acc_sc[...] = alpha * acc_sc[...] + pv_of(p.astype(BF16))
        m_sc[...] = m_new

    n = DEC_GROUP * pc
    gq = DEC_GROUP * T
    far = far_ref[0, pl.ds(pl.multiple_of(c * n, n), n), :]
    s = jnp.concatenate(
        [lax.dot_general(jnp.concatenate([qa_of(g + DEC_STRIDE * i) for i in range(DEC_GROUP)], axis=0),
                         jnp.concatenate([kc_ref[0, pl.ds(g, n, stride=DEC_STRIDE), :].astype(BF16), far], axis=1),
                         nt, preferred_element_type=F32) for g in range(DEC_STRIDE)], axis=0)
    row_i = (lax.broadcasted_iota(jnp.int32, s.shape, 0) // T) % DEC_GROUP
    col_i = lax.broadcasted_iota(jnp.int32, s.shape, 1) % DEC_GROUP
    update(jnp.where(row_i == col_i, s, NEG),
           lambda pb: jnp.concatenate(
               [jnp.dot(pb[g * gq:(g + 1) * gq, :], vc_ref[0, pl.ds(g, n, stride=DEC_STRIDE), :].astype(BF16),
                        preferred_element_type=F32) for g in range(DEC_STRIDE)], axis=0))

    @pl.when(c == pl.num_programs(1) - 1)
    def _():
        rows = lax.broadcasted_iota(jnp.int32, (N_HEADS * T, T), 0) % T
        cols = lax.broadcasted_iota(jnp.int32, (N_HEADS * T, T), 1)
        fa_n = fa_ref[0, P:, :]
        s_n = jnp.concatenate(
            [lax.dot_general(qa_of(h), jnp.concatenate([kn_ref[:, h * HEAD_DIM:(h + 1) * HEAD_DIM], fa_n], axis=1),
                             nt, preferred_element_type=F32) for _, _, h in heads], axis=0)
        update(jnp.where(cols <= rows, s_n, NEG),
               lambda pb: jnp.concatenate(
                   [jnp.dot(pb[r * T:(r + 1) * T, :], vn_ref[:, h * HEAD_DIM:(h + 1) * HEAD_DIM],
                            preferred_element_type=F32) for r, (_, _, h) in enumerate(heads)], axis=0))
        o = acc_sc[...] * (1.0 / l_sc[...])
        for r, (_, _, h) in enumerate(heads):
            hc = slice(h * HEAD_DIM, (h + 1) * HEAD_DIM)
            gate = z_ref[:, W_B + h * HEAD_DIM:W_B + (h + 1) * HEAD_DIM].astype(F32)
            o_ref[:, hc] = (o[r * T:(r + 1) * T, :] * _silu(gate)).astype(BF16)


def _fox_decode(zb, k16, v16, cache_k, cache_v, fa, fa_rep, *, B, T, P, pc, name):
    c_spec = pl.BlockSpec((1, pc * N_HEADS, HEAD_DIM), lambda b, c: (b, c, 0))
    n_spec = pl.BlockSpec((T, W_B), lambda b, c: (b, 0))
    return pl.pallas_call(
        functools.partial(_fox_decode_kernel, P=P, T=T, pc=pc),
        out_shape=jax.ShapeDtypeStruct((B * T, W_B), BF16),
        grid=(B, P // pc),
        in_specs=[
            pl.BlockSpec((T, 2 * W_B), lambda b, c: (b, 0)),
            c_spec, c_spec, n_spec, n_spec,
            pl.BlockSpec((1, P + T, LANES), lambda b, c: (b, 0, 0)),
            pl.BlockSpec((1, P * DEC_GROUP, LANES), lambda b, c: (b, 0, 0)),
        ],
        out_specs=n_spec,
        scratch_shapes=[
            pltpu.VMEM((N_HEADS * T, 1), F32),
            pltpu.VMEM((N_HEADS * T, 1), F32),
            pltpu.VMEM((N_HEADS * T, HEAD_DIM), F32),
        ],
        compiler_params=_params("parallel", "arbitrary"),
        name=name,
    )(zb, cache_k, cache_v, k16, v16, fa, fa_rep)


def _trunk(x3, pos0, pool_prev, past, wts, tag):
    B, T, D = x3.shape
    M = B * T
    x = x3.reshape(M, D)
    prompt = pool_prev is None
    tm = 1024 if prompt else M
    tm_res = 1024 if prompt else M
    tn_in = 2048
    tn_out = 1024 if prompt else 2048
    E = wts["w_out_a"].shape[1]
    q_scale = HEAD_DIM ** -0.5 * LOG2E

    new_pool = []
    for l in range(wts["w_in_a"].shape[0]):
        z = _norm_proj(x, wts["norm_a"][l], wts["w_in_a"], l, tm=tm, tn=tn_in, name=f"in_a{l}_{tag}")
        u3 = z[:, :E].reshape(B, T, E) if not prompt else None
        if prompt:
            t = _pool_prompt(z, wts["w_grp_a"], l, wts["scale_a"][l], T=T, tm=256, name=f"pool{l}_{tag}")
            new_pool.append(z.reshape(B, T, 2 * E)[:, T - POOL_PAD:, :E].astype(F32))
        else:
            hist = jnp.pad(pool_prev[l].astype(F32), ((0, 0), (HALO - POOL_PAD, 0), (0, 0)))
            full = jnp.concatenate([hist, u3.astype(F32)], axis=1)
            t = _pool_sample(full.reshape(B * (HALO + T), E), z, wts["w_grp_a"], l, wts["scale_a"][l],
                             nseq=B, T=T, pos0=pos0, name=f"pool{l}_{tag}")
            new_pool.append(full[:, HALO + T - POOL_PAD:, :])
        x = _proj_res(t, wts["w_out_a"], l, x, tm=tm_res, tn=tn_out, name=f"out_a{l}_{tag}")

    k32, v32, k16, v16, logf_rep = _kv_proj(x, wts["norm_kv"], wts["w_k"], wts["w_v"], wts["w_f_rep"],
                                            wts["b_f_rep"], tm=min(M, 512), name=f"kv_{tag}")
    logf3 = logf_rep[:, :N_HEADS].reshape(B, T, N_HEADS)
    logf_rep = logf_rep.reshape(B, T, LANES)
    if not prompt:
        past_k, past_v, past_logf = past
        P = past_k.shape[1]
        past_rep = jnp.pad(jnp.tile(past_logf.astype(F32), (1, 1, 3)), ((0, 0), (0, 0), (0, LANES - 3 * N_HEADS)))
        logf_rep = jnp.concatenate([past_rep, logf_rep], axis=1)
        ck = past_k.reshape(B, P * N_HEADS, HEAD_DIM)
        cv = past_v.reshape(B, P * N_HEADS, HEAD_DIM)
    fa = _forget_cols(logf_rep, name=f"fcols_{tag}")
    if not prompt:
        fa_rep = jnp.repeat(fa[:, :P], DEC_GROUP, axis=1)

    n_b = wts["w_in_b"].shape[0]
    for l in range(n_b):
        zb = _norm_proj(x, wts["norm_b"][l], wts["w_in_b"], l, tm=tm, tn=tn_in,
                        n_scaled=W_B // tn_in, scale=q_scale, name=f"in_b{l}_{tag}")
        if prompt:
            og = _fox_stream(zb.reshape(B, T, 2 * W_B), k16.reshape(B, T, W_B), v16.reshape(B, T, W_B), fa,
                             B=B, T=T, tq=512, hps=2, name=f"attn{l}_{tag}").reshape(M, W_B)
        else:
            og = _fox_decode(zb, k16, v16, ck, cv, fa, fa_rep, B=B, T=T, P=P, pc=512, name=f"attn{l}_{tag}")
        if l + 1 < n_b:
            x = _proj_res(og, wts["w_out_b"], l, x, tm=tm_res, tn=tn_out, name=f"out_b{l}_{tag}")
        else:
            y = _proj_res_norm(og, wts["w_out_b"], l, x, wts["norm_f"], tm=min(tm_res, 512),
                               name=f"out_b{l}_{tag}")
    return (y.reshape(B, T, D), k32.reshape(B, T, N_HEADS, HEAD_DIM), v32.reshape(B, T, N_HEADS, HEAD_DIM),
            logf3, jnp.stack(new_pool))


def kernel(x_prompt, x_sample, cache_k, cache_v, cache_logf, state_pool, norm_a, w_in_a, w_grp_a, scale_a,
           w_out_a, norm_kv, w_kv, b_f, norm_b, w_in_b, w_out_b, norm_f):
    wts = dict(
        norm_a=norm_a, w_in_a=w_in_a.astype(BF16), w_grp_a=w_grp_a.astype(BF16), scale_a=scale_a,
        w_out_a=w_out_a.astype(BF16), norm_kv=norm_kv,
        w_k=w_kv[:, :W_B].astype(BF16), w_v=w_kv[:, W_B:2 * W_B].astype(BF16),
        w_f_rep=jnp.pad(jnp.tile(w_kv[:, 2 * W_B:], (1, 3)), ((0, 0), (0, LANES - 3 * N_HEADS))).astype(BF16),
        b_f_rep=jnp.pad(jnp.tile(b_f, 3), (0, LANES - 3 * N_HEADS)),
        norm_b=norm_b, w_in_b=w_in_b.astype(BF16), w_out_b=w_out_b.astype(BF16), norm_f=norm_f,
    )
    y_p, k_p, v_p, lf_p, pool_p = _trunk(x_prompt, 0, None, None, wts, "p")
    y_s, k_s, v_s, lf_s, pool_s = _trunk(x_sample, cache_k.shape[1], state_pool,
                                         (cache_k, cache_v, cache_logf), wts, "s")
    return (y_p, y_s, k_p, v_p, lf_p, pool_p, k_s, v_s, lf_s, pool_s)
```

```python
import functools

import jax
import jax.numpy as jnp
from jax import lax
from jax.experimental import pallas as pl
from jax.experimental.pallas import tpu as pltpu

F32 = jnp.float32
BF16 = jnp.bfloat16

EPS = 1e-6
N_HEADS = 16
HEAD_DIM = 128
W_B = N_HEADS * HEAD_DIM
POOL_WINDOWS = (2, 4, 8, 16)
POOL_PAD = max(POOL_WINDOWS) - 1
HALO = 16
NEG = -1e30
LOG2E = 1.4426950408889634
ONES_ROWS = 16
LANES = 128
VMEM_LIMIT = 56 * 1024 * 1024


def _params(*sem):
    return pltpu.CompilerParams(dimension_semantics=sem, vmem_limit_bytes=VMEM_LIMIT)


def _silu(g):
    return g * (1.0 / (1.0 + jnp.exp(-g)))


def _rms_rows(xf, g):
    r = lax.rsqrt(jnp.mean(xf * xf, axis=-1, keepdims=True) + EPS)
    return (xf * r) * g


def _norm_proj_kernel(x_ref, g_ref, w_ref, o_ref, h_ref, *, n_scaled, scale):
    j = pl.program_id(1)

    @pl.when(j == 0)
    def _():
        h_ref[...] = _rms_rows(x_ref[...], g_ref[...]).astype(BF16)

    acc = jnp.dot(h_ref[...], w_ref[...], preferred_element_type=F32)
    if n_scaled:
        acc = acc * jnp.where(j < n_scaled, scale, 1.0)
    o_ref[...] = acc.astype(o_ref.dtype)


def _norm_proj(x, g, w, layer, *, tm, tn, n_scaled=0, scale=1.0, name):
    M, D = x.shape
    N = w.shape[2]
    return pl.pallas_call(
        functools.partial(_norm_proj_kernel, n_scaled=n_scaled, scale=scale),
        out_shape=jax.ShapeDtypeStruct((M, N), BF16),
        grid=(M // tm, N // tn),
        in_specs=[
            pl.BlockSpec((tm, D), lambda i, j: (i, 0)),
            pl.BlockSpec((1, D), lambda i, j: (0, 0)),
            pl.BlockSpec((None, D, tn), lambda i, j: (layer, 0, j)),
        ],
        out_specs=pl.BlockSpec((tm, tn), lambda i, j: (i, j)),
        scratch_shapes=[pltpu.VMEM((tm, D), BF16)],
        compiler_params=_params("parallel", "arbitrary"),
        name=name,
    )(x, g.reshape(1, D), w)


def _proj_res_kernel(a_ref, w_ref, x_ref, o_ref):
    o_ref[...] = x_ref[...] + jnp.dot(a_ref[...], w_ref[...], preferred_element_type=F32)


def _proj_res(a, w, layer, x, *, tm, tn, name):
    M, K = a.shape
    N = w.shape[2]
    return pl.pallas_call(
        _proj_res_kernel,
        out_shape=jax.ShapeDtypeStruct((M, N), F32),
        grid=(N // tn, M // tm),
        in_specs=[
            pl.BlockSpec((tm, K), lambda j, i: (i, 0)),
            pl.BlockSpec((None, K, tn), lambda j, i: (layer, 0, j)),
            pl.BlockSpec((tm, tn), lambda j, i: (i, j)),
        ],
        out_specs=pl.BlockSpec((tm, tn), lambda j, i: (i, j)),
        compiler_params=_params("parallel", "parallel"),
        name=name,
    )(a, w, x)


def _proj_res_norm_kernel(a_ref, w_ref, x_ref, g_ref, o_ref):
    x = x_ref[...] + jnp.dot(a_ref[...], w_ref[...], preferred_element_type=F32)
    o_ref[...] = _rms_rows(x, g_ref[...])


def _proj_res_norm(a, w, layer, x, g, *, tm, name):
    M, K = a.shape
    N = w.shape[2]
    return pl.pallas_call(
        _proj_res_norm_kernel,
        out_shape=jax.ShapeDtypeStruct((M, N), F32),
        grid=(M // tm,),
        in_specs=[
            pl.BlockSpec((tm, K), lambda i: (i, 0)),
            pl.BlockSpec((None, K, N), lambda i: (layer, 0, 0)),
            pl.BlockSpec((tm, N), lambda i: (i, 0)),
            pl.BlockSpec((1, N), lambda i: (0, 0)),
        ],
        out_specs=pl.BlockSpec((tm, N), lambda i: (i, 0)),
        compiler_params=_params("parallel"),
        name=name,
    )(a, w, x, g.reshape(1, N))


def _window_sum(full, w):
    s = full
    k = 1
    while k < w:
        s = s + pltpu.roll(s, k, 0)
        k *= 2
    return s


def _pool_finish(s, uf, inv_cnt, gate, wg, sc):
    d = s * inv_cnt - uf
    y = jnp.dot(d.astype(BF16), wg, preferred_element_type=F32) * sc
    return (y * _silu(gate)).astype(BF16)


def _pool_prompt_kernel(u_ref, halo_ref, gate_ref, band_ref, wg_ref, sc_ref, o_ref, *, tm, tiles_per_seq, pos0):
    G = wg_ref.shape[-1]
    ti = pl.program_id(0) % tiles_per_seq
    t = pos0 + ti * tm + lax.broadcasted_iota(jnp.int32, (HALO, 1), 0)
    for g, w in enumerate(POOL_WINDOWS):
        c = slice(g * G, (g + 1) * G)
        u = u_ref[:, c]
        d_main = jnp.dot(band_ref[g], u, preferred_element_type=F32).astype(BF16)
        u0 = u[:HALO].astype(F32)
        h0 = jnp.where(ti == 0, 0.0, halo_ref[:, c].astype(F32))
        s0 = _window_sum(jnp.concatenate([h0, u0], axis=0), w)[HALO:, :]
        d0 = s0 * (1.0 / jnp.minimum(t + 1, w).astype(F32)) - u0
        d = jnp.concatenate([d0.astype(BF16), d_main[HALO:]], axis=0)
        y = jnp.dot(d, wg_ref[g], preferred_element_type=F32) * sc_ref[:, c]
        o_ref[:, c] = (y * _silu(gate_ref[:, c].astype(F32))).astype(BF16)


def _pool_bands(tm):
    r = jnp.arange(tm)[:, None] - jnp.arange(tm)[None, :]
    return jnp.stack([jnp.where(r == 0, 1.0 / w - 1.0, jnp.where((r > 0) & (r < w), 1.0 / w, 0.0))
                      for w in POOL_WINDOWS]).astype(BF16)


def _pool_prompt(z, wg, layer, sc, *, T, tm, name):
    M = z.shape[0]
    E = z.shape[1] // 2
    hb = tm // HALO
    bands = _pool_bands(tm)
    return pl.pallas_call(
        functools.partial(_pool_prompt_kernel, tm=tm, tiles_per_seq=T // tm, pos0=0),
        out_shape=jax.ShapeDtypeStruct((M, E), BF16),
        grid=(M // tm,),
        in_specs=[
            pl.BlockSpec((tm, E), lambda i: (i, 0)),
            pl.BlockSpec((HALO, E), lambda i: (jnp.maximum(i * hb - 1, 0), 0)),
            pl.BlockSpec((tm, E), lambda i: (i, 1)),
            pl.BlockSpec(bands.shape, lambda i: (0, 0, 0)),
            pl.BlockSpec((None,) + wg.shape[1:], lambda i: (layer, 0, 0, 0)),
            pl.BlockSpec((1, E), lambda i: (0, 0)),
        ],
        out_specs=pl.BlockSpec((tm, E), lambda i: (i, 0)),
        compiler_params=_params("parallel"),
        name=name,
    )(z, z, z, bands, wg, sc.reshape(1, E))


def _pool_sample_kernel(full_ref, gate_ref, wg_ref, sc_ref, o_ref, *, nseq, T, pos0):
    G = wg_ref.shape[-1]
    seg = HALO + T
    t = pos0 + lax.broadcasted_iota(jnp.int32, (nseq * T, 1), 0) % T

    def tail(a):
        return a.reshape(nseq, seg, G)[:, HALO:, :].reshape(nseq * T, G)

    for g, w in enumerate(POOL_WINDOWS):
        c = slice(g * G, (g + 1) * G)
        full = full_ref[:, c]
        inv_cnt = 1.0 / jnp.minimum(t + 1, w).astype(F32)
        o_ref[:, c] = _pool_finish(tail(_window_sum(full, w)), tail(full), inv_cnt,
                                   gate_ref[:, c].astype(F32), wg_ref[g], sc_ref[:, c])


def _pool_sample(full, z, wg, layer, sc, *, nseq, T, pos0, name):
    E = full.shape[1]
    return pl.pallas_call(
        functools.partial(_pool_sample_kernel, nseq=nseq, T=T, pos0=pos0),
        out_shape=jax.ShapeDtypeStruct((nseq * T, E), BF16),
        grid=(1,),
        in_specs=[
            pl.BlockSpec(full.shape, lambda i: (0, 0)),
            pl.BlockSpec((nseq * T, E), lambda i: (0, 1)),
            pl.BlockSpec((None,) + wg.shape[1:], lambda i: (layer, 0, 0, 0)),
            pl.BlockSpec((1, E), lambda i: (0, 0)),
        ],
        out_specs=pl.BlockSpec((nseq * T, E), lambda i: (0, 0)),
        compiler_params=_params("arbitrary"),
        name=name,
    )(full, z, wg, sc.reshape(1, E))


def _kv_kernel(x_ref, g_ref, wk_ref, wv_ref, wf_ref, bf_ref, k32_ref, v32_ref, k16_ref, v16_ref, lf_ref, *, tn):
    tm = x_ref.shape[0]
    h = _rms_rows(x_ref[...], g_ref[...]).astype(BF16)
    for w_ref, o32_ref, o16_ref in ((wk_ref, k32_ref, k16_ref), (wv_ref, v32_ref, v16_ref)):
        for c in range(w_ref.shape[1] // tn):
            cs = slice(c * tn, (c + 1) * tn)
            acc = jnp.dot(h, w_ref[:, cs], preferred_element_type=F32)
            o16_ref[:, cs] = acc.astype(BF16)
            for hd in range(tn // HEAD_DIM):
                head = c * (tn // HEAD_DIM) + hd
                o32_ref[pl.ds(head, tm, stride=N_HEADS), :] = acc[:, hd * HEAD_DIM:(hd + 1) * HEAD_DIM]
    a = jnp.dot(h, wf_ref[...], preferred_element_type=F32) + bf_ref[...]
    lf_ref[...] = jnp.minimum(a, 0.0) - jnp.log1p(jnp.exp(-jnp.abs(a)))


def _kv_proj(x, g, wk, wv, wf, bf, *, tm, name):
    M, D = x.shape
    N = wk.shape[1]
    L = wf.shape[1]
    resident = dict(pipeline_mode=pl.Buffered(1))
    row = lambda i: (i, 0)
    fixed = lambda i: (0, 0)
    return pl.pallas_call(
        functools.partial(_kv_kernel, tn=512),
        out_shape=(jax.ShapeDtypeStruct((M * N_HEADS, HEAD_DIM), F32),
                   jax.ShapeDtypeStruct((M * N_HEADS, HEAD_DIM), F32),
                   jax.ShapeDtypeStruct((M, N), BF16), jax.ShapeDtypeStruct((M, N), BF16),
                   jax.ShapeDtypeStruct((M, L), F32)),
        grid=(M // tm,),
        in_specs=[
            pl.BlockSpec((tm, D), row),
            pl.BlockSpec((1, D), fixed),
            pl.BlockSpec((D, N), fixed, **resident),
            pl.BlockSpec((D, N), fixed, **resident),
            pl.BlockSpec((D, L), fixed, **resident),
            pl.BlockSpec((1, L), fixed),
        ],
        out_specs=(pl.BlockSpec((tm * N_HEADS, HEAD_DIM), row), pl.BlockSpec((tm * N_HEADS, HEAD_DIM), row),
                   pl.BlockSpec((tm, N), row), pl.BlockSpec((tm, N), row), pl.BlockSpec((tm, L), row)),
        compiler_params=_params("parallel"),
        name=name,
    )(x, g.reshape(1, D), wk, wv, wf, bf.reshape(1, L))


def _forget_cols_kernel(x_ref, o_ref):
    x = x_ref[0]
    n = x.shape[0]
    row = lax.broadcasted_iota(jnp.int32, x.shape, 0)
    lane = lax.broadcasted_iota(jnp.int32, x.shape, 1)
    k = 1
    while k < n:
        x = x + jnp.where(row >= k, pltpu.roll(x, k, 0), 0.0)
        k *= 2
    x = x * LOG2E
    r1 = x - x.astype(BF16).astype(F32)
    r2 = r1 - r1.astype(BF16).astype(F32)
    piece = jnp.where(lane < N_HEADS, x, jnp.where(lane < 2 * N_HEADS, r1, r2))
    o_ref[0] = jnp.where(lane < 3 * N_HEADS, piece, 0.0).astype(BF16)


def _forget_cols(logf_rep, *, name):
    B, n, L = logf_rep.shape
    return pl.pallas_call(
        _forget_cols_kernel,
        out_shape=jax.ShapeDtypeStruct((B, n, L), BF16),
        grid=(B,),
        in_specs=[pl.BlockSpec((1, n, L), lambda b: (b, 0, 0))],
        out_specs=pl.BlockSpec((1, n, L), lambda b: (b, 0, 0)),
        compiler_params=_params("parallel"),
        name=name,
    )(logf_rep)


def _forget_query_cols(h, rows):
    lane = lax.broadcasted_iota(jnp.int32, (rows, LANES), 1)
    return jnp.where((lane % N_HEADS == h) & (lane < 3 * N_HEADS), -1.0, 0.0).astype(BF16)


N_SCORE_BUFS = 3


def _block_pairs(n, n_trips):
    low = [(i, j) for i in range(n) for j in range(i)]
    diag = [(i, i) for i in range(n)]
    lows, diags = len(low) // n_trips, n // n_trips
    assert lows * n_trips == len(low) and diags * n_trips == n and (lows + diags) % N_SCORE_BUFS == 0
    pairs = []
    for t in range(n_trips):
        pairs += low[t * lows:(t + 1) * lows] + diag[t * diags:(t + 1) * diags]
    for i in range(n):
        assert all(pairs.index((i, j)) < pairs.index((i, i)) for j in range(i))
    pairs.append(pairs[-1])
    return lows, diags, jnp.asarray([p[0] for p in pairs], jnp.int32), jnp.asarray([p[1] for p in pairs], jnp.int32)


def _fox_stream_kernel(qt_ref, kb_ref, q_ref, k_ref, v_ref, fa_ref, gate_ref, o_ref, m_sc, acc_sc, s_sc, *,
                       tq, tk, hps, lows, diags, n_trips):
    heads = range(hps)
    cols = [slice(hh * HEAD_DIM, (hh + 1) * HEAD_DIM) for hh in heads]
    m_sc[...] = jnp.full(m_sc.shape, NEG, F32)
    acc_sc[...] = jnp.zeros(acc_sc.shape, F32)
    ones = jnp.ones((ONES_ROWS, tk), BF16)
    fq = [_forget_query_cols(pl.program_id(1) * hps + hh, tq) for hh in heads]

    def scores(p, s_ref):
        qoff = pl.multiple_of(qt_ref[p] * tq, tq)
        koff = pl.multiple_of(kb_ref[p] * tk, tk)
        fa = fa_ref[0, pl.ds(koff, tk), :]
        for hh in heads:
            qa = jnp.concatenate([q_ref[0, pl.ds(qoff, tq), cols[hh]], fq[hh]], axis=1)
            ka = jnp.concatenate([k_ref[0, pl.ds(koff, tk), cols[hh]], fa], axis=1)
            s_ref[hh] = lax.dot_general(ka, qa, (((1,), (1,)), ((), ())),
                                        preferred_element_type=F32)

    def absorb(p, s_ref, diagonal):
        i = qt_ref[p]
        koff = pl.multiple_of(kb_ref[p] * tk, tk)
        for hh in heads:
            s = s_ref[hh]
            if diagonal:
                s = jnp.where(lax.broadcasted_iota(jnp.int32, (tk, tq), 0)
                              <= lax.broadcasted_iota(jnp.int32, (tk, tq), 1), s, NEG)
            m_prev = m_sc[hh, i]
            m_new = jnp.maximum(m_prev, jnp.max(s, axis=0, keepdims=True))
            alpha = jnp.exp2(m_prev - m_new)
            pr = jnp.exp2(s - m_new).astype(BF16)
            vt = jnp.concatenate([v_ref[0, pl.ds(koff, tk), cols[hh]].T, ones], axis=0)
            acc = alpha * acc_sc[hh, i] + jnp.dot(vt, pr, preferred_element_type=F32)
            m_sc[hh, i] = m_new
            if diagonal:
                qoff = pl.multiple_of(i * tq, tq)
                o = (acc[:HEAD_DIM] * (1.0 / acc[HEAD_DIM:HEAD_DIM + 1])).T
                gate = gate_ref[0, pl.ds(qoff, tq), cols[hh]].astype(F32)
                o_ref[0, pl.ds(qoff, tq), cols[hh]] = (o * _silu(gate)).astype(BF16)
            else:
                acc_sc[hh, i] = acc

    per_trip = lows + diags

    def body(t, c):
        p = t * per_trip
        for u in range(per_trip):
            scores(p + u + 1, s_sc.at[(u + 1) % N_SCORE_BUFS])
            absorb(p + u, s_sc.at[u % N_SCORE_BUFS], u >= lows)
        return c

    scores(0, s_sc.at[0])
    lax.fori_loop(0, n_trips, body, 0)


def _fox_stream(zb, k16, v16, fa, *, B, T, tq, hps, name):
    assert T % tq == 0 and N_HEADS % hps == 0
    nq = T // tq
    n_trips = nq // 2
    lows, diags, qt, kb = _block_pairs(nq, n_trips)
    w = hps * HEAD_DIM
    seq = lambda b, h, *_: (b, 0, h)
    return pl.pallas_call(
        functools.partial(_fox_stream_kernel, tq=tq, tk=tq, hps=hps, lows=lows, diags=diags, n_trips=n_trips),
        out_shape=jax.ShapeDtypeStruct((B, T, W_B), BF16),
        grid_spec=pltpu.PrefetchScalarGridSpec(
            num_scalar_prefetch=2,
            grid=(B, N_HEADS // hps),
            in_specs=[
                pl.BlockSpec((1, T, w), seq),
                pl.BlockSpec((1, T, w), seq),
                pl.BlockSpec((1, T, w), seq),
                pl.BlockSpec((1, T, LANES), lambda b, h, *_: (b, 0, 0)),
                pl.BlockSpec((1, T, w), lambda b, h, *_: (b, 0, N_HEADS // hps + h)),
            ],
            out_specs=pl.BlockSpec((1, T, w), seq),
            scratch_shapes=[
                pltpu.VMEM((hps, nq, 1, tq), F32),
                pltpu.VMEM((hps, nq, HEAD_DIM + ONES_ROWS, tq), F32),
                pltpu.VMEM((N_SCORE_BUFS, hps, tq, tq), F32),
            ],
        ),
        compiler_params=_params("parallel", "parallel"),
        name=name,
    )(qt, kb, zb, k16, v16, fa, zb)


DEC_STRIDE = 4
DEC_GROUP = N_HEADS // DEC_STRIDE


def _fox_decode_kernel(z_ref, kc_ref, vc_ref, kn_ref, vn_ref, fa_ref, far_ref, o_ref, m_sc, l_sc, acc_sc, *,
                       P, T, pc):
    c = pl.program_id(1)
    nt = (((1,), (1,)), ((), ()))
    heads = [(g, i, g + DEC_STRIDE * i) for g in range(DEC_STRIDE) for i in range(DEC_GROUP)]

    @pl.when(c == 0)
    def _():
        m_sc[...] = jnp.full(m_sc.shape, NEG, F32)
        l_sc[...] = jnp.zeros(l_sc.shape, F32)
        acc_sc[...] = jnp.zeros(acc_sc.shape, F32)

    def qa_of(h):
        return jnp.concatenate([z_ref[:, h * HEAD_DIM:(h + 1) * HEAD_DIM], _forget_query_cols(h, T)], axis=1)

    def update(s, pv_of):
        m_prev = m_sc[...]
        m_new = jnp.maximum(m_prev, jnp.max(s, axis=-1, keepdims=True))
        alpha = jnp.exp2(m_prev - m_new)
        p = jnp.exp2(s - m_new)
        l_sc[...] = alpha * l_sc[...] + jnp.sum(p, axis=-1, keepdims=True)
        acc_sc[...] = alpha * acc_sc[...] + pv_of(p.astype(BF16))
        m_sc[...] = m_new

    n = DEC_GROUP * pc
    gq = DEC_GROUP * T
    far = far_ref[0, pl.ds(pl.multiple_of(c * n, n), n), :]
    s = jnp.concatenate(
        [lax.dot_general(jnp.concatenate([qa_of(g + DEC_STRIDE * i) for i in range(DEC_GROUP)], axis=0),
                         jnp.concatenate([kc_ref[0, pl.ds(g, n, stride=DEC_STRIDE), :].astype(BF16), far], axis=1),
                         nt, preferred_element_type=F32) for g in range(DEC_STRIDE)], axis=0)
    row_i = (lax.broadcasted_iota(jnp.int32, s.shape, 0) // T) % DEC_GROUP
    col_i = lax.broadcasted_iota(jnp.int32, s.shape, 1) % DEC_GROUP
    update(jnp.where(row_i == col_i, s, NEG),
           lambda pb: jnp.concatenate(
               [jnp.dot(pb[g * gq:(g + 1) * gq, :], vc_ref[0, pl.ds(g, n, stride=DEC_STRIDE), :].astype(BF16),
                        preferred_element_type=F32) for g in range(DEC_STRIDE)], axis=0))

    @pl.when(c == pl.num_programs(1) - 1)
    def _():
        rows = lax.broadcasted_iota(jnp.int32, (N_HEADS * T, T), 0) % T
        cols = lax.broadcasted_iota(jnp.int32, (N_HEADS * T, T), 1)
        fa_n = fa_ref[0, P:, :]
        s_n = jnp.concatenate(
            [lax.dot_general(qa_of(h), jnp.concatenate([kn_ref[:, h * HEAD_DIM:(h + 1) * HEAD_DIM], fa_n], axis=1),
                             nt, preferred_element_type=F32) for _, _, h in heads], axis=0)
        update(jnp.where(cols <= rows, s_n, NEG),
               lambda pb: jnp.concatenate(
                   [jnp.dot(pb[r * T:(r + 1) * T, :], vn_ref[:, h * HEAD_DIM:(h + 1) * HEAD_DIM],
                            preferred_element_type=F32) for r, (_, _, h) in enumerate(heads)], axis=0))
        o = acc_sc[...] * (1.0 / l_sc[...])
        for r, (_, _, h) in enumerate(heads):
            hc = slice(h * HEAD_DIM, (h + 1) * HEAD_DIM)
            gate = z_ref[:, W_B + h * HEAD_DIM:W_B + (h + 1) * HEAD_DIM].astype(F32)
            o_ref[:, hc] = (o[r * T:(r + 1) * T, :] * _silu(gate)).astype(BF16)


def _fox_decode(zb, k16, v16, cache_k, cache_v, fa, fa_rep, *, B, T, P, pc, name):
    c_spec = pl.BlockSpec((1, pc * N_HEADS, HEAD_DIM), lambda b, c: (b, c, 0))
    n_spec = pl.BlockSpec((T, W_B), lambda b, c: (b, 0))
    return pl.pallas_call(
        functools.partial(_fox_decode_kernel, P=P, T=T, pc=pc),
        out_shape=jax.ShapeDtypeStruct((B * T, W_B), BF16),
        grid=(B, P // pc),
        in_specs=[
            pl.BlockSpec((T, 2 * W_B), lambda b, c: (b, 0)),
            c_spec, c_spec, n_spec, n_spec,
            pl.BlockSpec((1, P + T, LANES), lambda b, c: (b, 0, 0)),
            pl.BlockSpec((1, P * DEC_GROUP, LANES), lambda b, c: (b, 0, 0)),
        ],
        out_specs=n_spec,
        scratch_shapes=[
            pltpu.VMEM((N_HEADS * T, 1), F32),
            pltpu.VMEM((N_HEADS * T, 1), F32),
            pltpu.VMEM((N_HEADS * T, HEAD_DIM), F32),
        ],
        compiler_params=_params("parallel", "arbitrary"),
        name=name,
    )(zb, cache_k, cache_v, k16, v16, fa, fa_rep)


def _trunk(x3, pos0, pool_prev, past, wts, tag):
    B, T, D = x3.shape
    M = B * T
    x = x3.reshape(M, D)
    prompt = pool_prev is None
    tm = 1024 if prompt else M
    tm_res = 1024 if prompt else M
    tn_in = 2048
    tn_out = 1024 if prompt else 2048
    E = wts["w_out_a"].shape[1]
    q_scale = HEAD_DIM ** -0.5 * LOG2E

    new_pool = []
    for l in range(wts["w_in_a"].shape[0]):
        z = _norm_proj(x, wts["norm_a"][l], wts["w_in_a"], l, tm=tm, tn=tn_in, name=f"in_a{l}_{tag}")
        u3 = z[:, :E].reshape(B, T, E) if not prompt else None
        if prompt:
            t = _pool_prompt(z, wts["w_grp_a"], l, wts["scale_a"][l], T=T, tm=256, name=f"pool{l}_{tag}")
            new_pool.append(z.reshape(B, T, 2 * E)[:, T - POOL_PAD:, :E].astype(F32))
        else:
            hist = jnp.pad(pool_prev[l].astype(F32), ((0, 0), (HALO - POOL_PAD, 0), (0, 0)))
            full = jnp.concatenate([hist, u3.astype(F32)], axis=1)
            t = _pool_sample(full.reshape(B * (HALO + T), E), z, wts["w_grp_a"], l, wts["scale_a"][l],
                             nseq=B, T=T, pos0=pos0, name=f"pool{l}_{tag}")
            new_pool.append(full[:, HALO + T - POOL_PAD:, :])
        x = _proj_res(t, wts["w_out_a"], l, x, tm=tm_res, tn=tn_out, name=f"out_a{l}_{tag}")

    k32, v32, k16, v16, logf_rep = _kv_proj(x, wts["norm_kv"], wts["w_k"], wts["w_v"], wts["w_f_rep"],
                                            wts["b_f_rep"], tm=min(M, 512), name=f"kv_{tag}")
    logf3 = logf_rep[:, :N_HEADS].reshape(B, T, N_HEADS)
    logf_rep = logf_rep.reshape(B, T, LANES)
    if not prompt:
        past_k, past_v, past_logf = past
        P = past_k.shape[1]
        past_rep = jnp.pad(jnp.tile(past_logf.astype(F32), (1, 1, 3)), ((0, 0), (0, 0), (0, LANES - 3 * N_HEADS)))
        logf_rep = jnp.concatenate([past_rep, logf_rep], axis=1)
        ck = past_k.reshape(B, P * N_HEADS, HEAD_DIM)
        cv = past_v.reshape(B, P * N_HEADS, HEAD_DIM)
    fa = _forget_cols(logf_rep, name=f"fcols_{tag}")
    if not prompt:
        fa_rep = jnp.repeat(fa[:, :P], DEC_GROUP, axis=1)

    n_b = wts["w_in_b"].shape[0]
    for l in range(n_b):
        zb = _norm_proj(x, wts["norm_b"][l], wts["w_in_b"], l, tm=tm, tn=tn_in,
                        n_scaled=W_B // tn_in, scale=q_scale, name=f"in_b{l}_{tag}")
        if prompt:
            og = _fox_stream(zb.reshape(B, T, 2 * W_B), k16.reshape(B, T, W_B), v16.reshape(B, T, W_B), fa,
                             B=B, T=T, tq=512, hps=2, name=f"attn{l}_{tag}").reshape(M, W_B)
        else:
            og = _fox_decode(zb, k16, v16, ck, cv, fa, fa_rep, B=B, T=T, P=P, pc=1024, name=f"attn{l}_{tag}")
        if l + 1 < n_b:
            x = _proj_res(og, wts["w_out_b"], l, x, tm=tm_res, tn=tn_out, name=f"out_b{l}_{tag}")
        else:
            y = _proj_res_norm(og, wts["w_out_b"], l, x, wts["norm_f"], tm=min(tm_res, 512),
                               name=f"out_b{l}_{tag}")
    return (y.reshape(B, T, D), k32.reshape(B, T, N_HEADS, HEAD_DIM), v32.reshape(B, T, N_HEADS, HEAD_DIM),
            logf3, jnp.stack(new_pool))


def kernel(x_prompt, x_sample, cache_k, cache_v, cache_logf, state_pool, norm_a, w_in_a, w_grp_a, scale_a,
           w_out_a, norm_kv, w_kv, b_f, norm_b, w_in_b, w_out_b, norm_f):
    wts = dict(
        norm_a=norm_a, w_in_a=w_in_a.astype(BF16), w_grp_a=w_grp_a.astype(BF16), scale_a=scale_a,
        w_out_a=w_out_a.astype(BF16), norm_kv=norm_kv,
        w_k=w_kv[:, :W_B].astype(BF16), w_v=w_kv[:, W_B:2 * W_B].astype(BF16),
        w_f_rep=jnp.pad(jnp.tile(w_kv[:, 2 * W_B:], (1, 3)), ((0, 0), (0, LANES - 3 * N_HEADS))).astype(BF16),
        b_f_rep=jnp.pad(jnp.tile(b_f, 3), (0, LANES - 3 * N_HEADS)),
        norm_b=norm_b, w_in_b=w_in_b.astype(BF16), w_out_b=w_out_b.astype(BF16), norm_f=norm_f,
    )
    y_p, k_p, v_p, lf_p, pool_p = _trunk(x_prompt, 0, None, None, wts, "p")
    y_s, k_s, v_s, lf_s, pool_s = _trunk(x_sample, cache_k.shape[1], state_pool,
                                         (cache_k, cache_v, cache_logf), wts, "s")
    return (y_p, y_s, k_p, v_p, lf_p, pool_p, k_s, v_s, lf_s, pool_s)
```

```python
import functools

import jax
import jax.numpy as jnp
from jax import lax
from jax.experimental import pallas as pl
from jax.experimental.pallas import tpu as pltpu

F32 = jnp.float32
BF16 = jnp.bfloat16

EPS = 1e-6
N_HEADS = 16
HEAD_DIM = 128
W_B = N_HEADS * HEAD_DIM
POOL_WINDOWS = (2, 4, 8, 16)
POOL_PAD = max(POOL_WINDOWS) - 1
HALO = 16
NEG = -1e30
LOG2E = 1.4426950408889634
ONES_ROWS = 16
LANES = 128
VMEM_LIMIT = 56 * 1024 * 1024


def _params(*sem):
    return pltpu.CompilerParams(dimension_semantics=sem, vmem_limit_bytes=VMEM_LIMIT)


def _silu(g):
    return g * (1.0 / (1.0 + jnp.exp(-g)))


def _rms_rows(xf, g):
    r = lax.rsqrt(jnp.mean(xf * xf, axis=-1, keepdims=True) + EPS)
    return (xf * r) * g


def _norm_proj_kernel(x_ref, g_ref, w_ref, o_ref, h_ref, *, n_scaled, scale):
    j = pl.program_id(1)

    @pl.when(j == 0)
    def _():
        h_ref[...] = _rms_rows(x_ref[...], g_ref[...]).astype(BF16)

    acc = jnp.dot(h_ref[...], w_ref[...], preferred_element_type=F32)
    if n_scaled:
        acc = acc * jnp.where(j < n_scaled, scale, 1.0)
    o_ref[...] = acc.astype(o_ref.dtype)


def _norm_proj(x, g, w, layer, *, tm, tn, n_scaled=0, scale=1.0, name):
    M, D = x.shape
    N = w.shape[2]
    return pl.pallas_call(
        functools.partial(_norm_proj_kernel, n_scaled=n_scaled, scale=scale),
        out_shape=jax.ShapeDtypeStruct((M, N), BF16),
        grid=(M // tm, N // tn),
        in_specs=[
            pl.BlockSpec((tm, D), lambda i, j: (i, 0)),
            pl.BlockSpec((1, D), lambda i, j: (0, 0)),
            pl.BlockSpec((None, D, tn), lambda i, j: (layer, 0, j)),
        ],
        out_specs=pl.BlockSpec((tm, tn), lambda i, j: (i, j)),
        scratch_shapes=[pltpu.VMEM((tm, D), BF16)],
        compiler_params=_params("parallel", "arbitrary"),
        name=name,
    )(x, g.reshape(1, D), w)


def _proj_res_kernel(a_ref, w_ref, x_ref, o_ref):
    o_ref[...] = x_ref[...] + jnp.dot(a_ref[...], w_ref[...], preferred_element_type=F32)


def _proj_res(a, w, layer, x, *, tm, tn, name):
    M, K = a.shape
    N = w.shape[2]
    return pl.pallas_call(
        _proj_res_kernel,
        out_shape=jax.ShapeDtypeStruct((M, N), F32),
        grid=(N // tn, M // tm),
        in_specs=[
            pl.BlockSpec((tm, K), lambda j, i: (i, 0)),
            pl.BlockSpec((None, K, tn), lambda j, i: (layer, 0, j)),
            pl.BlockSpec((tm, tn), lambda j, i: (i, j)),
        ],
        out_specs=pl.BlockSpec((tm, tn), lambda j, i: (i, j)),
        compiler_params=_params("parallel", "parallel"),
        name=name,
    )(a, w, x)


def _proj_res_norm_kernel(a_ref, w_ref, x_ref, g_ref, o_ref):
    x = x_ref[...] + jnp.dot(a_ref[...], w_ref[...], preferred_element_type=F32)
    o_ref[...] = _rms_rows(x, g_ref[...])


def _proj_res_norm(a, w, layer, x, g, *, tm, name):
    M, K = a.shape
    N = w.shape[2]
    return pl.pallas_call(
        _proj_res_norm_kernel,
        out_shape=jax.ShapeDtypeStruct((M, N), F32),
        grid=(M // tm,),
        in_specs=[
            pl.BlockSpec((tm, K), lambda i: (i, 0)),
            pl.BlockSpec((None, K, N), lambda i: (layer, 0, 0)),
            pl.BlockSpec((tm, N), lambda i: (i, 0)),
            pl.BlockSpec((1, N), lambda i: (0, 0)),
        ],
        out_specs=pl.BlockSpec((tm, N), lambda i: (i, 0)),
        compiler_params=_params("parallel"),
        name=name,
    )(a, w, x, g.reshape(1, N))


def _window_sum(full, w):
    s = full
    k = 1
    while k < w:
        s = s + pltpu.roll(s, k, 0)
        k *= 2
    return s


def _pool_finish(s, uf, inv_cnt, gate, wg, sc):
    d = s * inv_cnt - uf
    y = jnp.dot(d.astype(BF16), wg, preferred_element_type=F32) * sc
    return (y * _silu(gate)).astype(BF16)


def _pool_prompt_kernel(u_ref, halo_ref, gate_ref, band_ref, wg_ref, sc_ref, o_ref, *, tm, tiles_per_seq, pos0):
    G = wg_ref.shape[-1]
    ti = pl.program_id(0) % tiles_per_seq
    t = pos0 + ti * tm + lax.broadcasted_iota(jnp.int32, (HALO, 1), 0)
    for g, w in enumerate(POOL_WINDOWS):
        c = slice(g * G, (g + 1) * G)
        u = u_ref[:, c]
        d_main = jnp.dot(band_ref[g], u, preferred_element_type=F32).astype(BF16)
        u0 = u[:HALO].astype(F32)
        h0 = jnp.where(ti == 0, 0.0, halo_ref[:, c].astype(F32))
        s0 = _window_sum(jnp.concatenate([h0, u0], axis=0), w)[HALO:, :]
        d0 = s0 * (1.0 / jnp.minimum(t + 1, w).astype(F32)) - u0
        d = jnp.concatenate([d0.astype(BF16), d_main[HALO:]], axis=0)
        y = jnp.dot(d, wg_ref[g], preferred_element_type=F32) * sc_ref[:, c]
        o_ref[:, c] = (y * _silu(gate_ref[:, c].astype(F32))).astype(BF16)


def _pool_bands(tm):
    r = jnp.arange(tm)[:, None] - jnp.arange(tm)[None, :]
    return jnp.stack([jnp.where(r == 0, 1.0 / w - 1.0, jnp.where((r > 0) & (r < w), 1.0 / w, 0.0))
                      for w in POOL_WINDOWS]).astype(BF16)


def _pool_prompt(z, wg, layer, sc, *, T, tm, name):
    M = z.shape[0]
    E = z.shape[1] // 2
    hb = tm // HALO
    bands = _pool_bands(tm)
    return pl.pallas_call(
        functools.partial(_pool_prompt_kernel, tm=tm, tiles_per_seq=T // tm, pos0=0),
        out_shape=jax.ShapeDtypeStruct((M, E), BF16),
        grid=(M // tm,),
        in_specs=[
            pl.BlockSpec((tm, E), lambda i: (i, 0)),
            pl.BlockSpec((HALO, E), lambda i: (jnp.maximum(i * hb - 1, 0), 0)),
            pl.BlockSpec((tm, E), lambda i: (i, 1)),
            pl.BlockSpec(bands.shape, lambda i: (0, 0, 0)),
            pl.BlockSpec((None,) + wg.shape[1:], lambda i: (layer, 0, 0, 0)),
            pl.BlockSpec((1, E), lambda i: (0, 0)),
        ],
        out_specs=pl.BlockSpec((tm, E), lambda i: (i, 0)),
        compiler_params=_params("parallel"),
        name=name,
    )(z, z, z, bands, wg, sc.reshape(1, E))


def _pool_sample_kernel(full_ref, gate_ref, wg_ref, sc_ref, o_ref, *, nseq, T, pos0):
    G = wg_ref.shape[-1]
    seg = HALO + T
    t = pos0 + lax.broadcasted_iota(jnp.int32, (nseq * T, 1), 0) % T

    def tail(a):
        return a.reshape(nseq, seg, G)[:, HALO:, :].reshape(nseq * T, G)

    for g, w in enumerate(POOL_WINDOWS):
        c = slice(g * G, (g + 1) * G)
        full = full_ref[:, c]
        inv_cnt = 1.0 / jnp.minimum(t + 1, w).astype(F32)
        o_ref[:, c] = _pool_finish(tail(_window_sum(full, w)), tail(full), inv_cnt,
                                   gate_ref[:, c].astype(F32), wg_ref[g], sc_ref[:, c])


def _pool_sample(full, z, wg, layer, sc, *, nseq, T, pos0, name):
    E = full.shape[1]
    return pl.pallas_call(
        functools.partial(_pool_sample_kernel, nseq=nseq, T=T, pos0=pos0),
        out_shape=jax.ShapeDtypeStruct((nseq * T, E), BF16),
        grid=(1,),
        in_specs=[
            pl.BlockSpec(full.shape, lambda i: (0, 0)),
            pl.BlockSpec((nseq * T, E), lambda i: (0, 1)),
            pl.BlockSpec((None,) + wg.shape[1:], lambda i: (layer, 0, 0, 0)),
            pl.BlockSpec((1, E), lambda i: (0, 0)),
        ],
        out_specs=pl.BlockSpec((nseq * T, E), lambda i: (0, 0)),
        compiler_params=_params("arbitrary"),
        name=name,
    )(full, z, wg, sc.reshape(1, E))


def _kv_kernel(x_ref, g_ref, wk_ref, wv_ref, wf_ref, bf_ref, k32_ref, v32_ref, k16_ref, v16_ref, lf_ref, *, tn):
    tm = x_ref.shape[0]
    h = _rms_rows(x_ref[...], g_ref[...]).astype(BF16)
    for w_ref, o32_ref, o16_ref in ((wk_ref, k32_ref, k16_ref), (wv_ref, v32_ref, v16_ref)):
        for c in range(w_ref.shape[1] // tn):
            cs = slice(c * tn, (c + 1) * tn)
            acc = jnp.dot(h, w_ref[:, cs], preferred_element_type=F32)
            o16_ref[:, cs] = acc.astype(BF16)
            for hd in range(tn // HEAD_DIM):
                head = c * (tn // HEAD_DIM) + hd
                o32_ref[pl.ds(head, tm, stride=N_HEADS), :] = acc[:, hd * HEAD_DIM:(hd + 1) * HEAD_DIM]
    a = jnp.dot(h, wf_ref[...], preferred_element_type=F32) + bf_ref[...]
    lf_ref[...] = jnp.minimum(a, 0.0) - jnp.log1p(jnp.exp(-jnp.abs(a)))


def _kv_proj(x, g, wk, wv, wf, bf, *, tm, name):
    M, D = x.shape
    N = wk.shape[1]
    L = wf.shape[1]
    resident = dict(pipeline_mode=pl.Buffered(1))
    row = lambda i: (i, 0)
    fixed = lambda i: (0, 0)
    return pl.pallas_call(
        functools.partial(_kv_kernel, tn=512),
        out_shape=(jax.ShapeDtypeStruct((M * N_HEADS, HEAD_DIM), F32),
                   jax.ShapeDtypeStruct((M * N_HEADS, HEAD_DIM), F32),
                   jax.ShapeDtypeStruct((M, N), BF16), jax.ShapeDtypeStruct((M, N), BF16),
                   jax.ShapeDtypeStruct((M, L), F32)),
        grid=(M // tm,),
        in_specs=[
            pl.BlockSpec((tm, D), row),
            pl.BlockSpec((1, D), fixed),
            pl.BlockSpec((D, N), fixed, **resident),
            pl.BlockSpec((D, N), fixed, **resident),
            pl.BlockSpec((D, L), fixed, **resident),
            pl.BlockSpec((1, L), fixed),
        ],
        out_specs=(pl.BlockSpec((tm * N_HEADS, HEAD_DIM), row), pl.BlockSpec((tm * N_HEADS, HEAD_DIM), row),
                   pl.BlockSpec((tm, N), row), pl.BlockSpec((tm, N), row), pl.BlockSpec((tm, L), row)),
        compiler_params=_params("parallel"),
        name=name,
    )(x, g.reshape(1, D), wk, wv, wf, bf.reshape(1, L))


def _forget_cols_kernel(x_ref, o_ref):
    x = x_ref[0]
    n = x.shape[0]
    row = lax.broadcasted_iota(jnp.int32, x.shape, 0)
    lane = lax.broadcasted_iota(jnp.int32, x.shape, 1)
    k = 1
    while k < n:
        x = x + jnp.where(row >= k, pltpu.roll(x, k, 0), 0.0)
        k *= 2
    x = x * LOG2E
    r1 = x - x.astype(BF16).astype(F32)
    r2 = r1 - r1.astype(BF16).astype(F32)
    piece = jnp.where(lane < N_HEADS, x, jnp.where(lane < 2 * N_HEADS, r1, r2))
    o_ref[0] = jnp.where(lane < 3 * N_HEADS, piece, 0.0).astype(BF16)


def _forget_cols(logf_rep, *, name):
    B, n, L = logf_rep.shape
    return pl.pallas_call(
        _forget_cols_kernel,
        out_shape=jax.ShapeDtypeStruct((B, n, L), BF16),
        grid=(B,),
        in_specs=[pl.BlockSpec((1, n, L), lambda b: (b, 0, 0))],
        out_specs=pl.BlockSpec((1, n, L), lambda b: (b, 0, 0)),
        compiler_params=_params("parallel"),
        name=name,
    )(logf_rep)


def _forget_query_cols(h, rows):
    lane = lax.broadcasted_iota(jnp.int32, (rows, LANES), 1)
    return jnp.where((lane % N_HEADS == h) & (lane < 3 * N_HEADS), -1.0, 0.0).astype(BF16)


N_SCORE_BUFS = 3


def _block_pairs(n, n_trips):
    low = [(i, j) for i in range(n) for j in range(i)]
    diag = [(i, i) for i in range(n)]
    lows, diags = len(low) // n_trips, n // n_trips
    assert lows * n_trips == len(low) and diags * n_trips == n and (lows + diags) % N_SCORE_BUFS == 0
    pairs = []
    for t in range(n_trips):
        pairs += low[t * lows:(t + 1) * lows] + diag[t * diags:(t + 1) * diags]
    for i in range(n):
        assert all(pairs.index((i, j)) < pairs.index((i, i)) for j in range(i))
    pairs.append(pairs[-1])
    return lows, diags, jnp.asarray([p[0] for p in pairs], jnp.int32), jnp.asarray([p[1] for p in pairs], jnp.int32)


def _fox_stream_kernel(qt_ref, kb_ref, q_ref, k_ref, v_ref, fa_ref, gate_ref, o_ref, m_sc, acc_sc, s_sc, *,
                       tq, tk, hps, lows, diags, n_trips):
    heads = range(hps)
    cols = [slice(hh * HEAD_DIM, (hh + 1) * HEAD_DIM) for hh in heads]
    m_sc[...] = jnp.full(m_sc.shape, NEG, F32)
    acc_sc[...] = jnp.zeros(acc_sc.shape, F32)
    ones = jnp.ones((ONES_ROWS, tk), BF16)
    fq = [_forget_query_cols(pl.program_id(1) * hps + hh, tq) for hh in heads]

    nt = (((1,), (1,)), ((), ()))
    half = tk // 2

    def scores(p, s_ref, diagonal):
        qoff = pl.multiple_of(qt_ref[p] * tq, tq)
        koff = pl.multiple_of(kb_ref[p] * tk, tk)
        fa = fa_ref[0, pl.ds(koff, tk), :]
        for hh in heads:
            qa = jnp.concatenate([q_ref[0, pl.ds(qoff, tq), cols[hh]], fq[hh]], axis=1)
            ka = jnp.concatenate([k_ref[0, pl.ds(koff, tk), cols[hh]], fa], axis=1)
            if diagonal:
                s_ref[hh, :half, :] = lax.dot_general(ka[:half], qa, nt, preferred_element_type=F32)
                s_ref[hh, half:, half:] = lax.dot_general(ka[half:], qa[half:], nt, preferred_element_type=F32)
            else:
                s_ref[hh] = lax.dot_general(ka, qa, nt, preferred_element_type=F32)

    def absorb(p, s_ref, diagonal):
        i = qt_ref[p]
        koff = pl.multiple_of(kb_ref[p] * tk, tk)
        for hh in heads:
            s = s_ref[hh]
            if diagonal:
                s = jnp.where(lax.broadcasted_iota(jnp.int32, (tk, tq), 0)
                              <= lax.broadcasted_iota(jnp.int32, (tk, tq), 1), s, NEG)
            m_prev = m_sc[hh, i]
            m_new = jnp.maximum(m_prev, jnp.max(s, axis=0, keepdims=True))
            alpha = jnp.exp2(m_prev - m_new)
            pr = jnp.exp2(s - m_new).astype(BF16)
            vt = jnp.concatenate([v_ref[0, pl.ds(koff, tk), cols[hh]].T, ones], axis=0)
            if diagonal:
                pv = jnp.concatenate([jnp.dot(vt[:, :half], pr[:half, :half], preferred_element_type=F32),
                                      jnp.dot(vt, pr[:, half:], preferred_element_type=F32)], axis=1)
            else:
                pv = jnp.dot(vt, pr, preferred_element_type=F32)
            acc = alpha * acc_sc[hh, i] + pv
            m_sc[hh, i] = m_new
            if diagonal:
                qoff = pl.multiple_of(i * tq, tq)
                o = (acc[:HEAD_DIM] * (1.0 / acc[HEAD_DIM:HEAD_DIM + 1])).T
                gate = gate_ref[0, pl.ds(qoff, tq), cols[hh]].astype(F32)
                o_ref[0, pl.ds(qoff, tq), cols[hh]] = (o * _silu(gate)).astype(BF16)
            else:
                acc_sc[hh, i] = acc

    per_trip = lows + diags

    def body(t, c):
        p = t * per_trip
        for u in range(per_trip):
            scores(p + u + 1, s_sc.at[(u + 1) % N_SCORE_BUFS], (u + 1) % per_trip >= lows)
            absorb(p + u, s_sc.at[u % N_SCORE_BUFS], u >= lows)
        return c

    scores(0, s_sc.at[0], lows == 0)
    lax.fori_loop(0, n_trips, body, 0)


def _fox_stream(zb, k16, v16, fa, *, B, T, tq, hps, name):
    assert T % tq == 0 and N_HEADS % hps == 0
    nq = T // tq
    n_trips = nq // 4
    lows, diags, qt, kb = _block_pairs(nq, n_trips)
    w = hps * HEAD_DIM
    seq = lambda b, h, *_: (b, 0, h)
    return pl.pallas_call(
        functools.partial(_fox_stream_kernel, tq=tq, tk=tq, hps=hps, lows=lows, diags=diags, n_trips=n_trips),
        out_shape=jax.ShapeDtypeStruct((B, T, W_B), BF16),
        grid_spec=pltpu.PrefetchScalarGridSpec(
            num_scalar_prefetch=2,
            grid=(B, N_HEADS // hps),
            in_specs=[
                pl.BlockSpec((1, T, w), seq),
                pl.BlockSpec((1, T, w), seq),
                pl.BlockSpec((1, T, w), seq),
                pl.BlockSpec((1, T, LANES), lambda b, h, *_: (b, 0, 0)),
                pl.BlockSpec((1, T, w), lambda b, h, *_: (b, 0, N_HEADS // hps + h)),
            ],
            out_specs=pl.BlockSpec((1, T, w), seq),
            scratch_shapes=[
                pltpu.VMEM((hps, nq, 1, tq), F32),
                pltpu.VMEM((hps, nq, HEAD_DIM + ONES_ROWS, tq), F32),
                pltpu.VMEM((N_SCORE_BUFS, hps, tq, tq), F32),
            ],
        ),
        compiler_params=_params("parallel", "parallel"),
        name=name,
    )(qt, kb, zb, k16, v16, fa, zb)


DEC_STRIDE = 4
DEC_GROUP = N_HEADS // DEC_STRIDE


def _fox_decode_kernel(z_ref, kc_ref, vc_ref, kn_ref, vn_ref, fa_ref, far_ref, o_ref, m_sc, l_sc, acc_sc, *,
                       P, T, pc):
    c = pl.program_id(1)
    nt = (((1,), (1,)), ((), ()))
    heads = [(g, i, g + DEC_STRIDE * i) for g in range(DEC_STRIDE) for i in range(DEC_GROUP)]

    @pl.when(c == 0)
    def _():
        m_sc[...] = jnp.full(m_sc.shape, NEG, F32)
        l_sc[...] = jnp.zeros(l_sc.shape, F32)
        acc_sc[...] = jnp.zeros(acc_sc.shape, F32)

    def qa_of(h):
        return jnp.concatenate([z_ref[:, h * HEAD_DIM:(h + 1) * HEAD_DIM], _forget_query_cols(h, T)], axis=1)

    def update(s, pv_of):
        m_prev = m_sc[...]
        m_new = jnp.maximum(m_prev, jnp.max(s, axis=-1, keepdims=True))
        alpha = jnp.exp2(m_prev - m_new)
        p = jnp.exp2(s - m_new)
        l_sc[...] = alpha * l_sc[...] + jnp.sum(p, axis=-1, keepdims=True)
        acc_sc[...] = alpha * acc_sc[...] + pv_of(p.astype(BF16))
        m_sc[...] = m_new

    n = DEC_GROUP * pc
    gq = DEC_GROUP * T
    far = far_ref[0, pl.ds(pl.multiple_of(c * n, n), n), :]
    s = jnp.concatenate(
        [lax.dot_general(jnp.concatenate([qa_of(g + DEC_STRIDE * i) for i in range(DEC_GROUP)], axis=0),
                         jnp.concatenate([kc_ref[0, pl.ds(g, n, stride=DEC_STRIDE), :].astype(BF16), far], axis=1),
                         nt, preferred_element_type=F32) for g in range(DEC_STRIDE)], axis=0)
    row_i = (lax.broadcasted_iota(jnp.int32, s.shape, 0) // T) % DEC_GROUP
    col_i = lax.broadcasted_iota(jnp.int32, s.shape, 1) % DEC_GROUP
    update(jnp.where(row_i == col_i, s, NEG),
           lambda pb: jnp.concatenate(
               [jnp.dot(pb[g * gq:(g + 1) * gq, :], vc_ref[0, pl.ds(g, n, stride=DEC_STRIDE), :].astype(BF16),
                        preferred_element_type=F32) for g in range(DEC_STRIDE)], axis=0))

    @pl.when(c == pl.num_programs(1) - 1)
    def _():
        rows = lax.broadcasted_iota(jnp.int32, (N_HEADS * T, T), 0) % T
        cols = lax.broadcasted_iota(jnp.int32, (N_HEADS * T, T), 1)
        fa_n = fa_ref[0, P:, :]
        s_n = jnp.concatenate(
            [lax.dot_general(qa_of(h), jnp.concatenate([kn_ref[:, h * HEAD_DIM:(h + 1) * HEAD_DIM], fa_n], axis=1),
                             nt, preferred_element_type=F32) for _, _, h in heads], axis=0)
        update(jnp.where(cols <= rows, s_n, NEG),
               lambda pb: jnp.concatenate(
                   [jnp.dot(pb[r * T:(r + 1) * T, :], vn_ref[:, h * HEAD_DIM:(h + 1) * HEAD_DIM],
                            preferred_element_type=F32) for r, (_, _, h) in enumerate(heads)], axis=0))
        o = acc_sc[...] * (1.0 / l_sc[...])
        for r, (_, _, h) in enumerate(heads):
            hc = slice(h * HEAD_DIM, (h + 1) * HEAD_DIM)
            gate = z_ref[:, W_B + h * HEAD_DIM:W_B + (h + 1) * HEAD_DIM].astype(F32)
            o_ref[:, hc] = (o[r * T:(r + 1) * T, :] * _silu(gate)).astype(BF16)


def _fox_decode(zb, k16, v16, cache_k, cache_v, fa, fa_rep, *, B, T, P, pc, name):
    c_spec = pl.BlockSpec((1, pc * N_HEADS, HEAD_DIM), lambda b, c: (b, c, 0))
    n_spec = pl.BlockSpec((T, W_B), lambda b, c: (b, 0))
    return pl.pallas_call(
        functools.partial(_fox_decode_kernel, P=P, T=T, pc=pc),
        out_shape=jax.ShapeDtypeStruct((B * T, W_B), BF16),
        grid=(B, P // pc),
        in_specs=[
            pl.BlockSpec((T, 2 * W_B), lambda b, c: (b, 0)),
            c_spec, c_spec, n_spec, n_spec,
            pl.BlockSpec((1, P + T, LANES), lambda b, c: (b, 0, 0)),
            pl.BlockSpec((1, P * DEC_GROUP, LANES), lambda b, c: (b, 0, 0)),
        ],
        out_specs=n_spec,
        scratch_shapes=[
            pltpu.VMEM((N_HEADS * T, 1), F32),
            pltpu.VMEM((N_HEADS * T, 1), F32),
            pltpu.VMEM((N_HEADS * T, HEAD_DIM), F32),
        ],
        compiler_params=_params("parallel", "arbitrary"),
        name=name,
    )(zb, cache_k, cache_v, k16, v16, fa, fa_rep)


def _trunk(x3, pos0, pool_prev, past, wts, tag):
    B, T, D = x3.shape
    M = B * T
    x = x3.reshape(M, D)
    prompt = pool_prev is None
    tm = 1024 if prompt else M
    tm_res = 1024 if prompt else M
    tn_in = 2048
    tn_out = 1024 if prompt else 2048
    E = wts["w_out_a"].shape[1]
    q_scale = HEAD_DIM ** -0.5 * LOG2E

    new_pool = []
    for l in range(wts["w_in_a"].shape[0]):
        z = _norm_proj(x, wts["norm_a"][l], wts["w_in_a"], l, tm=tm, tn=tn_in, name=f"in_a{l}_{tag}")
        u3 = z[:, :E].reshape(B, T, E) if not prompt else None
        if prompt:
            t = _pool_prompt(z, wts["w_grp_a"], l, wts["scale_a"][l], T=T, tm=256, name=f"pool{l}_{tag}")
            new_pool.append(z.reshape(B, T, 2 * E)[:, T - POOL_PAD:, :E].astype(F32))
        else:
            hist = jnp.pad(pool_prev[l].astype(F32), ((0, 0), (HALO - POOL_PAD, 0), (0, 0)))
            full = jnp.concatenate([hist, u3.astype(F32)], axis=1)
            t = _pool_sample(full.reshape(B * (HALO + T), E), z, wts["w_grp_a"], l, wts["scale_a"][l],
                             nseq=B, T=T, pos0=pos0, name=f"pool{l}_{tag}")
            new_pool.append(full[:, HALO + T - POOL_PAD:, :])
        x = _proj_res(t, wts["w_out_a"], l, x, tm=tm_res, tn=tn_out, name=f"out_a{l}_{tag}")

    k32, v32, k16, v16, logf_rep = _kv_proj(x, wts["norm_kv"], wts["w_k"], wts["w_v"], wts["w_f_rep"],
                                            wts["b_f_rep"], tm=min(M, 512), name=f"kv_{tag}")
    logf3 = logf_rep[:, :N_HEADS].reshape(B, T, N_HEADS)
    logf_rep = logf_rep.reshape(B, T, LANES)
    if not prompt:
        past_k, past_v, past_logf = past
        P = past_k.shape[1]
        past_rep = jnp.pad(jnp.tile(past_logf.astype(F32), (1, 1, 3)), ((0, 0), (0, 0), (0, LANES - 3 * N_HEADS)))
        logf_rep = jnp.concatenate([past_rep, logf_rep], axis=1)
        ck = past_k.reshape(B, P * N_HEADS, HEAD_DIM)
        cv = past_v.reshape(B, P * N_HEADS, HEAD_DIM)
    fa = _forget_cols(logf_rep, name=f"fcols_{tag}")
    if not prompt:
        fa_rep = jnp.repeat(fa[:, :P], DEC_GROUP, axis=1)

    n_b = wts["w_in_b"].shape[0]
    for l in range(n_b):
        zb = _norm_proj(x, wts["norm_b"][l], wts["w_in_b"], l, tm=tm, tn=tn_in,
                        n_scaled=W_B // tn_in, scale=q_scale, name=f"in_b{l}_{tag}")
        if prompt:
            og = _fox_stream(zb.reshape(B, T, 2 * W_B), k16.reshape(B, T, W_B), v16.reshape(B, T, W_B), fa,
                             B=B, T=T, tq=512, hps=2, name=f"attn{l}_{tag}").reshape(M, W_B)
        else:
            og = _fox_decode(zb, k16, v16, ck, cv, fa, fa_rep, B=B, T=T, P=P, pc=1024, name=f"attn{l}_{tag}")
        if l + 1 < n_b:
            x = _proj_res(og, wts["w_out_b"], l, x, tm=tm_res, tn=tn_out, name=f"out_b{l}_{tag}")
        else:
            y = _proj_res_norm(og, wts["w_out_b"], l, x, wts["norm_f"], tm=min(tm_res, 512),
                               name=f"out_b{l}_{tag}")
    return (y.reshape(B, T, D), k32.reshape(B, T, N_HEADS, HEAD_DIM), v32.reshape(B, T, N_HEADS, HEAD_DIM),
            logf3, jnp.stack(new_pool))


def kernel(x_prompt, x_sample, cache_k, cache_v, cache_logf, state_pool, norm_a, w_in_a, w_grp_a, scale_a,
           w_out_a, norm_kv, w_kv, b_f, norm_b, w_in_b, w_out_b, norm_f):
    wts = dict(
        norm_a=norm_a, w_in_a=w_in_a.astype(BF16), w_grp_a=w_grp_a.astype(BF16), scale_a=scale_a,
        w_out_a=w_out_a.astype(BF16), norm_kv=norm_kv,
        w_k=w_kv[:, :W_B].astype(BF16), w_v=w_kv[:, W_B:2 * W_B].astype(BF16),
        w_f_rep=jnp.pad(jnp.tile(w_kv[:, 2 * W_B:], (1, 3)), ((0, 0), (0, LANES - 3 * N_HEADS))).astype(BF16),
        b_f_rep=jnp.pad(jnp.tile(b_f, 3), (0, LANES - 3 * N_HEADS)),
        norm_b=norm_b, w_in_b=w_in_b.astype(BF16), w_out_b=w_out_b.astype(BF16), norm_f=norm_f,
    )
    y_p, k_p, v_p, lf_p, pool_p = _trunk(x_prompt, 0, None, None, wts, "p")
    y_s, k_s, v_s, lf_s, pool_s = _trunk(x_sample, cache_k.shape[1], state_pool,
                                         (cache_k, cache_v, cache_logf), wts, "s")
    return (y_p, y_s, k_p, v_p, lf_p, pool_p, k_s, v_s, lf_s, pool_s)
```

```python
import functools

import jax
import jax.numpy as jnp
from jax import lax
from jax.experimental import pallas as pl
from jax.experimental.pallas import tpu as pltpu

F32 = jnp.float32
BF16 = jnp.bfloat16

EPS = 1e-6
N_HEADS = 16
HEAD_DIM = 128
W_B = N_HEADS * HEAD_DIM
POOL_WINDOWS = (2, 4, 8, 16)
POOL_PAD = max(POOL_WINDOWS) - 1
HALO = 16
NEG = -1e30
LOG2E = 1.4426950408889634
ONES_ROWS = 16
LANES = 128
VMEM_LIMIT = 56 * 1024 * 1024


def _params(*sem):
    return pltpu.CompilerParams(dimension_semantics=sem, vmem_limit_bytes=VMEM_LIMIT)


def _silu(g):
    return g * (1.0 / (1.0 + jnp.exp(-g)))


def _rms_rows(xf, g):
    r = lax.rsqrt(jnp.mean(xf * xf, axis=-1, keepdims=True) + EPS)
    return (xf * r) * g


def _norm_proj_kernel(x_ref, g_ref, w_ref, o_ref, h_ref, *, n_scaled, scale):
    j = pl.program_id(1)

    @pl.when(j == 0)
    def _():
        h_ref[...] = _rms_rows(x_ref[...], g_ref[...]).astype(BF16)

    acc = jnp.dot(h_ref[...], w_ref[...], preferred_element_type=F32)
    if n_scaled:
        acc = acc * jnp.where(j < n_scaled, scale, 1.0)
    o_ref[...] = acc.astype(o_ref.dtype)


def _norm_proj(x, g, w, layer, *, tm, tn, n_scaled=0, scale=1.0, name):
    M, D = x.shape
    N = w.shape[2]
    return pl.pallas_call(
        functools.partial(_norm_proj_kernel, n_scaled=n_scaled, scale=scale),
        out_shape=jax.ShapeDtypeStruct((M, N), BF16),
        grid=(M // tm, N // tn),
        in_specs=[
            pl.BlockSpec((tm, D), lambda i, j: (i, 0)),
            pl.BlockSpec((1, D), lambda i, j: (0, 0)),
            pl.BlockSpec((None, D, tn), lambda i, j: (layer, 0, j)),
        ],
        out_specs=pl.BlockSpec((tm, tn), lambda i, j: (i, j)),
        scratch_shapes=[pltpu.VMEM((tm, D), BF16)],
        compiler_params=_params("parallel", "arbitrary"),
        name=name,
    )(x, g.reshape(1, D), w)


def _proj_res_kernel(a_ref, w_ref, x_ref, o_ref):
    o_ref[...] = x_ref[...] + jnp.dot(a_ref[...], w_ref[...], preferred_element_type=F32)


def _proj_res(a, w, layer, x, *, tm, tn, name):
    M, K = a.shape
    N = w.shape[2]
    return pl.pallas_call(
        _proj_res_kernel,
        out_shape=jax.ShapeDtypeStruct((M, N), F32),
        grid=(N // tn, M // tm),
        in_specs=[
            pl.BlockSpec((tm, K), lambda j, i: (i, 0)),
            pl.BlockSpec((None, K, tn), lambda j, i: (layer, 0, j)),
            pl.BlockSpec((tm, tn), lambda j, i: (i, j)),
        ],
        out_specs=pl.BlockSpec((tm, tn), lambda j, i: (i, j)),
        compiler_params=_params("parallel", "parallel"),
        name=name,
    )(a, w, x)


def _proj_res_norm_kernel(a_ref, w_ref, x_ref, g_ref, o_ref):
    x = x_ref[...] + jnp.dot(a_ref[...], w_ref[...], preferred_element_type=F32)
    o_ref[...] = _rms_rows(x, g_ref[...])


def _proj_res_norm(a, w, layer, x, g, *, tm, name):
    M, K = a.shape
    N = w.shape[2]
    return pl.pallas_call(
        _proj_res_norm_kernel,
        out_shape=jax.ShapeDtypeStruct((M, N), F32),
        grid=(M // tm,),
        in_specs=[
            pl.BlockSpec((tm, K), lambda i: (i, 0)),
            pl.BlockSpec((None, K, N), lambda i: (layer, 0, 0)),
            pl.BlockSpec((tm, N), lambda i: (i, 0)),
            pl.BlockSpec((1, N), lambda i: (0, 0)),
        ],
        out_specs=pl.BlockSpec((tm, N), lambda i: (i, 0)),
        compiler_params=_params("parallel"),
        name=name,
    )(a, w, x, g.reshape(1, N))


def _window_sum(full, w):
    s = full
    k = 1
    while k < w:
        s = s + pltpu.roll(s, k, 0)
        k *= 2
    return s


def _pool_finish(s, uf, inv_cnt, gate, wg, sc):
    d = s * inv_cnt - uf
    y = jnp.dot(d.astype(BF16), wg, preferred_element_type=F32) * sc
    return (y * _silu(gate)).astype(BF16)


def _pool_prompt_kernel(u_ref, halo_ref, gate_ref, band_ref, wg_ref, sc_ref, o_ref, *, tm, tiles_per_seq, pos0):
    G = wg_ref.shape[-1]
    ti = pl.program_id(0) % tiles_per_seq
    t = pos0 + ti * tm + lax.broadcasted_iota(jnp.int32, (HALO, 1), 0)
    for g, w in enumerate(POOL_WINDOWS):
        c = slice(g * G, (g + 1) * G)
        u = u_ref[:, c]
        d_main = jnp.dot(band_ref[g], u, preferred_element_type=F32).astype(BF16)
        u0 = u[:HALO].astype(F32)
        h0 = jnp.where(ti == 0, 0.0, halo_ref[:, c].astype(F32))
        s0 = _window_sum(jnp.concatenate([h0, u0], axis=0), w)[HALO:, :]
        d0 = s0 * (1.0 / jnp.minimum(t + 1, w).astype(F32)) - u0
        d = jnp.concatenate([d0.astype(BF16), d_main[HALO:]], axis=0)
        y = jnp.dot(d, wg_ref[g], preferred_element_type=F32) * sc_ref[:, c]
        o_ref[:, c] = (y * _silu(gate_ref[:, c].astype(F32))).astype(BF16)


def _pool_bands(tm):
    r = jnp.arange(tm)[:, None] - jnp.arange(tm)[None, :]
    return jnp.stack([jnp.where(r == 0, 1.0 / w - 1.0, jnp.where((r > 0) & (r < w), 1.0 / w, 0.0))
                      for w in POOL_WINDOWS]).astype(BF16)


def _pool_prompt(z, wg, layer, sc, *, T, tm, name):
    M = z.shape[0]
    E = z.shape[1] // 2
    hb = tm // HALO
    bands = _pool_bands(tm)
    return pl.pallas_call(
        functools.partial(_pool_prompt_kernel, tm=tm, tiles_per_seq=T // tm, pos0=0),
        out_shape=jax.ShapeDtypeStruct((M, E), BF16),
        grid=(M // tm,),
        in_specs=[
            pl.BlockSpec((tm, E), lambda i: (i, 0)),
            pl.BlockSpec((HALO, E), lambda i: (jnp.maximum(i * hb - 1, 0), 0)),
            pl.BlockSpec((tm, E), lambda i: (i, 1)),
            pl.BlockSpec(bands.shape, lambda i: (0, 0, 0)),
            pl.BlockSpec((None,) + wg.shape[1:], lambda i: (layer, 0, 0, 0)),
            pl.BlockSpec((1, E), lambda i: (0, 0)),
        ],
        out_specs=pl.BlockSpec((tm, E), lambda i: (i, 0)),
        compiler_params=_params("parallel"),
        name=name,
    )(z, z, z, bands, wg, sc.reshape(1, E))


def _pool_sample_kernel(full_ref, gate_ref, wg_ref, sc_ref, o_ref, *, nseq, T, pos0):
    G = wg_ref.shape[-1]
    seg = HALO + T
    t = pos0 + lax.broadcasted_iota(jnp.int32, (nseq * T, 1), 0) % T

    def tail(a):
        return a.reshape(nseq, seg, G)[:, HALO:, :].reshape(nseq * T, G)

    for g, w in enumerate(POOL_WINDOWS):
        c = slice(g * G, (g + 1) * G)
        full = full_ref[:, c]
        inv_cnt = 1.0 / jnp.minimum(t + 1, w).astype(F32)
        o_ref[:, c] = _pool_finish(tail(_window_sum(full, w)), tail(full), inv_cnt,
                                   gate_ref[:, c].astype(F32), wg_ref[g], sc_ref[:, c])


def _pool_sample(full, z, wg, layer, sc, *, nseq, T, pos0, name):
    E = full.shape[1]
    return pl.pallas_call(
        functools.partial(_pool_sample_kernel, nseq=nseq, T=T, pos0=pos0),
        out_shape=jax.ShapeDtypeStruct((nseq * T, E), BF16),
        grid=(1,),
        in_specs=[
            pl.BlockSpec(full.shape, lambda i: (0, 0)),
            pl.BlockSpec((nseq * T, E), lambda i: (0, 1)),
            pl.BlockSpec((None,) + wg.shape[1:], lambda i: (layer, 0, 0, 0)),
            pl.BlockSpec((1, E), lambda i: (0, 0)),
        ],
        out_specs=pl.BlockSpec((nseq * T, E), lambda i: (0, 0)),
        compiler_params=_params("arbitrary"),
        name=name,
    )(full, z, wg, sc.reshape(1, E))


def _kv_kernel(x_ref, g_ref, wk_ref, wv_ref, wf_ref, bf_ref, k32_ref, v32_ref, k16_ref, v16_ref, lf_ref, *, tn):
    tm = x_ref.shape[0]
    h = _rms_rows(x_ref[...], g_ref[...]).astype(BF16)
    for w_ref, o32_ref, o16_ref in ((wk_ref, k32_ref, k16_ref), (wv_ref, v32_ref, v16_ref)):
        for c in range(w_ref.shape[1] // tn):
            cs = slice(c * tn, (c + 1) * tn)
            acc = jnp.dot(h, w_ref[:, cs], preferred_element_type=F32)
            o16_ref[:, cs] = acc.astype(BF16)
            nh = tn // HEAD_DIM
            o32_ref[:, c * nh:(c + 1) * nh, :] = acc.reshape(tm, nh, HEAD_DIM)
    a = jnp.dot(h, wf_ref[...], preferred_element_type=F32) + bf_ref[...]
    lf_ref[...] = jnp.minimum(a, 0.0) - jnp.log1p(jnp.exp(-jnp.abs(a)))


def _kv_proj(x, g, wk, wv, wf, bf, *, tm, name):
    M, D = x.shape
    N = wk.shape[1]
    L = wf.shape[1]
    resident = dict(pipeline_mode=pl.Buffered(1))
    row = lambda i: (i, 0)
    fixed = lambda i: (0, 0)
    return pl.pallas_call(
        functools.partial(_kv_kernel, tn=1024),
        out_shape=(jax.ShapeDtypeStruct((M, N_HEADS, HEAD_DIM), F32),
                   jax.ShapeDtypeStruct((M, N_HEADS, HEAD_DIM), F32),
                   jax.ShapeDtypeStruct((M, N), BF16), jax.ShapeDtypeStruct((M, N), BF16),
                   jax.ShapeDtypeStruct((M, L), F32)),
        grid=(M // tm,),
        in_specs=[
            pl.BlockSpec((tm, D), row),
            pl.BlockSpec((1, D), fixed),
            pl.BlockSpec((D, N), fixed, **resident),
            pl.BlockSpec((D, N), fixed, **resident),
            pl.BlockSpec((D, L), fixed, **resident),
            pl.BlockSpec((1, L), fixed),
        ],
        out_specs=(pl.BlockSpec((tm, N_HEADS, HEAD_DIM), lambda i: (i, 0, 0)),
                   pl.BlockSpec((tm, N_HEADS, HEAD_DIM), lambda i: (i, 0, 0)),
                   pl.BlockSpec((tm, N), row), pl.BlockSpec((tm, N), row), pl.BlockSpec((tm, L), row)),
        compiler_params=_params("parallel"),
        name=name,
    )(x, g.reshape(1, D), wk, wv, wf, bf.reshape(1, L))


def _forget_cols_kernel(x_ref, o_ref):
    x = x_ref[0]
    n = x.shape[0]
    row = lax.broadcasted_iota(jnp.int32, x.shape, 0)
    lane = lax.broadcasted_iota(jnp.int32, x.shape, 1)
    k = 1
    while k < n:
        x = x + jnp.where(row >= k, pltpu.roll(x, k, 0), 0.0)
        k *= 2
    x = x * LOG2E
    r1 = x - x.astype(BF16).astype(F32)
    r2 = r1 - r1.astype(BF16).astype(F32)
    piece = jnp.where(lane < N_HEADS, x, jnp.where(lane < 2 * N_HEADS, r1, r2))
    o_ref[0] = jnp.where(lane < 3 * N_HEADS, piece, 0.0).astype(BF16)


def _forget_cols(logf_rep, *, name):
    B, n, L = logf_rep.shape
    return pl.pallas_call(
        _forget_cols_kernel,
        out_shape=jax.ShapeDtypeStruct((B, n, L), BF16),
        grid=(B,),
        in_specs=[pl.BlockSpec((1, n, L), lambda b: (b, 0, 0))],
        out_specs=pl.BlockSpec((1, n, L), lambda b: (b, 0, 0)),
        compiler_params=_params("parallel"),
        name=name,
    )(logf_rep)


def _forget_query_cols(h, rows):
    lane = lax.broadcasted_iota(jnp.int32, (rows, LANES), 1)
    return jnp.where((lane % N_HEADS == h) & (lane < 3 * N_HEADS), -1.0, 0.0).astype(BF16)


N_SCORE_BUFS = 3


def _block_pairs(n, n_trips):
    low = [(i, j) for i in range(n) for j in range(i)]
    diag = [(i, i) for i in range(n)]
    lows, diags = len(low) // n_trips, n // n_trips
    assert lows * n_trips == len(low) and diags * n_trips == n and (lows + diags) % N_SCORE_BUFS == 0
    pairs = []
    for t in range(n_trips):
        pairs += low[t * lows:(t + 1) * lows] + diag[t * diags:(t + 1) * diags]
    for i in range(n):
        assert all(pairs.index((i, j)) < pairs.index((i, i)) for j in range(i))
    pairs.append(pairs[-1])
    return lows, diags, jnp.asarray([p[0] for p in pairs], jnp.int32), jnp.asarray([p[1] for p in pairs], jnp.int32)


def _fox_stream_kernel(qt_ref, kb_ref, q_ref, k_ref, v_ref, fa_ref, gate_ref, o_ref, m_sc, acc_sc, s_sc, *,
                       tq, tk, hps, lows, diags, n_trips):
    heads = range(hps)
    cols = [slice(hh * HEAD_DIM, (hh + 1) * HEAD_DIM) for hh in heads]
    m_sc[...] = jnp.full(m_sc.shape, NEG, F32)
    acc_sc[...] = jnp.zeros(acc_sc.shape, F32)
    ones = jnp.ones((ONES_ROWS, tk), BF16)
    fq = [_forget_query_cols(pl.program_id(1) * hps + hh, tq) for hh in heads]

    nt = (((1,), (1,)), ((), ()))
    half = tk // 2

    def scores(p, s_ref, diagonal):
        qoff = pl.multiple_of(qt_ref[p] * tq, tq)
        koff = pl.multiple_of(kb_ref[p] * tk, tk)
        fa = fa_ref[0, pl.ds(koff, tk), :]
        for hh in heads:
            qa = jnp.concatenate([q_ref[0, pl.ds(qoff, tq), cols[hh]], fq[hh]], axis=1)
            ka = jnp.concatenate([k_ref[0, pl.ds(koff, tk), cols[hh]], fa], axis=1)
            if diagonal:
                s_ref[hh, :half, :] = lax.dot_general(ka[:half], qa, nt, preferred_element_type=F32)
                s_ref[hh, half:, half:] = lax.dot_general(ka[half:], qa[half:], nt, preferred_element_type=F32)
            else:
                s_ref[hh] = lax.dot_general(ka, qa, nt, preferred_element_type=F32)

    def absorb(p, s_ref, diagonal):
        i = qt_ref[p]
        koff = pl.multiple_of(kb_ref[p] * tk, tk)
        for hh in heads:
            s = s_ref[hh]
            if diagonal:
                s = jnp.where(lax.broadcasted_iota(jnp.int32, (tk, tq), 0)
                              <= lax.broadcasted_iota(jnp.int32, (tk, tq), 1), s, NEG)
            m_prev = m_sc[hh, i]
            m_new = jnp.maximum(m_prev, jnp.max(s, axis=0, keepdims=True))
            alpha = jnp.exp2(m_prev - m_new)
            pr = jnp.exp2(s - m_new).astype(BF16)
            vt = jnp.concatenate([v_ref[0, pl.ds(koff, tk), cols[hh]].T, ones], axis=0)
            if diagonal:
                pv = jnp.concatenate([jnp.dot(vt[:, :half], pr[:half, :half], preferred_element_type=F32),
                                      jnp.dot(vt, pr[:, half:], preferred_element_type=F32)], axis=1)
            else:
                pv = jnp.dot(vt, pr, preferred_element_type=F32)
            acc = alpha * acc_sc[hh, i] + pv
            m_sc[hh, i] = m_new
            if diagonal:
                qoff = pl.multiple_of(i * tq, tq)
                o = (acc[:HEAD_DIM] * (1.0 / acc[HEAD_DIM:HEAD_DIM + 1])).T
                gate = gate_ref[0, pl.ds(qoff, tq), cols[hh]].astype(F32)
                o_ref[0, pl.ds(qoff, tq), cols[hh]] = (o * _silu(gate)).astype(BF16)
            else:
                acc_sc[hh, i] = acc

    per_trip = lows + diags

    def body(t, c):
        p = t * per_trip
        for u in range(per_trip):
            scores(p + u + 1, s_sc.at[(u + 1) % N_SCORE_BUFS], (u + 1) % per_trip >= lows)
            absorb(p + u, s_sc.at[u % N_SCORE_BUFS], u >= lows)
        return c

    scores(0, s_sc.at[0], lows == 0)
    lax.fori_loop(0, n_trips, body, 0)


def _fox_stream(zb, k16, v16, fa, *, B, T, tq, hps, name):
    assert T % tq == 0 and N_HEADS % hps == 0
    nq = T // tq
    n_trips = nq // 4
    lows, diags, qt, kb = _block_pairs(nq, n_trips)
    w = hps * HEAD_DIM
    seq = lambda b, h, *_: (b, 0, h)
    return pl.pallas_call(
        functools.partial(_fox_stream_kernel, tq=tq, tk=tq, hps=hps, lows=lows, diags=diags, n_trips=n_trips),
        out_shape=jax.ShapeDtypeStruct((B, T, W_B), BF16),
        grid_spec=pltpu.PrefetchScalarGridSpec(
            num_scalar_prefetch=2,
            grid=(B, N_HEADS // hps),
            in_specs=[
                pl.BlockSpec((1, T, w), seq),
                pl.BlockSpec((1, T, w), seq),
                pl.BlockSpec((1, T, w), seq),
                pl.BlockSpec((1, T, LANES), lambda b, h, *_: (b, 0, 0)),
                pl.BlockSpec((1, T, w), lambda b, h, *_: (b, 0, N_HEADS // hps + h)),
            ],
            out_specs=pl.BlockSpec((1, T, w), seq),
            scratch_shapes=[
                pltpu.VMEM((hps, nq, 1, tq), F32),
                pltpu.VMEM((hps, nq, HEAD_DIM + ONES_ROWS, tq), F32),
                pltpu.VMEM((N_SCORE_BUFS, hps, tq, tq), F32),
            ],
        ),
        compiler_params=_params("parallel", "parallel"),
        name=name,
    )(qt, kb, zb, k16, v16, fa, zb)


DEC_STRIDE = 4
DEC_GROUP = N_HEADS // DEC_STRIDE


def _fox_decode_kernel(z_ref, kc_ref, vc_ref, kn_ref, vn_ref, fa_ref, far_ref, o_ref, m_sc, l_sc, acc_sc, *,
                       P, T, pc):
    c = pl.program_id(1)
    nt = (((1,), (1,)), ((), ()))
    heads = [(g, i, g + DEC_STRIDE * i) for g in range(DEC_STRIDE) for i in range(DEC_GROUP)]

    @pl.when(c == 0)
    def _():
        m_sc[...] = jnp.full(m_sc.shape, NEG, F32)
        l_sc[...] = jnp.zeros(l_sc.shape, F32)
        acc_sc[...] = jnp.zeros(acc_sc.shape, F32)

    def qa_of(h):
        return jnp.concatenate([z_ref[:, h * HEAD_DIM:(h + 1) * HEAD_DIM], _forget_query_cols(h, T)], axis=1)

    def update(s, pv_of):
        m_prev = m_sc[...]
        m_new = jnp.maximum(m_prev, jnp.max(s, axis=-1, keepdims=True))
        alpha = jnp.exp2(m_prev - m_new)
        p = jnp.exp2(s - m_new)
        l_sc[...] = alpha * l_sc[...] + jnp.sum(p, axis=-1, keepdims=True)
        acc_sc[...] = alpha * acc_sc[...] + pv_of(p.astype(BF16))
        m_sc[...] = m_new

    n = DEC_GROUP * pc
    gq = DEC_GROUP * T
    far = far_ref[0, pl.ds(pl.multiple_of(c * n, n), n), :]
    s = jnp.concatenate(
        [lax.dot_general(jnp.concatenate([qa_of(g + DEC_STRIDE * i) for i in range(DEC_GROUP)], axis=0),
                         jnp.concatenate([kc_ref[0, pl.ds(g, n, stride=DEC_STRIDE), :].astype(BF16), far], axis=1),
                         nt, preferred_element_type=F32) for g in range(DEC_STRIDE)], axis=0)
    row_i = (lax.broadcasted_iota(jnp.int32, s.shape, 0) // T) % DEC_GROUP
    col_i = lax.broadcasted_iota(jnp.int32, s.shape, 1) % DEC_GROUP
    update(jnp.where(row_i == col_i, s, NEG),
           lambda pb: jnp.concatenate(
               [jnp.dot(pb[g * gq:(g + 1) * gq, :], vc_ref[0, pl.ds(g, n, stride=DEC_STRIDE), :].astype(BF16),
                        preferred_element_type=F32) for g in range(DEC_STRIDE)], axis=0))

    @pl.when(c == pl.num_programs(1) - 1)
    def _():
        rows = lax.broadcasted_iota(jnp.int32, (N_HEADS * T, T), 0) % T
        cols = lax.broadcasted_iota(jnp.int32, (N_HEADS * T, T), 1)
        fa_n = fa_ref[0, P:, :]
        s_n = jnp.concatenate(
            [lax.dot_general(qa_of(h), jnp.concatenate([kn_ref[:, h * HEAD_DIM:(h + 1) * HEAD_DIM], fa_n], axis=1),
                             nt, preferred_element_type=F32) for _, _, h in heads], axis=0)
        update(jnp.where(cols <= rows, s_n, NEG),
               lambda pb: jnp.concatenate(
                   [jnp.dot(pb[r * T:(r + 1) * T, :], vn_ref[:, h * HEAD_DIM:(h + 1) * HEAD_DIM],
                            preferred_element_type=F32) for r, (_, _, h) in enumerate(heads)], axis=0))
        o = acc_sc[...] * (1.0 / l_sc[...])
        for r, (_, _, h) in enumerate(heads):
            hc = slice(h * HEAD_DIM, (h + 1) * HEAD_DIM)
            gate = z_ref[:, W_B + h * HEAD_DIM:W_B + (h + 1) * HEAD_DIM].astype(F32)
            o_ref[:, hc] = (o[r * T:(r + 1) * T, :] * _silu(gate)).astype(BF16)


def _fox_decode(zb, k16, v16, cache_k, cache_v, fa, fa_rep, *, B, T, P, pc, name):
    c_spec = pl.BlockSpec((1, pc * N_HEADS, HEAD_DIM), lambda b, c: (b, c, 0))
    n_spec = pl.BlockSpec((T, W_B), lambda b, c: (b, 0))
    return pl.pallas_call(
        functools.partial(_fox_decode_kernel, P=P, T=T, pc=pc),
        out_shape=jax.ShapeDtypeStruct((B * T, W_B), BF16),
        grid=(B, P // pc),
        in_specs=[
            pl.BlockSpec((T, 2 * W_B), lambda b, c: (b, 0)),
            c_spec, c_spec, n_spec, n_spec,
            pl.BlockSpec((1, P + T, LANES), lambda b, c: (b, 0, 0)),
            pl.BlockSpec((1, P * DEC_GROUP, LANES), lambda b, c: (b, 0, 0)),
        ],
        out_specs=n_spec,
        scratch_shapes=[
            pltpu.VMEM((N_HEADS * T, 1), F32),
            pltpu.VMEM((N_HEADS * T, 1), F32),
            pltpu.VMEM((N_HEADS * T, HEAD_DIM), F32),
        ],
        compiler_params=_params("parallel", "arbitrary"),
        name=name,
    )(zb, cache_k, cache_v, k16, v16, fa, fa_rep)


def _trunk(x3, pos0, pool_prev, past, wts, tag):
    B, T, D = x3.shape
    M = B * T
    x = x3.reshape(M, D)
    prompt = pool_prev is None
    tm = 1024 if prompt else M
    tm_res = 1024 if prompt else M
    tn_in = 2048
    tn_out = 1024 if prompt else 2048
    E = wts["w_out_a"].shape[1]
    q_scale = HEAD_DIM ** -0.5 * LOG2E

    new_pool = []
    for l in range(wts["w_in_a"].shape[0]):
        z = _norm_proj(x, wts["norm_a"][l], wts["w_in_a"], l, tm=tm, tn=tn_in, name=f"in_a{l}_{tag}")
        u3 = z[:, :E].reshape(B, T, E) if not prompt else None
        if prompt:
            t = _pool_prompt(z, wts["w_grp_a"], l, wts["scale_a"][l], T=T, tm=256, name=f"pool{l}_{tag}")
            new_pool.append(z.reshape(B, T, 2 * E)[:, T - POOL_PAD:, :E].astype(F32))
        else:
            hist = jnp.pad(pool_prev[l].astype(F32), ((0, 0), (HALO - POOL_PAD, 0), (0, 0)))
            full = jnp.concatenate([hist, u3.astype(F32)], axis=1)
            t = _pool_sample(full.reshape(B * (HALO + T), E), z, wts["w_grp_a"], l, wts["scale_a"][l],
                             nseq=B, T=T, pos0=pos0, name=f"pool{l}_{tag}")
            new_pool.append(full[:, HALO + T - POOL_PAD:, :])
        x = _proj_res(t, wts["w_out_a"], l, x, tm=tm_res, tn=tn_out, name=f"out_a{l}_{tag}")

    k32, v32, k16, v16, logf_rep = _kv_proj(x, wts["norm_kv"], wts["w_k"], wts["w_v"], wts["w_f_rep"],
                                            wts["b_f_rep"], tm=min(M, 512), name=f"kv_{tag}")
    logf3 = logf_rep[:, :N_HEADS].reshape(B, T, N_HEADS)
    logf_rep = logf_rep.reshape(B, T, LANES)
    if not prompt:
        past_k, past_v, past_logf = past
        P = past_k.shape[1]
        past_rep = jnp.pad(jnp.tile(past_logf.astype(F32), (1, 1, 3)), ((0, 0), (0, 0), (0, LANES - 3 * N_HEADS)))
        logf_rep = jnp.concatenate([past_rep, logf_rep], axis=1)
        ck = past_k.reshape(B, P * N_HEADS, HEAD_DIM)
        cv = past_v.reshape(B, P * N_HEADS, HEAD_DIM)
    fa = _forget_cols(logf_rep, name=f"fcols_{tag}")
    if not prompt:
        fa_rep = jnp.repeat(fa[:, :P], DEC_GROUP, axis=1)

    n_b = wts["w_in_b"].shape[0]
    for l in range(n_b):
        zb = _norm_proj(x, wts["norm_b"][l], wts["w_in_b"], l, tm=tm, tn=tn_in,
                        n_scaled=W_B // tn_in, scale=q_scale, name=f"in_b{l}_{tag}")
        if prompt:
            og = _fox_stream(zb.reshape(B, T, 2 * W_B), k16.reshape(B, T, W_B), v16.reshape(B, T, W_B), fa,
                             B=B, T=T, tq=512, hps=2, name=f"attn{l}_{tag}").reshape(M, W_B)
        else:
            og = _fox_decode(zb, k16, v16, ck, cv, fa, fa_rep, B=B, T=T, P=P, pc=1024, name=f"attn{l}_{tag}")
        if l + 1 < n_b:
            x = _proj_res(og, wts["w_out_b"], l, x, tm=tm_res, tn=tn_out, name=f"out_b{l}_{tag}")
        else:
            y = _proj_res_norm(og, wts["w_out_b"], l, x, wts["norm_f"], tm=min(tm_res, 512),
                               name=f"out_b{l}_{tag}")
    return (y.reshape(B, T, D), k32.reshape(B, T, N_HEADS, HEAD_DIM), v32.reshape(B, T, N_HEADS, HEAD_DIM),
            logf3, jnp.stack(new_pool))


def kernel(x_prompt, x_sample, cache_k, cache_v, cache_logf, state_pool, norm_a, w_in_a, w_grp_a, scale_a,
           w_out_a, norm_kv, w_kv, b_f, norm_b, w_in_b, w_out_b, norm_f):
    wts = dict(
        norm_a=norm_a, w_in_a=w_in_a.astype(BF16), w_grp_a=w_grp_a.astype(BF16), scale_a=scale_a,
        w_out_a=w_out_a.astype(BF16), norm_kv=norm_kv,
        w_k=w_kv[:, :W_B].astype(BF16), w_v=w_kv[:, W_B:2 * W_B].astype(BF16),
        w_f_rep=jnp.pad(jnp.tile(w_kv[:, 2 * W_B:], (1, 3)), ((0, 0), (0, LANES - 3 * N_HEADS))).astype(BF16),
        b_f_rep=jnp.pad(jnp.tile(b_f, 3), (0, LANES - 3 * N_HEADS)),
        norm_b=norm_b, w_in_b=w_in_b.astype(BF16), w_out_b=w_out_b.astype(BF16), norm_f=norm_f,
    )
    y_p, k_p, v_p, lf_p, pool_p = _trunk(x_prompt, 0, None, None, wts, "p")
    y_s, k_s, v_s, lf_s, pool_s = _trunk(x_sample, cache_k.shape[1], state_pool,
                                         (cache_k, cache_v, cache_logf), wts, "s")
    return (y_p, y_s, k_p, v_p, lf_p, pool_p, k_s, v_s, lf_s, pool_s)
```

```python
import functools

import jax
import jax.numpy as jnp
from jax import lax
from jax.experimental import pallas as pl
from jax.experimental.pallas import tpu as pltpu

F32 = jnp.float32
BF16 = jnp.bfloat16

EPS = 1e-6
N_HEADS = 16
HEAD_DIM = 128
W_B = N_HEADS * HEAD_DIM
POOL_WINDOWS = (2, 4, 8, 16)
POOL_PAD = max(POOL_WINDOWS) - 1
HALO = 16
NEG = -1e30
LOG2E = 1.4426950408889634
ONES_ROWS = 16
LANES = 128
MXU_DEPTH = 256
VMEM_LIMIT = 56 * 1024 * 1024
POOL_TILE = MXU_DEPTH
ATTN_TILE = 512
ATTN_HEADS_PER_STEP = 2
DEC_CHUNK = 1024


def _params(*sem):
    return pltpu.CompilerParams(dimension_semantics=sem, vmem_limit_bytes=VMEM_LIMIT)


def _silu(g):
    return g * (1.0 / (1.0 + jnp.exp(-g)))


def _rms_rows(xf, g):
    r = lax.rsqrt(jnp.mean(xf * xf, axis=-1, keepdims=True) + EPS)
    return (xf * r) * g


def _norm_proj_kernel(x_ref, g_ref, w_ref, o_ref, h_ref, *, n_scaled, scale):
    j = pl.program_id(1)

    @pl.when(j == 0)
    def _():
        h_ref[...] = _rms_rows(x_ref[...], g_ref[...]).astype(BF16)

    acc = jnp.dot(h_ref[...], w_ref[...], preferred_element_type=F32)
    if n_scaled:
        acc = acc * jnp.where(j < n_scaled, scale, 1.0)
    o_ref[...] = acc.astype(o_ref.dtype)


def _norm_proj(x, g, w, layer, *, tm, tn, n_scaled=0, scale=1.0, name):
    M, D = x.shape
    N = w.shape[2]
    return pl.pallas_call(
        functools.partial(_norm_proj_kernel, n_scaled=n_scaled, scale=scale),
        out_shape=jax.ShapeDtypeStruct((M, N), BF16),
        grid=(M // tm, N // tn),
        in_specs=[
            pl.BlockSpec((tm, D), lambda i, j: (i, 0)),
            pl.BlockSpec((1, D), lambda i, j: (0, 0)),
            pl.BlockSpec((None, D, tn), lambda i, j: (layer, 0, j)),
        ],
        out_specs=pl.BlockSpec((tm, tn), lambda i, j: (i, j)),
        scratch_shapes=[pltpu.VMEM((tm, D), BF16)],
        compiler_params=_params("parallel", "arbitrary"),
        name=name,
    )(x, g.reshape(1, D), w)


def _proj_res_kernel(a_ref, w_ref, x_ref, o_ref):
    o_ref[...] = x_ref[...] + jnp.dot(a_ref[...], w_ref[...], preferred_element_type=F32)


def _proj_res(a, w, layer, x, *, tm, tn, name):
    M, K = a.shape
    N = w.shape[2]
    return pl.pallas_call(
        _proj_res_kernel,
        out_shape=jax.ShapeDtypeStruct((M, N), F32),
        grid=(N // tn, M // tm),
        in_specs=[
            pl.BlockSpec((tm, K), lambda j, i: (i, 0)),
            pl.BlockSpec((None, K, tn), lambda j, i: (layer, 0, j)),
            pl.BlockSpec((tm, tn), lambda j, i: (i, j)),
        ],
        out_specs=pl.BlockSpec((tm, tn), lambda j, i: (i, j)),
        compiler_params=_params("parallel", "parallel"),
        name=name,
    )(a, w, x)


def _proj_res_norm_kernel(a_ref, w_ref, x_ref, g_ref, o_ref):
    x = x_ref[...] + jnp.dot(a_ref[...], w_ref[...], preferred_element_type=F32)
    o_ref[...] = _rms_rows(x, g_ref[...])


def _proj_res_norm(a, w, layer, x, g, *, tm, name):
    M, K = a.shape
    N = w.shape[2]
    return pl.pallas_call(
        _proj_res_norm_kernel,
        out_shape=jax.ShapeDtypeStruct((M, N), F32),
        grid=(M // tm,),
        in_specs=[
            pl.BlockSpec((tm, K), lambda i: (i, 0)),
            pl.BlockSpec((None, K, N), lambda i: (layer, 0, 0)),
            pl.BlockSpec((tm, N), lambda i: (i, 0)),
            pl.BlockSpec((1, N), lambda i: (0, 0)),
        ],
        out_specs=pl.BlockSpec((tm, N), lambda i: (i, 0)),
        compiler_params=_params("parallel"),
        name=name,
    )(a, w, x, g.reshape(1, N))


def _window_sum(full, w):
    s = full
    k = 1
    while k < w:
        s = s + pltpu.roll(s, k, 0)
        k *= 2
    return s


def _pool_finish(s, uf, inv_cnt, gate, wg, sc):
    d = s * inv_cnt - uf
    y = jnp.dot(d.astype(BF16), wg, preferred_element_type=F32) * sc
    return (y * _silu(gate)).astype(BF16)


def _pool_prompt_kernel(u_ref, halo_ref, gate_ref, band_ref, wg_ref, sc_ref, o_ref, *, tm, tiles_per_seq, pos0):
    G = wg_ref.shape[-1]
    ti = pl.program_id(0) % tiles_per_seq
    t = pos0 + ti * tm + lax.broadcasted_iota(jnp.int32, (HALO, 1), 0)
    for g, w in enumerate(POOL_WINDOWS):
        c = slice(g * G, (g + 1) * G)
        u = u_ref[:, c]
        d_main = jnp.dot(band_ref[g], u, preferred_element_type=F32).astype(BF16)
        u0 = u[:HALO].astype(F32)
        h0 = jnp.where(ti == 0, 0.0, halo_ref[:, c].astype(F32))
        s0 = _window_sum(jnp.concatenate([h0, u0], axis=0), w)[HALO:, :]
        d0 = s0 * (1.0 / jnp.minimum(t + 1, w).astype(F32)) - u0
        d = jnp.concatenate([d0.astype(BF16), d_main[HALO:]], axis=0)
        y = jnp.dot(d, wg_ref[g], preferred_element_type=F32) * sc_ref[:, c]
        o_ref[:, c] = (y * _silu(gate_ref[:, c].astype(F32))).astype(BF16)


def _pool_bands(tm):
    r = jnp.arange(tm)[:, None] - jnp.arange(tm)[None, :]
    return jnp.stack([jnp.where(r == 0, 1.0 / w - 1.0, jnp.where((r > 0) & (r < w), 1.0 / w, 0.0))
                      for w in POOL_WINDOWS]).astype(BF16)


def _pool_prompt(z, wg, layer, sc, *, T, tm, name):
    M = z.shape[0]
    E = z.shape[1] // 2
    hb = tm // HALO
    bands = _pool_bands(tm)
    return pl.pallas_call(
        functools.partial(_pool_prompt_kernel, tm=tm, tiles_per_seq=T // tm, pos0=0),
        out_shape=jax.ShapeDtypeStruct((M, E), BF16),
        grid=(M // tm,),
        in_specs=[
            pl.BlockSpec((tm, E), lambda i: (i, 0)),
            pl.BlockSpec((HALO, E), lambda i: (jnp.maximum(i * hb - 1, 0), 0)),
            pl.BlockSpec((tm, E), lambda i: (i, 1)),
            pl.BlockSpec(bands.shape, lambda i: (0, 0, 0)),
            pl.BlockSpec((None,) + wg.shape[1:], lambda i: (layer, 0, 0, 0)),
            pl.BlockSpec((1, E), lambda i: (0, 0)),
        ],
        out_specs=pl.BlockSpec((tm, E), lambda i: (i, 0)),
        compiler_params=_params("parallel"),
        name=name,
    )(z, z, z, bands, wg, sc.reshape(1, E))


def _pool_sample_kernel(full_ref, gate_ref, wg_ref, sc_ref, o_ref, *, nseq, T, pos0):
    G = wg_ref.shape[-1]
    seg = HALO + T
    t = pos0 + lax.broadcasted_iota(jnp.int32, (nseq * T, 1), 0) % T

    def tail(a):
        return a.reshape(nseq, seg, G)[:, HALO:, :].reshape(nseq * T, G)

    for g, w in enumerate(POOL_WINDOWS):
        c = slice(g * G, (g + 1) * G)
        full = full_ref[:, c]
        inv_cnt = 1.0 / jnp.minimum(t + 1, w).astype(F32)
        o_ref[:, c] = _pool_finish(tail(_window_sum(full, w)), tail(full), inv_cnt,
                                   gate_ref[:, c].astype(F32), wg_ref[g], sc_ref[:, c])


def _pool_sample(full, z, wg, layer, sc, *, nseq, T, pos0, name):
    E = full.shape[1]
    return pl.pallas_call(
        functools.partial(_pool_sample_kernel, nseq=nseq, T=T, pos0=pos0),
        out_shape=jax.ShapeDtypeStruct((nseq * T, E), BF16),
        grid=(1,),
        in_specs=[
            pl.BlockSpec(full.shape, lambda i: (0, 0)),
            pl.BlockSpec((nseq * T, E), lambda i: (0, 1)),
            pl.BlockSpec((None,) + wg.shape[1:], lambda i: (layer, 0, 0, 0)),
            pl.BlockSpec((1, E), lambda i: (0, 0)),
        ],
        out_specs=pl.BlockSpec((nseq * T, E), lambda i: (0, 0)),
        compiler_params=_params("arbitrary"),
        name=name,
    )(full, z, wg, sc.reshape(1, E))


def _kv_kernel(x_ref, g_ref, wk_ref, wv_ref, wf_ref, bf_ref, k32_ref, v32_ref, k16_ref, v16_ref, lf_ref, *, tn):
    tm = x_ref.shape[0]
    h = _rms_rows(x_ref[...], g_ref[...]).astype(BF16)
    for w_ref, o32_ref, o16_ref in ((wk_ref, k32_ref, k16_ref), (wv_ref, v32_ref, v16_ref)):
        for c in range(w_ref.shape[1] // tn):
            cs = slice(c * tn, (c + 1) * tn)
            acc = jnp.dot(h, w_ref[:, cs], preferred_element_type=F32)
            o16_ref[:, cs] = acc.astype(BF16)
            nh = tn // HEAD_DIM
            o32_ref[:, c * nh:(c + 1) * nh, :] = acc.reshape(tm, nh, HEAD_DIM)
    a = jnp.dot(h, wf_ref[...], preferred_element_type=F32) + bf_ref[...]
    lf_ref[...] = jnp.minimum(a, 0.0) - jnp.log1p(jnp.exp(-jnp.abs(a)))


def _kv_proj(x, g, wk, wv, wf, bf, *, tm, name):
    M, D = x.shape
    N = wk.shape[1]
    L = wf.shape[1]
    resident = dict(pipeline_mode=pl.Buffered(1))
    row = lambda i: (i, 0)
    fixed = lambda i: (0, 0)
    return pl.pallas_call(
        functools.partial(_kv_kernel, tn=1024),
        out_shape=(jax.ShapeDtypeStruct((M, N_HEADS, HEAD_DIM), F32),
                   jax.ShapeDtypeStruct((M, N_HEADS, HEAD_DIM), F32),
                   jax.ShapeDtypeStruct((M, N), BF16), jax.ShapeDtypeStruct((M, N), BF16),
                   jax.ShapeDtypeStruct((M, L), F32)),
        grid=(M // tm,),
        in_specs=[
            pl.BlockSpec((tm, D), row),
            pl.BlockSpec((1, D), fixed),
            pl.BlockSpec((D, N), fixed, **resident),
            pl.BlockSpec((D, N), fixed, **resident),
            pl.BlockSpec((D, L), fixed, **resident),
            pl.BlockSpec((1, L), fixed),
        ],
        out_specs=(pl.BlockSpec((tm, N_HEADS, HEAD_DIM), lambda i: (i, 0, 0)),
                   pl.BlockSpec((tm, N_HEADS, HEAD_DIM), lambda i: (i, 0, 0)),
                   pl.BlockSpec((tm, N), row), pl.BlockSpec((tm, N), row), pl.BlockSpec((tm, L), row)),
        compiler_params=_params("parallel"),
        name=name,
    )(x, g.reshape(1, D), wk, wv, wf, bf.reshape(1, L))


def _forget_cols_kernel(x_ref, o_ref):
    x = x_ref[0]
    n = x.shape[0]
    row = lax.broadcasted_iota(jnp.int32, x.shape, 0)
    lane = lax.broadcasted_iota(jnp.int32, x.shape, 1)
    k = 1
    while k < n:
        x = x + jnp.where(row >= k, pltpu.roll(x, k, 0), 0.0)
        k *= 2
    x = x * LOG2E
    r1 = x - x.astype(BF16).astype(F32)
    r2 = r1 - r1.astype(BF16).astype(F32)
    piece = jnp.where(lane < N_HEADS, x, jnp.where(lane < 2 * N_HEADS, r1, r2))
    o_ref[0] = jnp.where(lane < 3 * N_HEADS, piece, 0.0).astype(BF16)


def _forget_cols(logf_rep, *, name):
    B, n, L = logf_rep.shape
    return pl.pallas_call(
        _forget_cols_kernel,
        out_shape=jax.ShapeDtypeStruct((B, n, L), BF16),
        grid=(B,),
        in_specs=[pl.BlockSpec((1, n, L), lambda b: (b, 0, 0))],
        out_specs=pl.BlockSpec((1, n, L), lambda b: (b, 0, 0)),
        compiler_params=_params("parallel"),
        name=name,
    )(logf_rep)


def _forget_query_cols(h, rows):
    lane = lax.broadcasted_iota(jnp.int32, (rows, LANES), 1)
    return jnp.where((lane % N_HEADS == h) & (lane < 3 * N_HEADS), -1.0, 0.0).astype(BF16)


N_SCORE_BUFS = 3


def _block_pairs(n, n_trips):
    low = [(i, j) for i in range(n) for j in range(i)]
    diag = [(i, i) for i in range(n)]
    lows, diags = len(low) // n_trips, n // n_trips
    assert lows * n_trips == len(low) and diags * n_trips == n and (lows + diags) % N_SCORE_BUFS == 0
    pairs = []
    for t in range(n_trips):
        pairs += low[t * lows:(t + 1) * lows] + diag[t * diags:(t + 1) * diags]
    for i in range(n):
        assert all(pairs.index((i, j)) < pairs.index((i, i)) for j in range(i))
    pairs.append(pairs[-1])
    return lows, diags, jnp.asarray([p[0] for p in pairs], jnp.int32), jnp.asarray([p[1] for p in pairs], jnp.int32)


def _fox_stream_kernel(qt_ref, kb_ref, q_ref, k_ref, v_ref, fa_ref, gate_ref, o_ref, m_sc, acc_sc, s_sc, *,
                       tq, tk, hps, lows, diags, n_trips):
    heads = range(hps)
    cols = [slice(hh * HEAD_DIM, (hh + 1) * HEAD_DIM) for hh in heads]
    m_sc[...] = jnp.full(m_sc.shape, NEG, F32)
    acc_sc[...] = jnp.zeros(acc_sc.shape, F32)
    ones = jnp.ones((ONES_ROWS, tk), BF16)
    fq = [_forget_query_cols(pl.program_id(1) * hps + hh, tq) for hh in heads]

    nt = (((1,), (1,)), ((), ()))
    half = tk // 2

    def scores(p, s_ref, diagonal):
        qoff = pl.multiple_of(qt_ref[p] * tq, tq)
        koff = pl.multiple_of(kb_ref[p] * tk, tk)
        fa = fa_ref[0, pl.ds(koff, tk), :]
        for hh in heads:
            qa = jnp.concatenate([q_ref[0, pl.ds(qoff, tq), cols[hh]], fq[hh]], axis=1)
            ka = jnp.concatenate([k_ref[0, pl.ds(koff, tk), cols[hh]], fa], axis=1)
            if diagonal:
                s_ref[hh, :half, :] = lax.dot_general(ka[:half], qa, nt, preferred_element_type=F32)
                s_ref[hh, half:, half:] = lax.dot_general(ka[half:], qa[half:], nt, preferred_element_type=F32)
            else:
                s_ref[hh] = lax.dot_general(ka, qa, nt, preferred_element_type=F32)

    def absorb(p, s_ref, diagonal):
        i = qt_ref[p]
        koff = pl.multiple_of(kb_ref[p] * tk, tk)
        for hh in heads:
            s = s_ref[hh]
            if diagonal:
                s = jnp.where(lax.broadcasted_iota(jnp.int32, (tk, tq), 0)
                              <= lax.broadcasted_iota(jnp.int32, (tk, tq), 1), s, NEG)
            m_prev = m_sc[hh, i]
            m_new = jnp.maximum(m_prev, jnp.max(s, axis=0, keepdims=True))
            alpha = jnp.exp2(m_prev - m_new)
            pr = jnp.exp2(s - m_new).astype(BF16)
            vt = jnp.concatenate([v_ref[0, pl.ds(koff, tk), cols[hh]].T, ones], axis=0)
            if diagonal:
                pv = jnp.concatenate([jnp.dot(vt[:, :half], pr[:half, :half], preferred_element_type=F32),
                                      jnp.dot(vt, pr[:, half:], preferred_element_type=F32)], axis=1)
            else:
                pv = jnp.dot(vt, pr, preferred_element_type=F32)
            acc = alpha * acc_sc[hh, i] + pv
            m_sc[hh, i] = m_new
            if diagonal:
                qoff = pl.multiple_of(i * tq, tq)
                o = (acc[:HEAD_DIM] * (1.0 / acc[HEAD_DIM:HEAD_DIM + 1])).T
                gate = gate_ref[0, pl.ds(qoff, tq), cols[hh]].astype(F32)
                o_ref[0, pl.ds(qoff, tq), cols[hh]] = (o * _silu(gate)).astype(BF16)
            else:
                acc_sc[hh, i] = acc

    per_trip = lows + diags

    def body(t, c):
        p = t * per_trip
        for u in range(per_trip):
            scores(p + u + 1, s_sc.at[(u + 1) % N_SCORE_BUFS], (u + 1) % per_trip >= lows)
            absorb(p + u, s_sc.at[u % N_SCORE_BUFS], u >= lows)
        return c

    scores(0, s_sc.at[0], lows == 0)
    lax.fori_loop(0, n_trips, body, 0)


def _fox_stream(zb, k16, v16, fa, *, B, T, tq, hps, name):
    assert T % tq == 0 and N_HEADS % hps == 0
    nq = T // tq
    n_trips = nq // 4
    lows, diags, qt, kb = _block_pairs(nq, n_trips)
    w = hps * HEAD_DIM
    seq = lambda b, h, *_: (b, 0, h)
    return pl.pallas_call(
        functools.partial(_fox_stream_kernel, tq=tq, tk=tq, hps=hps, lows=lows, diags=diags, n_trips=n_trips),
        out_shape=jax.ShapeDtypeStruct((B, T, W_B), BF16),
        grid_spec=pltpu.PrefetchScalarGridSpec(
            num_scalar_prefetch=2,
            grid=(B, N_HEADS // hps),
            in_specs=[
                pl.BlockSpec((1, T, w), seq),
                pl.BlockSpec((1, T, w), seq),
                pl.BlockSpec((1, T, w), seq),
                pl.BlockSpec((1, T, LANES), lambda b, h, *_: (b, 0, 0)),
                pl.BlockSpec((1, T, w), lambda b, h, *_: (b, 0, N_HEADS // hps + h)),
            ],
            out_specs=pl.BlockSpec((1, T, w), seq),
            scratch_shapes=[
                pltpu.VMEM((hps, nq, 1, tq), F32),
                pltpu.VMEM((hps, nq, HEAD_DIM + ONES_ROWS, tq), F32),
                pltpu.VMEM((N_SCORE_BUFS, hps, tq, tq), F32),
            ],
        ),
        compiler_params=_params("parallel", "parallel"),
        name=name,
    )(qt, kb, zb, k16, v16, fa, zb)


DEC_STRIDE = 4
DEC_GROUP = N_HEADS // DEC_STRIDE


def _fox_decode_kernel(z_ref, kc_ref, vc_ref, kn_ref, vn_ref, fa_ref, far_ref, o_ref, m_sc, l_sc, acc_sc, *,
                       P, T, pc):
    c = pl.program_id(1)
    nt = (((1,), (1,)), ((), ()))
    heads = [(g, i, g + DEC_STRIDE * i) for g in range(DEC_STRIDE) for i in range(DEC_GROUP)]

    @pl.when(c == 0)
    def _():
        m_sc[...] = jnp.full(m_sc.shape, NEG, F32)
        l_sc[...] = jnp.zeros(l_sc.shape, F32)
        acc_sc[...] = jnp.zeros(acc_sc.shape, F32)

    def qa_of(h):
        return jnp.concatenate([z_ref[:, h * HEAD_DIM:(h + 1) * HEAD_DIM], _forget_query_cols(h, T)], axis=1)

    def update(s, pv_of):
        m_prev = m_sc[...]
        m_new = jnp.maximum(m_prev, jnp.max(s, axis=-1, keepdims=True))
        alpha = jnp.exp2(m_prev - m_new)
        p = jnp.exp2(s - m_new)
        l_sc[...] = alpha * l_sc[...] + jnp.sum(p, axis=-1, keepdims=True)
        acc_sc[...] = alpha * acc_sc[...] + pv_of(p.astype(BF16))
        m_sc[...] = m_new

    n = DEC_GROUP * pc
    gq = DEC_GROUP * T
    far = far_ref[0, pl.ds(pl.multiple_of(c * n, n), n), :]
    s = jnp.concatenate(
        [lax.dot_general(jnp.concatenate([qa_of(g + DEC_STRIDE * i) for i in range(DEC_GROUP)], axis=0),
                         jnp.concatenate([kc_ref[0, pl.ds(g, n, stride=DEC_STRIDE), :].astype(BF16), far], axis=1),
                         nt, preferred_element_type=F32) for g in range(DEC_STRIDE)], axis=0)
    row_i = (lax.broadcasted_iota(jnp.int32, s.shape, 0) // T) % DEC_GROUP
    col_i = lax.broadcasted_iota(jnp.int32, s.shape, 1) % DEC_GROUP
    update(jnp.where(row_i == col_i, s, NEG),
           lambda pb: jnp.concatenate(
               [jnp.dot(pb[g * gq:(g + 1) * gq, :], vc_ref[0, pl.ds(g, n, stride=DEC_STRIDE), :].astype(BF16),
                        preferred_element_type=F32) for g in range(DEC_STRIDE)], axis=0))

    @pl.when(c == pl.num_programs(1) - 1)
    def _():
        rows = lax.broadcasted_iota(jnp.int32, (N_HEADS * T, T), 0) % T
        cols = lax.broadcasted_iota(jnp.int32, (N_HEADS * T, T), 1)
        fa_n = fa_ref[0, P:, :]
        s_n = jnp.concatenate(
            [lax.dot_general(qa_of(h), jnp.concatenate([kn_ref[:, h * HEAD_DIM:(h + 1) * HEAD_DIM], fa_n], axis=1),
                             nt, preferred_element_type=F32) for _, _, h in heads], axis=0)
        update(jnp.where(cols <= rows, s_n, NEG),
               lambda pb: jnp.concatenate(
                   [jnp.dot(pb[r * T:(r + 1) * T, :], vn_ref[:, h * HEAD_DIM:(h + 1) * HEAD_DIM],
                            preferred_element_type=F32) for r, (_, _, h) in enumerate(heads)], axis=0))
        o = acc_sc[...] * (1.0 / l_sc[...])
        for r, (_, _, h) in enumerate(heads):
            hc = slice(h * HEAD_DIM, (h + 1) * HEAD_DIM)
            gate = z_ref[:, W_B + h * HEAD_DIM:W_B + (h + 1) * HEAD_DIM].astype(F32)
            o_ref[:, hc] = (o[r * T:(r + 1) * T, :] * _silu(gate)).astype(BF16)


def _fox_decode(zb, k16, v16, cache_k, cache_v, fa, fa_rep, *, B, T, P, pc, name):
    c_spec = pl.BlockSpec((1, pc * N_HEADS, HEAD_DIM), lambda b, c: (b, c, 0))
    n_spec = pl.BlockSpec((T, W_B), lambda b, c: (b, 0))
    return pl.pallas_call(
        functools.partial(_fox_decode_kernel, P=P, T=T, pc=pc),
        out_shape=jax.ShapeDtypeStruct((B * T, W_B), BF16),
        grid=(B, P // pc),
        in_specs=[
            pl.BlockSpec((T, 2 * W_B), lambda b, c: (b, 0)),
            c_spec, c_spec, n_spec, n_spec,
            pl.BlockSpec((1, P + T, LANES), lambda b, c: (b, 0, 0)),
            pl.BlockSpec((1, P * DEC_GROUP, LANES), lambda b, c: (b, 0, 0)),
        ],
        out_specs=n_spec,
        scratch_shapes=[
            pltpu.VMEM((N_HEADS * T, 1), F32),
            pltpu.VMEM((N_HEADS * T, 1), F32),
            pltpu.VMEM((N_HEADS * T, HEAD_DIM), F32),
        ],
        compiler_params=_params("parallel", "arbitrary"),
        name=name,
    )(zb, cache_k, cache_v, k16, v16, fa, fa_rep)


def _trunk(x3, pos0, pool_prev, past, wts, tag):
    B, T, D = x3.shape
    M = B * T
    x = x3.reshape(M, D)
    prompt = pool_prev is None
    tm = 1024 if prompt else M
    tm_res = 1024 if prompt else M
    tm_kv = 512 if prompt else M
    tn_in = 2048
    tn_out = 1024 if prompt else 2048
    E = wts["w_out_a"].shape[1]
    q_scale = HEAD_DIM ** -0.5 * LOG2E

    new_pool = []
    for l in range(wts["w_in_a"].shape[0]):
        z = _norm_proj(x, wts["norm_a"][l], wts["w_in_a"], l, tm=tm, tn=tn_in, name=f"in_a{l}_{tag}")
        u3 = z[:, :E].reshape(B, T, E) if not prompt else None
        if prompt:
            t = _pool_prompt(z, wts["w_grp_a"], l, wts["scale_a"][l], T=T, tm=POOL_TILE, name=f"pool{l}_{tag}")
            new_pool.append(z.reshape(B, T, 2 * E)[:, T - POOL_PAD:, :E].astype(F32))
        else:
            hist = jnp.pad(pool_prev[l].astype(F32), ((0, 0), (HALO - POOL_PAD, 0), (0, 0)))
            full = jnp.concatenate([hist, u3.astype(F32)], axis=1)
            t = _pool_sample(full.reshape(B * (HALO + T), E), z, wts["w_grp_a"], l, wts["scale_a"][l],
                             nseq=B, T=T, pos0=pos0, name=f"pool{l}_{tag}")
            new_pool.append(full[:, HALO + T - POOL_PAD:, :])
        x = _proj_res(t, wts["w_out_a"], l, x, tm=tm_res, tn=tn_out, name=f"out_a{l}_{tag}")

    k32, v32, k16, v16, logf_rep = _kv_proj(x, wts["norm_kv"], wts["w_k"], wts["w_v"], wts["w_f_rep"],
                                            wts["b_f_rep"], tm=tm_kv, name=f"kv_{tag}")
    logf3 = logf_rep[:, :N_HEADS].reshape(B, T, N_HEADS)
    logf_rep = logf_rep.reshape(B, T, LANES)
    if not prompt:
        past_k, past_v, past_logf = past
        P = past_k.shape[1]
        past_rep = jnp.pad(jnp.tile(past_logf.astype(F32), (1, 1, 3)), ((0, 0), (0, 0), (0, LANES - 3 * N_HEADS)))
        logf_rep = jnp.concatenate([past_rep, logf_rep], axis=1)
        ck = past_k.reshape(B, P * N_HEADS, HEAD_DIM)
        cv = past_v.reshape(B, P * N_HEADS, HEAD_DIM)
    fa = _forget_cols(logf_rep, name=f"fcols_{tag}")
    if not prompt:
        fa_rep = jnp.repeat(fa[:, :P], DEC_GROUP, axis=1)

    n_b = wts["w_in_b"].shape[0]
    for l in range(n_b):
        zb = _norm_proj(x, wts["norm_b"][l], wts["w_in_b"], l, tm=tm, tn=tn_in,
                        n_scaled=W_B // tn_in, scale=q_scale, name=f"in_b{l}_{tag}")
        if prompt:
            og = _fox_stream(zb.reshape(B, T, 2 * W_B), k16.reshape(B, T, W_B), v16.reshape(B, T, W_B), fa,
                             B=B, T=T, tq=ATTN_TILE, hps=ATTN_HEADS_PER_STEP,
                             name=f"attn{l}_{tag}").reshape(M, W_B)
        else:
            og = _fox_decode(zb, k16, v16, ck, cv, fa, fa_rep, B=B, T=T, P=P, pc=DEC_CHUNK,
                             name=f"attn{l}_{tag}")
        if l + 1 < n_b:
            x = _proj_res(og, wts["w_out_b"], l, x, tm=tm_res, tn=tn_out, name=f"out_b{l}_{tag}")
        else:
            y = _proj_res_norm(og, wts["w_out_b"], l, x, wts["norm_f"], tm=min(tm_res, 512),
                               name=f"out_b{l}_{tag}")
    return (y.reshape(B, T, D), k32.reshape(B, T, N_HEADS, HEAD_DIM), v32.reshape(B, T, N_HEADS, HEAD_DIM),
            logf3, jnp.stack(new_pool))


def kernel(x_prompt, x_sample, cache_k, cache_v, cache_logf, state_pool, norm_a, w_in_a, w_grp_a, scale_a,
           w_out_a, norm_kv, w_kv, b_f, norm_b, w_in_b, w_out_b, norm_f):
    wts = dict(
        norm_a=norm_a, w_in_a=w_in_a.astype(BF16), w_grp_a=w_grp_a.astype(BF16), scale_a=scale_a,
        w_out_a=w_out_a.astype(BF16), norm_kv=norm_kv,
        w_k=w_kv[:, :W_B].astype(BF16), w_v=w_kv[:, W_B:2 * W_B].astype(BF16),
        w_f_rep=jnp.pad(jnp.tile(w_kv[:, 2 * W_B:], (1, 3)), ((0, 0), (0, LANES - 3 * N_HEADS))).astype(BF16),
        b_f_rep=jnp.pad(jnp.tile(b_f, 3), (0, LANES - 3 * N_HEADS)),
        norm_b=norm_b, w_in_b=w_in_b.astype(BF16), w_out_b=w_out_b.astype(BF16), norm_f=norm_f,
    )
    y_p, k_p, v_p, lf_p, pool_p = _trunk(x_prompt, 0, None, None, wts, "p")
    y_s, k_s, v_s, lf_s, pool_s = _trunk(x_sample, cache_k.shape[1], state_pool,
                                         (cache_k, cache_v, cache_logf), wts, "s")
    return (y_p, y_s, k_p, v_p, lf_p, pool_p, k_s, v_s, lf_s, pool_s)
```

```python
import functools

import jax
import jax.numpy as jnp
from jax import lax
from jax.experimental import pallas as pl
from jax.experimental.pallas import tpu as pltpu

F32 = jnp.float32
BF16 = jnp.bfloat16

EPS = 1e-6
N_HEADS = 16
HEAD_DIM = 128
W_B = N_HEADS * HEAD_DIM
POOL_WINDOWS = (2, 4, 8, 16)
POOL_PAD = max(POOL_WINDOWS) - 1
HALO = 16
NEG = -1e30
LOG2E = 1.4426950408889634
ONES_ROWS = 16
LANES = 128
MXU_DEPTH = 256
VMEM_LIMIT = 56 * 1024 * 1024
POOL_TILE = MXU_DEPTH
ATTN_TILE = 512
ATTN_HEADS_PER_STEP = 2
DEC_CHUNK = 1024


def _params(*sem):
    return pltpu.CompilerParams(dimension_semantics=sem, vmem_limit_bytes=VMEM_LIMIT)


def _silu(g):
    return g * (1.0 / (1.0 + jnp.exp(-g)))


def _rms_rows(xf, g):
    r = lax.rsqrt(jnp.mean(xf * xf, axis=-1, keepdims=True) + EPS)
    return (xf * r) * g


def _norm_proj_kernel(x_ref, g_ref, w_ref, *refs, n_scaled, scale, n_side):
    side_in, o_ref, side_out, h_ref = refs[:n_side], refs[n_side], refs[n_side + 1:-1], refs[-1]
    j = pl.program_id(1)

    @pl.when(j == 0)
    def _():
        h_ref[...] = _rms_rows(x_ref[...], g_ref[...]).astype(BF16)

    acc = jnp.dot(h_ref[...], w_ref[...], preferred_element_type=F32)
    if n_scaled:
        acc = acc * jnp.where(j < n_scaled, scale, 1.0)
    o_ref[...] = acc.astype(o_ref.dtype)
    for s_ref, c_ref in zip(side_in, side_out):
        c_ref[...] = s_ref[...].astype(BF16)


def _norm_proj(x, g, w, layer, *, tm, tn, n_scaled=0, scale=1.0, side=(), name):
    M, D = x.shape
    N = w.shape[2]
    n_i, n_j = M // tm, N // tn
    steps = n_i * n_j
    side_in, side_out, side_shapes = [], [], []
    for s, row0 in side:
        rows = (s.shape[0] - row0) // steps
        assert rows % 16 == 0 and rows * steps == s.shape[0] - row0 and row0 % rows == 0
        side_in.append(pl.BlockSpec((rows, s.shape[1]), lambda i, j, b0=row0 // rows: (b0 + i * n_j + j, 0)))
        side_out.append(pl.BlockSpec((rows, s.shape[1]), lambda i, j: (i * n_j + j, 0)))
        side_shapes.append(jax.ShapeDtypeStruct((s.shape[0] - row0, s.shape[1]), BF16))
    out = pl.pallas_call(
        functools.partial(_norm_proj_kernel, n_scaled=n_scaled, scale=scale, n_side=len(side)),
        out_shape=[jax.ShapeDtypeStruct((M, N), BF16)] + side_shapes,
        grid=(n_i, n_j),
        in_specs=[
            pl.BlockSpec((tm, D), lambda i, j: (i, 0)),
            pl.BlockSpec((1, D), lambda i, j: (0, 0)),
            pl.BlockSpec((None, D, tn), lambda i, j: (layer, 0, j)),
        ] + side_in,
        out_specs=[pl.BlockSpec((tm, tn), lambda i, j: (i, j))] + side_out,
        scratch_shapes=[pltpu.VMEM((tm, D), BF16)],
        compiler_params=_params("parallel", "arbitrary"),
        name=name,
    )(x, g.reshape(1, D), w, *[s for s, _ in side])
    return out[0], out[1:]


def _proj_res_kernel(a_ref, w_ref, x_ref, o_ref):
    o_ref[...] = x_ref[...] + jnp.dot(a_ref[...], w_ref[...], preferred_element_type=F32)


def _proj_res(a, w, layer, x, *, tm, tn, name):
    M, K = a.shape
    N = w.shape[2]
    return pl.pallas_call(
        _proj_res_kernel,
        out_shape=jax.ShapeDtypeStruct((M, N), F32),
        grid=(N // tn, M // tm),
        in_specs=[
            pl.BlockSpec((tm, K), lambda j, i: (i, 0)),
            pl.BlockSpec((None, K, tn), lambda j, i: (layer, 0, j)),
            pl.BlockSpec((tm, tn), lambda j, i: (i, j)),
        ],
        out_specs=pl.BlockSpec((tm, tn), lambda j, i: (i, j)),
        compiler_params=_params("parallel", "parallel"),
        name=name,
    )(a, w, x)


def _proj_res_norm_kernel(a_ref, w_ref, x_ref, g_ref, o_ref):
    x = x_ref[...] + jnp.dot(a_ref[...], w_ref[...], preferred_element_type=F32)
    o_ref[...] = _rms_rows(x, g_ref[...])


def _proj_res_norm(a, w, layer, x, g, *, tm, name):
    M, K = a.shape
    N = w.shape[2]
    return pl.pallas_call(
        _proj_res_norm_kernel,
        out_shape=jax.ShapeDtypeStruct((M, N), F32),
        grid=(M // tm,),
        in_specs=[
            pl.BlockSpec((tm, K), lambda i: (i, 0)),
            pl.BlockSpec((None, K, N), lambda i: (layer, 0, 0)),
            pl.BlockSpec((tm, N), lambda i: (i, 0)),
            pl.BlockSpec((1, N), lambda i: (0, 0)),
        ],
        out_specs=pl.BlockSpec((tm, N), lambda i: (i, 0)),
        compiler_params=_params("parallel"),
        name=name,
    )(a, w, x, g.reshape(1, N))


def _window_sum(full, w):
    s = full
    k = 1
    while k < w:
        s = s + pltpu.roll(s, k, 0)
        k *= 2
    return s


def _pool_finish(s, uf, inv_cnt, gate, wg, sc):
    d = s * inv_cnt - uf
    y = jnp.dot(d.astype(BF16), wg, preferred_element_type=F32) * sc
    return (y * _silu(gate)).astype(BF16)


def _pool_prompt_kernel(u_ref, halo_ref, gate_ref, band_ref, wg_ref, sc_ref, o_ref, *, tm, tiles_per_seq, pos0):
    G = wg_ref.shape[-1]
    ti = pl.program_id(0) % tiles_per_seq
    t = pos0 + ti * tm + lax.broadcasted_iota(jnp.int32, (HALO, 1), 0)
    for g, w in enumerate(POOL_WINDOWS):
        c = slice(g * G, (g + 1) * G)
        u = u_ref[:, c]
        d_main = jnp.dot(band_ref[g], u, preferred_element_type=F32).astype(BF16)
        u0 = u[:HALO].astype(F32)
        h0 = jnp.where(ti == 0, 0.0, halo_ref[:, c].astype(F32))
        s0 = _window_sum(jnp.concatenate([h0, u0], axis=0), w)[HALO:, :]
        d0 = s0 * (1.0 / jnp.minimum(t + 1, w).astype(F32)) - u0
        d = jnp.concatenate([d0.astype(BF16), d_main[HALO:]], axis=0)
        y = jnp.dot(d, wg_ref[g], preferred_element_type=F32) * sc_ref[:, c]
        o_ref[:, c] = (y * _silu(gate_ref[:, c].astype(F32))).astype(BF16)


def _pool_bands(tm):
    r = jnp.arange(tm)[:, None] - jnp.arange(tm)[None, :]
    return jnp.stack([jnp.where(r == 0, 1.0 / w - 1.0, jnp.where((r > 0) & (r < w), 1.0 / w, 0.0))
                      for w in POOL_WINDOWS]).astype(BF16)


def _pool_prompt(z, wg, layer, sc, *, T, tm, name):
    M = z.shape[0]
    E = z.shape[1] // 2
    hb = tm // HALO
    bands = _pool_bands(tm)
    return pl.pallas_call(
        functools.partial(_pool_prompt_kernel, tm=tm, tiles_per_seq=T // tm, pos0=0),
        out_shape=jax.ShapeDtypeStruct((M, E), BF16),
        grid=(M // tm,),
        in_specs=[
            pl.BlockSpec((tm, E), lambda i: (i, 0)),
            pl.BlockSpec((HALO, E), lambda i: (jnp.maximum(i * hb - 1, 0), 0)),
            pl.BlockSpec((tm, E), lambda i: (i, 1)),
            pl.BlockSpec(bands.shape, lambda i: (0, 0, 0)),
            pl.BlockSpec((None,) + wg.shape[1:], lambda i: (layer, 0, 0, 0)),
            pl.BlockSpec((1, E), lambda i: (0, 0)),
        ],
        out_specs=pl.BlockSpec((tm, E), lambda i: (i, 0)),
        compiler_params=_params("parallel"),
        name=name,
    )(z, z, z, bands, wg, sc.reshape(1, E))


def _pool_sample_kernel(full_ref, gate_ref, wg_ref, sc_ref, o_ref, *, nseq, T, pos0):
    G = wg_ref.shape[-1]
    seg = HALO + T
    t = pos0 + lax.broadcasted_iota(jnp.int32, (nseq * T, 1), 0) % T

    def tail(a):
        return a.reshape(nseq, seg, G)[:, HALO:, :].reshape(nseq * T, G)

    for g, w in enumerate(POOL_WINDOWS):
        c = slice(g * G, (g + 1) * G)
        full = full_ref[:, c]
        inv_cnt = 1.0 / jnp.minimum(t + 1, w).astype(F32)
        o_ref[:, c] = _pool_finish(tail(_window_sum(full, w)), tail(full), inv_cnt,
                                   gate_ref[:, c].astype(F32), wg_ref[g], sc_ref[:, c])


def _pool_sample(full, z, wg, layer, sc, *, nseq, T, pos0, name):
    E = full.shape[1]
    return pl.pallas_call(
        functools.partial(_pool_sample_kernel, nseq=nseq, T=T, pos0=pos0),
        out_shape=jax.ShapeDtypeStruct((nseq * T, E), BF16),
        grid=(1,),
        in_specs=[
            pl.BlockSpec(full.shape, lambda i: (0, 0)),
            pl.BlockSpec((nseq * T, E), lambda i: (0, 1)),
            pl.BlockSpec((None,) + wg.shape[1:], lambda i: (layer, 0, 0, 0)),
            pl.BlockSpec((1, E), lambda i: (0, 0)),
        ],
        out_specs=pl.BlockSpec((nseq * T, E), lambda i: (0, 0)),
        compiler_params=_params("arbitrary"),
        name=name,
    )(full, z, wg, sc.reshape(1, E))


def _kv_kernel(x_ref, g_ref, wk_ref, wv_ref, wf_ref, bf_ref, k32_ref, v32_ref, k16_ref, v16_ref, lf_ref, *, tn):
    tm = x_ref.shape[0]
    h = _rms_rows(x_ref[...], g_ref[...]).astype(BF16)
    for w_ref, o32_ref, o16_ref in ((wk_ref, k32_ref, k16_ref), (wv_ref, v32_ref, v16_ref)):
        for c in range(w_ref.shape[1] // tn):
            cs = slice(c * tn, (c + 1) * tn)
            acc = jnp.dot(h, w_ref[:, cs], preferred_element_type=F32)
            o16_ref[:, cs] = acc.astype(BF16)
            nh = tn // HEAD_DIM
            o32_ref[:, c * nh:(c + 1) * nh, :] = acc.reshape(tm, nh, HEAD_DIM)
    a = jnp.dot(h, wf_ref[...], preferred_element_type=F32) + bf_ref[...]
    lf_ref[...] = jnp.minimum(a, 0.0) - jnp.log1p(jnp.exp(-jnp.abs(a)))


def _kv_proj(x, g, wkv, wf, bf, *, tm, name):
    M, D = x.shape
    N = W_B
    L = wf.shape[1]
    resident = dict(pipeline_mode=pl.Buffered(1))
    row = lambda i: (i, 0)
    fixed = lambda i: (0, 0)
    return pl.pallas_call(
        functools.partial(_kv_kernel, tn=1024),
        out_shape=(jax.ShapeDtypeStruct((M, N_HEADS, HEAD_DIM), F32),
                   jax.ShapeDtypeStruct((M, N_HEADS, HEAD_DIM), F32),
                   jax.ShapeDtypeStruct((M, N), BF16), jax.ShapeDtypeStruct((M, N), BF16),
                   jax.ShapeDtypeStruct((M, L), F32)),
        grid=(M // tm,),
        in_specs=[
            pl.BlockSpec((tm, D), row),
            pl.BlockSpec((1, D), fixed),
            pl.BlockSpec((D, N), fixed, **resident),
            pl.BlockSpec((D, N), lambda i: (0, 1), **resident),
            pl.BlockSpec((D, L), fixed, **resident),
            pl.BlockSpec((1, L), fixed),
        ],
        out_specs=(pl.BlockSpec((tm, N_HEADS, HEAD_DIM), lambda i: (i, 0, 0)),
                   pl.BlockSpec((tm, N_HEADS, HEAD_DIM), lambda i: (i, 0, 0)),
                   pl.BlockSpec((tm, N), row), pl.BlockSpec((tm, N), row), pl.BlockSpec((tm, L), row)),
        compiler_params=_params("parallel"),
        name=name,
    )(x, g.reshape(1, D), wkv, wkv, wf, bf.reshape(1, L))


def _forget_cols_kernel(x_ref, o_ref):
    x = x_ref[0]
    n = x.shape[0]
    row = lax.broadcasted_iota(jnp.int32, x.shape, 0)
    lane = lax.broadcasted_iota(jnp.int32, x.shape, 1)
    k = 1
    while k < n:
        x = x + jnp.where(row >= k, pltpu.roll(x, k, 0), 0.0)
        k *= 2
    x = x * LOG2E
    r1 = x - x.astype(BF16).astype(F32)
    r2 = r1 - r1.astype(BF16).astype(F32)
    piece = jnp.where(lane < N_HEADS, x, jnp.where(lane < 2 * N_HEADS, r1, r2))
    o_ref[0] = jnp.where(lane < 3 * N_HEADS, piece, 0.0).astype(BF16)


def _forget_cols(logf_rep, *, name):
    B, n, L = logf_rep.shape
    return pl.pallas_call(
        _forget_cols_kernel,
        out_shape=jax.ShapeDtypeStruct((B, n, L), BF16),
        grid=(B,),
        in_specs=[pl.BlockSpec((1, n, L), lambda b: (b, 0, 0))],
        out_specs=pl.BlockSpec((1, n, L), lambda b: (b, 0, 0)),
        compiler_params=_params("parallel"),
        name=name,
    )(logf_rep)


def _forget_query_cols(h, rows):
    lane = lax.broadcasted_iota(jnp.int32, (rows, LANES), 1)
    return jnp.where((lane % N_HEADS == h) & (lane < 3 * N_HEADS), -1.0, 0.0).astype(BF16)


N_SCORE_BUFS = 3


def _block_pairs(n, n_trips):
    low = [(i, j) for i in range(n) for j in range(i)]
    diag = [(i, i) for i in range(n)]
    lows, diags = len(low) // n_trips, n // n_trips
    assert lows * n_trips == len(low) and diags * n_trips == n and (lows + diags) % N_SCORE_BUFS == 0
    pairs = []
    for t in range(n_trips):
        pairs += low[t * lows:(t + 1) * lows] + diag[t * diags:(t + 1) * diags]
    for i in range(n):
        assert all(pairs.index((i, j)) < pairs.index((i, i)) for j in range(i))
    pairs.append(pairs[-1])
    return lows, diags, jnp.asarray([p[0] for p in pairs], jnp.int32), jnp.asarray([p[1] for p in pairs], jnp.int32)


def _fox_stream_kernel(qt_ref, kb_ref, q_ref, k_ref, v_ref, fa_ref, gate_ref, o_ref, m_sc, acc_sc, s_sc, *,
                       tq, tk, hps, lows, diags, n_trips):
    heads = range(hps)
    cols = [slice(hh * HEAD_DIM, (hh + 1) * HEAD_DIM) for hh in heads]
    m_sc[...] = jnp.full(m_sc.shape, NEG, F32)
    acc_sc[...] = jnp.zeros(acc_sc.shape, F32)
    ones = jnp.ones((ONES_ROWS, tk), BF16)
    fq = [_forget_query_cols(pl.program_id(1) * hps + hh, tq) for hh in heads]

    nt = (((1,), (1,)), ((), ()))
    half = tk // 2

    def scores(p, s_ref, diagonal):
        qoff = pl.multiple_of(qt_ref[p] * tq, tq)
        koff = pl.multiple_of(kb_ref[p] * tk, tk)
        fa = fa_ref[0, pl.ds(koff, tk), :]
        for hh in heads:
            qa = jnp.concatenate([q_ref[0, pl.ds(qoff, tq), cols[hh]], fq[hh]], axis=1)
            ka = jnp.concatenate([k_ref[0, pl.ds(koff, tk), cols[hh]], fa], axis=1)
            if diagonal:
                s_ref[hh, :half, :] = lax.dot_general(ka[:half], qa, nt, preferred_element_type=F32)
                s_ref[hh, half:, half:] = lax.dot_general(ka[half:], qa[half:], nt, preferred_element_type=F32)
            else:
                s_ref[hh] = lax.dot_general(ka, qa, nt, preferred_element_type=F32)

    def absorb(p, s_ref, diagonal):
        i = qt_ref[p]
        koff = pl.multiple_of(kb_ref[p] * tk, tk)
        for hh in heads:
            s = s_ref[hh]
            if diagonal:
                s = jnp.where(lax.broadcasted_iota(jnp.int32, (tk, tq), 0)
                              <= lax.broadcasted_iota(jnp.int32, (tk, tq), 1), s, NEG)
            m_prev = m_sc[hh, i]
            m_new = jnp.maximum(m_prev, jnp.max(s, axis=0, keepdims=True))
            alpha = jnp.exp2(m_prev - m_new)
            pr = jnp.exp2(s - m_new).astype(BF16)
            vt = jnp.concatenate([v_ref[0, pl.ds(koff, tk), cols[hh]].T, ones], axis=0)
            if diagonal:
                pv = jnp.concatenate([jnp.dot(vt[:, :half], pr[:half, :half], preferred_element_type=F32),
                                      jnp.dot(vt, pr[:, half:], preferred_element_type=F32)], axis=1)
            else:
                pv = jnp.dot(vt, pr, preferred_element_type=F32)
            acc = alpha * acc_sc[hh, i] + pv
            m_sc[hh, i] = m_new
            if diagonal:
                qoff = pl.multiple_of(i * tq, tq)
                o = (acc[:HEAD_DIM] * (1.0 / acc[HEAD_DIM:HEAD_DIM + 1])).T
                gate = gate_ref[0, pl.ds(qoff, tq), cols[hh]].astype(F32)
                o_ref[0, pl.ds(qoff, tq), cols[hh]] = (o * _silu(gate)).astype(BF16)
            else:
                acc_sc[hh, i] = acc

    per_trip = lows + diags

    def body(t, c):
        p = t * per_trip
        for u in range(per_trip):
            scores(p + u + 1, s_sc.at[(u + 1) % N_SCORE_BUFS], (u + 1) % per_trip >= lows)
            absorb(p + u, s_sc.at[u % N_SCORE_BUFS], u >= lows)
        return c

    scores(0, s_sc.at[0], lows == 0)
    lax.fori_loop(0, n_trips, body, 0)


def _fox_stream(zb, k16, v16, fa, *, B, T, tq, hps, name):
    assert T % tq == 0 and N_HEADS % hps == 0
    nq = T // tq
    n_trips = nq // 4
    lows, diags, qt, kb = _block_pairs(nq, n_trips)
    w = hps * HEAD_DIM
    seq = lambda b, h, *_: (b, 0, h)
    return pl.pallas_call(
        functools.partial(_fox_stream_kernel, tq=tq, tk=tq, hps=hps, lows=lows, diags=diags, n_trips=n_trips),
        out_shape=jax.ShapeDtypeStruct((B, T, W_B), BF16),
        grid_spec=pltpu.PrefetchScalarGridSpec(
            num_scalar_prefetch=2,
            grid=(B, N_HEADS // hps),
            in_specs=[
                pl.BlockSpec((1, T, w), seq),
                pl.BlockSpec((1, T, w), seq),
                pl.BlockSpec((1, T, w), seq),
                pl.BlockSpec((1, T, LANES), lambda b, h, *_: (b, 0, 0)),
                pl.BlockSpec((1, T, w), lambda b, h, *_: (b, 0, N_HEADS // hps + h)),
            ],
            out_specs=pl.BlockSpec((1, T, w), seq),
            scratch_shapes=[
                pltpu.VMEM((hps, nq, 1, tq), F32),
                pltpu.VMEM((hps, nq, HEAD_DIM + ONES_ROWS, tq), F32),
                pltpu.VMEM((N_SCORE_BUFS, hps, tq, tq), F32),
            ],
        ),
        compiler_params=_params("parallel", "parallel"),
        name=name,
    )(qt, kb, zb, k16, v16, fa, zb)


DEC_STRIDE = 4
DEC_GROUP = N_HEADS // DEC_STRIDE


def _fox_decode_kernel(z_ref, kc_ref, vc_ref, kn_ref, vn_ref, fa_ref, far_ref, o_ref, m_sc, l_sc, acc_sc, *,
                       P, T, pc):
    c = pl.program_id(1)
    nt = (((1,), (1,)), ((), ()))
    heads = [(g, i, g + DEC_STRIDE * i) for g in range(DEC_STRIDE) for i in range(DEC_GROUP)]

    @pl.when(c == 0)
    def _():
        m_sc[...] = jnp.full(m_sc.shape, NEG, F32)
        l_sc[...] = jnp.zeros(l_sc.shape, F32)
        acc_sc[...] = jnp.zeros(acc_sc.shape, F32)

    def qa_of(h):
        return jnp.concatenate([z_ref[:, h * HEAD_DIM:(h + 1) * HEAD_DIM], _forget_query_cols(h, T)], axis=1)

    def update(s, pv_of):
        m_prev = m_sc[...]
        m_new = jnp.maximum(m_prev, jnp.max(s, axis=-1, keepdims=True))
        alpha = jnp.exp2(m_prev - m_new)
        p = jnp.exp2(s - m_new)
        l_sc[...] = alpha * l_sc[...] + jnp.sum(p, axis=-1, keepdims=True)
        acc_sc[...] = alpha * acc_sc[...] + pv_of(p.astype(BF16))
        m_sc[...] = m_new

    n = DEC_GROUP * pc
    gq = DEC_GROUP * T
    far = far_ref[0, pl.ds(pl.multiple_of(c * n, n), n), :]
    s = jnp.concatenate(
        [lax.dot_general(jnp.concatenate([qa_of(g + DEC_STRIDE * i) for i in range(DEC_GROUP)], axis=0),
                         jnp.concatenate([kc_ref[0, pl.ds(g, n, stride=DEC_STRIDE), :].astype(BF16), far], axis=1),
                         nt, preferred_element_type=F32) for g in range(DEC_STRIDE)], axis=0)
    row_i = (lax.broadcasted_iota(jnp.int32, s.shape, 0) // T) % DEC_GROUP
    col_i = lax.broadcasted_iota(jnp.int32, s.shape, 1) % DEC_GROUP
    update(jnp.where(row_i == col_i, s, NEG),
           lambda pb: jnp.concatenate(
               [jnp.dot(pb[g * gq:(g + 1) * gq, :], vc_ref[0, pl.ds(g, n, stride=DEC_STRIDE), :].astype(BF16),
                        preferred_element_type=F32) for g in range(DEC_STRIDE)], axis=0))

    @pl.when(c == pl.num_programs(1) - 1)
    def _():
        rows = lax.broadcasted_iota(jnp.int32, (N_HEADS * T, T), 0) % T
        cols = lax.broadcasted_iota(jnp.int32, (N_HEADS * T, T), 1)
        fa_n = fa_ref[0, P:, :]
        s_n = jnp.concatenate(
            [lax.dot_general(qa_of(h), jnp.concatenate([kn_ref[:, h * HEAD_DIM:(h + 1) * HEAD_DIM], fa_n], axis=1),
                             nt, preferred_element_type=F32) for _, _, h in heads], axis=0)
        update(jnp.where(cols <= rows, s_n, NEG),
               lambda pb: jnp.concatenate(
                   [jnp.dot(pb[r * T:(r + 1) * T, :], vn_ref[:, h * HEAD_DIM:(h + 1) * HEAD_DIM],
                            preferred_element_type=F32) for r, (_, _, h) in enumerate(heads)], axis=0))
        o = acc_sc[...] * (1.0 / l_sc[...])
        for r, (_, _, h) in enumerate(heads):
            hc = slice(h * HEAD_DIM, (h + 1) * HEAD_DIM)
            gate = z_ref[:, W_B + h * HEAD_DIM:W_B + (h + 1) * HEAD_DIM].astype(F32)
            o_ref[:, hc] = (o[r * T:(r + 1) * T, :] * _silu(gate)).astype(BF16)


def _fox_decode(zb, k16, v16, cache_k, cache_v, fa, fa_rep, *, B, T, P, pc, name):
    c_spec = pl.BlockSpec((1, pc * N_HEADS, HEAD_DIM), lambda b, c: (b, c, 0))
    n_spec = pl.BlockSpec((T, W_B), lambda b, c: (b, 0))
    return pl.pallas_call(
        functools.partial(_fox_decode_kernel, P=P, T=T, pc=pc),
        out_shape=jax.ShapeDtypeStruct((B * T, W_B), BF16),
        grid=(B, P // pc),
        in_specs=[
            pl.BlockSpec((T, 2 * W_B), lambda b, c: (b, 0)),
            c_spec, c_spec, n_spec, n_spec,
            pl.BlockSpec((1, P + T, LANES), lambda b, c: (b, 0, 0)),
            pl.BlockSpec((1, P * DEC_GROUP, LANES), lambda b, c: (b, 0, 0)),
        ],
        out_specs=n_spec,
        scratch_shapes=[
            pltpu.VMEM((N_HEADS * T, 1), F32),
            pltpu.VMEM((N_HEADS * T, 1), F32),
            pltpu.VMEM((N_HEADS * T, HEAD_DIM), F32),
        ],
        compiler_params=_params("parallel", "arbitrary"),
        name=name,
    )(zb, cache_k, cache_v, k16, v16, fa, fa_rep)


def _trunk(x3, pos0, pool_prev, past, wts, tag):
    B, T, D = x3.shape
    M = B * T
    x = x3.reshape(M, D)
    prompt = pool_prev is None
    tm = 1024 if prompt else M
    tm_res = 1024 if prompt else M
    tm_kv = 512 if prompt else M
    tn_in = 2048
    tn_out = 1024 if prompt else 2048
    E = wts["w_in_a0"].shape[2] // 2
    q_scale = HEAD_DIM ** -0.5 * LOG2E

    new_pool = []
    for l in range(wts["norm_a"].shape[0]):
        jobs = wts["to_cast"].pop(l, ()) if prompt else ()
        z, casts = _norm_proj(x, wts["norm_a"][l], wts[f"w_in_a{l}"], 0, tm=tm, tn=tn_in, name=f"in_a{l}_{tag}",
                              side=[(w.reshape(-1, w.shape[-1]), row0) for _, w, row0, _ in jobs])
        for (key, _, _, shape), c in zip(jobs, casts):
            wts[key] = c.reshape(shape)
        u3 = z[:, :E].reshape(B, T, E) if not prompt else None
        if prompt:
            t = _pool_prompt(z, wts["w_grp_a"], l, wts["scale_a"][l], T=T, tm=POOL_TILE, name=f"pool{l}_{tag}")
            new_pool.append(z.reshape(B, T, 2 * E)[:, T - POOL_PAD:, :E].astype(F32))
        else:
            hist = jnp.pad(pool_prev[l].astype(F32), ((0, 0), (HALO - POOL_PAD, 0), (0, 0)))
            full = jnp.concatenate([hist, u3.astype(F32)], axis=1)
            t = _pool_sample(full.reshape(B * (HALO + T), E), z, wts["w_grp_a"], l, wts["scale_a"][l],
                             nseq=B, T=T, pos0=pos0, name=f"pool{l}_{tag}")
            new_pool.append(full[:, HALO + T - POOL_PAD:, :])
        x = _proj_res(t, wts["w_out_a"], l, x, tm=tm_res, tn=tn_out, name=f"out_a{l}_{tag}")

    k32, v32, k16, v16, logf_rep = _kv_proj(x, wts["norm_kv"], wts["w_kv"], wts["w_f_rep"],
                                            wts["b_f_rep"], tm=tm_kv, name=f"kv_{tag}")
    logf3 = logf_rep[:, :N_HEADS].reshape(B, T, N_HEADS)
    logf_rep = logf_rep.reshape(B, T, LANES)
    if not prompt:
        past_k, past_v, past_logf = past
        P = past_k.shape[1]
        past_rep = jnp.pad(jnp.tile(past_logf.astype(F32), (1, 1, 3)), ((0, 0), (0, 0), (0, LANES - 3 * N_HEADS)))
        logf_rep = jnp.concatenate([past_rep, logf_rep], axis=1)
        ck = past_k.reshape(B, P * N_HEADS, HEAD_DIM)
        cv = past_v.reshape(B, P * N_HEADS, HEAD_DIM)
    fa = _forget_cols(logf_rep, name=f"fcols_{tag}")
    if not prompt:
        fa_rep = jnp.repeat(fa[:, :P], DEC_GROUP, axis=1)

    n_b = wts["w_in_b"].shape[0]
    for l in range(n_b):
        zb, _ = _norm_proj(x, wts["norm_b"][l], wts["w_in_b"], l, tm=tm, tn=tn_in,
                           n_scaled=W_B // tn_in, scale=q_scale, name=f"in_b{l}_{tag}")
        if prompt:
            og = _fox_stream(zb.reshape(B, T, 2 * W_B), k16.reshape(B, T, W_B), v16.reshape(B, T, W_B), fa,
                             B=B, T=T, tq=ATTN_TILE, hps=ATTN_HEADS_PER_STEP,
                             name=f"attn{l}_{tag}").reshape(M, W_B)
        else:
            og = _fox_decode(zb, k16, v16, ck, cv, fa, fa_rep, B=B, T=T, P=P, pc=DEC_CHUNK,
                             name=f"attn{l}_{tag}")
        if l + 1 < n_b:
            x = _proj_res(og, wts["w_out_b"], l, x, tm=tm_res, tn=tn_out, name=f"out_b{l}_{tag}")
        else:
            y = _proj_res_norm(og, wts["w_out_b"], l, x, wts["norm_f"], tm=min(tm_res, 512),
                               name=f"out_b{l}_{tag}")
    return (y.reshape(B, T, D), k32.reshape(B, T, N_HEADS, HEAD_DIM), v32.reshape(B, T, N_HEADS, HEAD_DIM),
            logf3, jnp.stack(new_pool))


def kernel(x_prompt, x_sample, cache_k, cache_v, cache_logf, state_pool, norm_a, w_in_a, w_grp_a, scale_a,
           w_out_a, norm_kv, w_kv, b_f, norm_b, w_in_b, w_out_b, norm_f):
    assert w_in_a.shape[0] == 2
    wts = dict(
        norm_a=norm_a, w_in_a0=w_in_a[:1].astype(BF16), scale_a=scale_a, norm_kv=norm_kv,
        w_kv=w_kv.astype(BF16),
        w_f_rep=jnp.pad(jnp.tile(w_kv[:, 2 * W_B:], (1, 3)), ((0, 0), (0, LANES - 3 * N_HEADS))).astype(BF16),
        b_f_rep=jnp.pad(jnp.tile(b_f, 3), (0, LANES - 3 * N_HEADS)),
        norm_b=norm_b, norm_f=norm_f,
        to_cast={0: (("w_in_a1", w_in_a, w_in_a.shape[1], (1,) + w_in_a.shape[1:]),
                     ("w_out_a", w_out_a, 0, w_out_a.shape), ("w_grp_a", w_grp_a, 0, w_grp_a.shape)),
                 1: (("w_in_b", w_in_b, 0, w_in_b.shape), ("w_out_b", w_out_b, 0, w_out_b.shape))},
    )
    y_p, k_p, v_p, lf_p, pool_p = _trunk(x_prompt, 0, None, None, wts, "p")
    y_s, k_s, v_s, lf_s, pool_s = _trunk(x_sample, cache_k.shape[1], state_pool,
                                         (cache_k, cache_v, cache_logf), wts, "s")
    return (y_p, y_s, k_p, v_p, lf_p, pool_p, k_s, v_s, lf_s, pool_s)
```

```python
import functools

import jax
import jax.numpy as jnp
from jax import lax
from jax.experimental import pallas as pl
from jax.experimental.pallas import tpu as pltpu

F32 = jnp.float32
BF16 = jnp.bfloat16

EPS = 1e-6
N_HEADS = 16
HEAD_DIM = 128
W_B = N_HEADS * HEAD_DIM
POOL_WINDOWS = (2, 4, 8, 16)
POOL_PAD = max(POOL_WINDOWS) - 1
HALO = 16
NEG = -1e30
LOG2E = 1.4426950408889634
ONES_ROWS = 16
LANES = 128
MXU_DEPTH = 256
VMEM_LIMIT = 56 * 1024 * 1024
POOL_TILE = MXU_DEPTH
ATTN_TILE = 512
ATTN_HEADS_PER_STEP = 2
DEC_CHUNK = 1024


def _params(*sem):
    return pltpu.CompilerParams(dimension_semantics=sem, vmem_limit_bytes=VMEM_LIMIT)


def _silu(g):
    return g * (1.0 / (1.0 + jnp.exp(-g)))


def _rms_rows(xf, g):
    r = lax.rsqrt(jnp.mean(xf * xf, axis=-1, keepdims=True) + EPS)
    return (xf * r) * g


def _norm_proj_kernel(x_ref, g_ref, w_ref, *refs, n_scaled, scale, n_side):
    side_in, o_ref, side_out, h_ref = refs[:n_side], refs[n_side], refs[n_side + 1:-1], refs[-1]
    j = pl.program_id(1)

    @pl.when(j == 0)
    def _():
        h_ref[...] = _rms_rows(x_ref[...], g_ref[...]).astype(BF16)

    acc = jnp.dot(h_ref[...], w_ref[...], preferred_element_type=F32)
    if n_scaled:
        acc = acc * jnp.where(j < n_scaled, scale, 1.0)
    o_ref[...] = acc.astype(o_ref.dtype)
    for s_ref, c_ref in zip(side_in, side_out):
        c_ref[...] = s_ref[...].astype(BF16)


def _norm_proj(x, g, w, layer, *, tm, tn, n_scaled=0, scale=1.0, side=(), name):
    M, D = x.shape
    N = w.shape[2]
    n_i, n_j = M // tm, N // tn
    steps = n_i * n_j
    side_in, side_out, side_shapes = [], [], []
    for s, row0 in side:
        rows = (s.shape[0] - row0) // steps
        assert rows % 16 == 0 and rows * steps == s.shape[0] - row0 and row0 % rows == 0
        side_in.append(pl.BlockSpec((rows, s.shape[1]), lambda i, j, b0=row0 // rows: (b0 + i * n_j + j, 0)))
        side_out.append(pl.BlockSpec((rows, s.shape[1]), lambda i, j: (i * n_j + j, 0)))
        side_shapes.append(jax.ShapeDtypeStruct((s.shape[0] - row0, s.shape[1]), BF16))
    out = pl.pallas_call(
        functools.partial(_norm_proj_kernel, n_scaled=n_scaled, scale=scale, n_side=len(side)),
        out_shape=[jax.ShapeDtypeStruct((M, N), BF16)] + side_shapes,
        grid=(n_i, n_j),
        in_specs=[
            pl.BlockSpec((tm, D), lambda i, j: (i, 0)),
            pl.BlockSpec((1, D), lambda i, j: (0, 0)),
            pl.BlockSpec((None, D, tn), lambda i, j: (layer, 0, j)),
        ] + side_in,
        out_specs=[pl.BlockSpec((tm, tn), lambda i, j: (i, j))] + side_out,
        scratch_shapes=[pltpu.VMEM((tm, D), BF16)],
        compiler_params=_params("parallel", "arbitrary"),
        name=name,
    )(x, g.reshape(1, D), w, *[s for s, _ in side])
    return out[0], out[1:]


def _proj_res_kernel(a_ref, w_ref, x_ref, o_ref):
    o_ref[...] = x_ref[...] + jnp.dot(a_ref[...], w_ref[...], preferred_element_type=F32)


def _proj_res(a, w, layer, x, *, tm, tn, name):
    M, K = a.shape
    N = w.shape[2]
    return pl.pallas_call(
        _proj_res_kernel,
        out_shape=jax.ShapeDtypeStruct((M, N), F32),
        grid=(N // tn, M // tm),
        in_specs=[
            pl.BlockSpec((tm, K), lambda j, i: (i, 0)),
            pl.BlockSpec((None, K, tn), lambda j, i: (layer, 0, j)),
            pl.BlockSpec((tm, tn), lambda j, i: (i, j)),
        ],
        out_specs=pl.BlockSpec((tm, tn), lambda j, i: (i, j)),
        compiler_params=_params("parallel", "parallel"),
        name=name,
    )(a, w, x)


def _proj_res_norm_kernel(a_ref, w_ref, x_ref, g_ref, o_ref):
    x = x_ref[...] + jnp.dot(a_ref[...], w_ref[...], preferred_element_type=F32)
    o_ref[...] = _rms_rows(x, g_ref[...])


def _proj_res_norm(a, w, layer, x, g, *, tm, name):
    M, K = a.shape
    N = w.shape[2]
    return pl.pallas_call(
        _proj_res_norm_kernel,
        out_shape=jax.ShapeDtypeStruct((M, N), F32),
        grid=(M // tm,),
        in_specs=[
            pl.BlockSpec((tm, K), lambda i: (i, 0)),
            pl.BlockSpec((None, K, N), lambda i: (layer, 0, 0)),
            pl.BlockSpec((tm, N), lambda i: (i, 0)),
            pl.BlockSpec((1, N), lambda i: (0, 0)),
        ],
        out_specs=pl.BlockSpec((tm, N), lambda i: (i, 0)),
        compiler_params=_params("parallel"),
        name=name,
    )(a, w, x, g.reshape(1, N))


def _window_sum(full, w):
    s = full
    k = 1
    while k < w:
        s = s + pltpu.roll(s, k, 0)
        k *= 2
    return s


def _pool_finish(s, uf, inv_cnt, gate, wg, sc):
    d = s * inv_cnt - uf
    y = jnp.dot(d.astype(BF16), wg, preferred_element_type=F32) * sc
    return (y * _silu(gate)).astype(BF16)


def _pool_prompt_kernel(u_ref, halo_ref, gate_ref, band_ref, wg_ref, sc_ref, o_ref, *, tm, tiles_per_seq, pos0):
    G = wg_ref.shape[-1]
    ti = pl.program_id(0) % tiles_per_seq
    t = pos0 + ti * tm + lax.broadcasted_iota(jnp.int32, (HALO, 1), 0)
    for g, w in enumerate(POOL_WINDOWS):
        c = slice(g * G, (g + 1) * G)
        u = u_ref[:, c]
        d_main = jnp.dot(band_ref[g], u, preferred_element_type=F32).astype(BF16)
        u0 = u[:HALO].astype(F32)
        h0 = jnp.where(ti == 0, 0.0, halo_ref[:, c].astype(F32))
        s0 = _window_sum(jnp.concatenate([h0, u0], axis=0), w)[HALO:, :]
        d0 = s0 * (1.0 / jnp.minimum(t + 1, w).astype(F32)) - u0
        d = jnp.concatenate([d0.astype(BF16), d_main[HALO:]], axis=0)
        y = jnp.dot(d, wg_ref[g], preferred_element_type=F32) * sc_ref[:, c]
        o_ref[:, c] = (y * _silu(gate_ref[:, c].astype(F32))).astype(BF16)


def _pool_bands(tm):
    r = jnp.arange(tm)[:, None] - jnp.arange(tm)[None, :]
    return jnp.stack([jnp.where(r == 0, 1.0 / w - 1.0, jnp.where((r > 0) & (r < w), 1.0 / w, 0.0))
                      for w in POOL_WINDOWS]).astype(BF16)


def _pool_prompt(z, wg, layer, sc, *, T, tm, name):
    M = z.shape[0]
    E = z.shape[1] // 2
    hb = tm // HALO
    bands = _pool_bands(tm)
    return pl.pallas_call(
        functools.partial(_pool_prompt_kernel, tm=tm, tiles_per_seq=T // tm, pos0=0),
        out_shape=jax.ShapeDtypeStruct((M, E), BF16),
        grid=(M // tm,),
        in_specs=[
            pl.BlockSpec((tm, E), lambda i: (i, 0)),
            pl.BlockSpec((HALO, E), lambda i: (jnp.maximum(i * hb - 1, 0), 0)),
            pl.BlockSpec((tm, E), lambda i: (i, 1)),
            pl.BlockSpec(bands.shape, lambda i: (0, 0, 0)),
            pl.BlockSpec((None,) + wg.shape[1:], lambda i: (layer, 0, 0, 0)),
            pl.BlockSpec((1, E), lambda i: (0, 0)),
        ],
        out_specs=pl.BlockSpec((tm, E), lambda i: (i, 0)),
        compiler_params=_params("parallel"),
        name=name,
    )(z, z, z, bands, wg, sc.reshape(1, E))


def _pool_sample_kernel(full_ref, gate_ref, wg_ref, sc_ref, o_ref, *, nseq, T, pos0):
    G = wg_ref.shape[-1]
    seg = HALO + T
    t = pos0 + lax.broadcasted_iota(jnp.int32, (nseq * T, 1), 0) % T

    def tail(a):
        return a.reshape(nseq, seg, G)[:, HALO:, :].reshape(nseq * T, G)

    for g, w in enumerate(POOL_WINDOWS):
        c = slice(g * G, (g + 1) * G)
        full = full_ref[:, c]
        inv_cnt = 1.0 / jnp.minimum(t + 1, w).astype(F32)
        o_ref[:, c] = _pool_finish(tail(_window_sum(full, w)), tail(full), inv_cnt,
                                   gate_ref[:, c].astype(F32), wg_ref[g], sc_ref[:, c])


def _pool_sample(full, z, wg, layer, sc, *, nseq, T, pos0, name):
    E = full.shape[1]
    return pl.pallas_call(
        functools.partial(_pool_sample_kernel, nseq=nseq, T=T, pos0=pos0),
        out_shape=jax.ShapeDtypeStruct((nseq * T, E), BF16),
        grid=(1,),
        in_specs=[
            pl.BlockSpec(full.shape, lambda i: (0, 0)),
            pl.BlockSpec((nseq * T, E), lambda i: (0, 1)),
            pl.BlockSpec((None,) + wg.shape[1:], lambda i: (layer, 0, 0, 0)),
            pl.BlockSpec((1, E), lambda i: (0, 0)),
        ],
        out_specs=pl.BlockSpec((nseq * T, E), lambda i: (0, 0)),
        compiler_params=_params("arbitrary"),
        name=name,
    )(full, z, wg, sc.reshape(1, E))


def _kv_kernel(x_ref, g_ref, wk_ref, wv_ref, wf_ref, bf_ref, k32_ref, v32_ref, k16_ref, v16_ref, lf_ref, *,
               tn, v_transposed):
    tm = x_ref.shape[0]
    nh = tn // HEAD_DIM
    h = _rms_rows(x_ref[...], g_ref[...]).astype(BF16)
    for w_ref, o32_ref, o16_ref in ((wk_ref, k32_ref, k16_ref), (wv_ref, v32_ref, v16_ref)):
        for c in range(w_ref.shape[1] // tn):
            cs = slice(c * tn, (c + 1) * tn)
            acc = jnp.dot(h, w_ref[:, cs], preferred_element_type=F32)
            if v_transposed and o16_ref is v16_ref:
                for hd in range(nh):
                    o16_ref[0, c * nh + hd, :HEAD_DIM, :] = acc[:, hd * HEAD_DIM:(hd + 1) * HEAD_DIM].T.astype(BF16)
                    o16_ref[0, c * nh + hd, HEAD_DIM:, :] = jnp.ones((ONES_ROWS, tm), BF16)
            else:
                o16_ref[:, cs] = acc.astype(BF16)
            o32_ref[:, c * nh:(c + 1) * nh, :] = acc.reshape(tm, nh, HEAD_DIM)
    a = jnp.dot(h, wf_ref[...], preferred_element_type=F32) + bf_ref[...]
    lf_ref[...] = jnp.minimum(a, 0.0) - jnp.log1p(jnp.exp(-jnp.abs(a)))


def _kv_proj(x, g, wkv, wf, bf, *, tm, v_transposed, name):
    M, D = x.shape
    N = W_B
    L = wf.shape[1]
    resident = dict(pipeline_mode=pl.Buffered(1))
    row = lambda i: (i, 0)
    fixed = lambda i: (0, 0)
    if v_transposed:
        v16_shape = jax.ShapeDtypeStruct((M // tm, N_HEADS, HEAD_DIM + ONES_ROWS, tm), BF16)
        v16_spec = pl.BlockSpec((1, N_HEADS, HEAD_DIM + ONES_ROWS, tm), lambda i: (i, 0, 0, 0))
    else:
        v16_shape, v16_spec = jax.ShapeDtypeStruct((M, N), BF16), pl.BlockSpec((tm, N), row)
    return pl.pallas_call(
        functools.partial(_kv_kernel, tn=1024, v_transposed=v_transposed),
        out_shape=(jax.ShapeDtypeStruct((M, N_HEADS, HEAD_DIM), F32),
                   jax.ShapeDtypeStruct((M, N_HEADS, HEAD_DIM), F32),
                   jax.ShapeDtypeStruct((M, N), BF16), v16_shape,
                   jax.ShapeDtypeStruct((M, L), F32)),
        grid=(M // tm,),
        in_specs=[
            pl.BlockSpec((tm, D), row),
            pl.BlockSpec((1, D), fixed),
            pl.BlockSpec((D, N), fixed, **resident),
            pl.BlockSpec((D, N), lambda i: (0, 1), **resident),
            pl.BlockSpec((D, L), fixed, **resident),
            pl.BlockSpec((1, L), fixed),
        ],
        out_specs=(pl.BlockSpec((tm, N_HEADS, HEAD_DIM), lambda i: (i, 0, 0)),
                   pl.BlockSpec((tm, N_HEADS, HEAD_DIM), lambda i: (i, 0, 0)),
                   pl.BlockSpec((tm, N), row), v16_spec, pl.BlockSpec((tm, L), row)),
        compiler_params=_params("parallel"),
        name=name,
    )(x, g.reshape(1, D), wkv, wkv, wf, bf.reshape(1, L))


def _forget_cols_kernel(x_ref, o_ref):
    x = x_ref[0]
    n = x.shape[0]
    row = lax.broadcasted_iota(jnp.int32, x.shape, 0)
    lane = lax.broadcasted_iota(jnp.int32, x.shape, 1)
    k = 1
    while k < n:
        x = x + jnp.where(row >= k, pltpu.roll(x, k, 0), 0.0)
        k *= 2
    x = x * LOG2E
    r1 = x - x.astype(BF16).astype(F32)
    r2 = r1 - r1.astype(BF16).astype(F32)
    piece = jnp.where(lane < N_HEADS, x, jnp.where(lane < 2 * N_HEADS, r1, r2))
    o_ref[0] = jnp.where(lane < 3 * N_HEADS, piece, 0.0).astype(BF16)


def _forget_cols(logf_rep, *, name):
    B, n, L = logf_rep.shape
    return pl.pallas_call(
        _forget_cols_kernel,
        out_shape=jax.ShapeDtypeStruct((B, n, L), BF16),
        grid=(B,),
        in_specs=[pl.BlockSpec((1, n, L), lambda b: (b, 0, 0))],
        out_specs=pl.BlockSpec((1, n, L), lambda b: (b, 0, 0)),
        compiler_params=_params("parallel"),
        name=name,
    )(logf_rep)


def _forget_query_cols(h, rows):
    lane = lax.broadcasted_iota(jnp.int32, (rows, LANES), 1)
    return jnp.where((lane % N_HEADS == h) & (lane < 3 * N_HEADS), -1.0, 0.0).astype(BF16)


N_SCORE_BUFS = 3


def _block_pairs(n, n_trips):
    low = [(i, j) for i in range(n) for j in range(i)]
    diag = [(i, i) for i in range(n)]
    lows, diags = len(low) // n_trips, n // n_trips
    assert lows * n_trips == len(low) and diags * n_trips == n and (lows + diags) % N_SCORE_BUFS == 0
    pairs = []
    for t in range(n_trips):
        pairs += low[t * lows:(t + 1) * lows] + diag[t * diags:(t + 1) * diags]
    for i in range(n):
        assert all(pairs.index((i, j)) < pairs.index((i, i)) for j in range(i))
    pairs.append(pairs[-1])
    return lows, diags, jnp.asarray([p[0] for p in pairs], jnp.int32), jnp.asarray([p[1] for p in pairs], jnp.int32)


def _fox_stream_kernel(qt_ref, kb_ref, q_ref, k_ref, v_ref, fa_ref, gate_ref, o_ref, m_sc, acc_sc, s_sc, *,
                       tq, tk, hps, lows, diags, n_trips):
    heads = range(hps)
    cols = [slice(hh * HEAD_DIM, (hh + 1) * HEAD_DIM) for hh in heads]
    m_sc[...] = jnp.full(m_sc.shape, NEG, F32)
    acc_sc[...] = jnp.zeros(acc_sc.shape, F32)
    fq = [_forget_query_cols(pl.program_id(1) * hps + hh, tq) for hh in heads]

    nt = (((1,), (1,)), ((), ()))
    half = tk // 2

    def scores(p, s_ref, diagonal):
        qoff = pl.multiple_of(qt_ref[p] * tq, tq)
        koff = pl.multiple_of(kb_ref[p] * tk, tk)
        fa = fa_ref[0, pl.ds(koff, tk), :]
        for hh in heads:
            qa = jnp.concatenate([q_ref[0, pl.ds(qoff, tq), cols[hh]], fq[hh]], axis=1)
            ka = jnp.concatenate([k_ref[0, pl.ds(koff, tk), cols[hh]], fa], axis=1)
            if diagonal:
                s_ref[hh, :half, :] = lax.dot_general(ka[:half], qa, nt, preferred_element_type=F32)
                s_ref[hh, half:, half:] = lax.dot_general(ka[half:], qa[half:], nt, preferred_element_type=F32)
            else:
                s_ref[hh] = lax.dot_general(ka, qa, nt, preferred_element_type=F32)

    def absorb(p, s_ref, diagonal):
        i = qt_ref[p]
        for hh in heads:
            s = s_ref[hh]
            if diagonal:
                s = jnp.where(lax.broadcasted_iota(jnp.int32, (tk, tq), 0)
                              <= lax.broadcasted_iota(jnp.int32, (tk, tq), 1), s, NEG)
            m_prev = m_sc[hh, i]
            m_new = jnp.maximum(m_prev, jnp.max(s, axis=0, keepdims=True))
            alpha = jnp.exp2(m_prev - m_new)
            pr = jnp.exp2(s - m_new).astype(BF16)
            vt = v_ref[0, kb_ref[p], hh]
            if diagonal:
                pv = jnp.concatenate([jnp.dot(vt[:, :half], pr[:half, :half], preferred_element_type=F32),
                                      jnp.dot(vt, pr[:, half:], preferred_element_type=F32)], axis=1)
            else:
                pv = jnp.dot(vt, pr, preferred_element_type=F32)
            acc = alpha * acc_sc[hh, i] + pv
            m_sc[hh, i] = m_new
            if diagonal:
                qoff = pl.multiple_of(i * tq, tq)
                o = (acc[:HEAD_DIM] * (1.0 / acc[HEAD_DIM:HEAD_DIM + 1])).T
                gate = gate_ref[0, pl.ds(qoff, tq), cols[hh]].astype(F32)
                o_ref[0, pl.ds(qoff, tq), cols[hh]] = (o * _silu(gate)).astype(BF16)
            else:
                acc_sc[hh, i] = acc

    per_trip = lows + diags

    def body(t, c):
        p = t * per_trip
        for u in range(per_trip):
            scores(p + u + 1, s_sc.at[(u + 1) % N_SCORE_BUFS], (u + 1) % per_trip >= lows)
            absorb(p + u, s_sc.at[u % N_SCORE_BUFS], u >= lows)
        return c

    scores(0, s_sc.at[0], lows == 0)
    lax.fori_loop(0, n_trips, body, 0)


def _fox_stream(zb, k16, v16, fa, *, B, T, tq, hps, name):
    assert T % tq == 0 and N_HEADS % hps == 0
    nq = T // tq
    n_trips = nq // 4
    lows, diags, qt, kb = _block_pairs(nq, n_trips)
    w = hps * HEAD_DIM
    seq = lambda b, h, *_: (b, 0, h)
    return pl.pallas_call(
        functools.partial(_fox_stream_kernel, tq=tq, tk=tq, hps=hps, lows=lows, diags=diags, n_trips=n_trips),
        out_shape=jax.ShapeDtypeStruct((B, T, W_B), BF16),
        grid_spec=pltpu.PrefetchScalarGridSpec(
            num_scalar_prefetch=2,
            grid=(B, N_HEADS // hps),
            in_specs=[
                pl.BlockSpec((1, T, w), seq),
                pl.BlockSpec((1, T, w), seq),
                pl.BlockSpec((1, nq, hps, HEAD_DIM + ONES_ROWS, tq), lambda b, h, *_: (b, 0, h, 0, 0)),
                pl.BlockSpec((1, T, LANES), lambda b, h, *_: (b, 0, 0)),
                pl.BlockSpec((1, T, w), lambda b, h, *_: (b, 0, N_HEADS // hps + h)),
            ],
            out_specs=pl.BlockSpec((1, T, w), seq),
            scratch_shapes=[
                pltpu.VMEM((hps, nq, 1, tq), F32),
                pltpu.VMEM((hps, nq, HEAD_DIM + ONES_ROWS, tq), F32),
                pltpu.VMEM((N_SCORE_BUFS, hps, tq, tq), F32),
            ],
        ),
        compiler_params=_params("parallel", "parallel"),
        name=name,
    )(qt, kb, zb, k16, v16, fa, zb)


DEC_STRIDE = 4
DEC_GROUP = N_HEADS // DEC_STRIDE


def _fox_decode_kernel(z_ref, kc_ref, vc_ref, kn_ref, vn_ref, fa_ref, far_ref, o_ref, m_sc, l_sc, acc_sc, *,
                       P, T, pc):
    c = pl.program_id(1)
    nt = (((1,), (1,)), ((), ()))
    heads = [(g, i, g + DEC_STRIDE * i) for g in range(DEC_STRIDE) for i in range(DEC_GROUP)]

    @pl.when(c == 0)
    def _():
        m_sc[...] = jnp.full(m_sc.shape, NEG, F32)
        l_sc[...] = jnp.zeros(l_sc.shape, F32)
        acc_sc[...] = jnp.zeros(acc_sc.shape, F32)

    def qa_of(h):
        return jnp.concatenate([z_ref[:, h * HEAD_DIM:(h + 1) * HEAD_DIM], _forget_query_cols(h, T)], axis=1)

    def update(s, pv_of):
        m_prev = m_sc[...]
        m_new = jnp.maximum(m_prev, jnp.max(s, axis=-1, keepdims=True))
        alpha = jnp.exp2(m_prev - m_new)
        p = jnp.exp2(s - m_new)
        l_sc[...] = alpha * l_sc[...] + jnp.sum(p, axis=-1, keepdims=True)
        acc_sc[...] = alpha * acc_sc[...] + pv_of(p.astype(BF16))
        m_sc[...] = m_new

    n = DEC_GROUP * pc
    gq = DEC_GROUP * T
    far = far_ref[0, pl.ds(pl.multiple_of(c * n, n), n), :]
    s = jnp.concatenate(
        [lax.dot_general(jnp.concatenate([qa_of(g + DEC_STRIDE * i) for i in range(DEC_GROUP)], axis=0),
                         jnp.concatenate([kc_ref[0, pl.ds(g, n, stride=DEC_STRIDE), :].astype(BF16), far], axis=1),
                         nt, preferred_element_type=F32) for g in range(DEC_STRIDE)], axis=0)
    row_i = (lax.broadcasted_iota(jnp.int32, s.shape, 0) // T) % DEC_GROUP
    col_i = lax.broadcasted_iota(jnp.int32, s.shape, 1) % DEC_GROUP
    update(jnp.where(row_i == col_i, s, NEG),
           lambda pb: jnp.concatenate(
               [jnp.dot(pb[g * gq:(g + 1) * gq, :], vc_ref[0, pl.ds(g, n, stride=DEC_STRIDE), :].astype(BF16),
                        preferred_element_type=F32) for g in range(DEC_STRIDE)], axis=0))

    @pl.when(c == pl.num_programs(1) - 1)
    def _():
        rows = lax.broadcasted_iota(jnp.int32, (N_HEADS * T, T), 0) % T
        cols = lax.broadcasted_iota(jnp.int32, (N_HEADS * T, T), 1)
        fa_n = fa_ref[0, P:, :]
        s_n = jnp.concatenate(
            [lax.dot_general(qa_of(h), jnp.concatenate([kn_ref[:, h * HEAD_DIM:(h + 1) * HEAD_DIM], fa_n], axis=1),
                             nt, preferred_element_type=F32) for _, _, h in heads], axis=0)
        update(jnp.where(cols <= rows, s_n, NEG),
               lambda pb: jnp.concatenate(
                   [jnp.dot(pb[r * T:(r + 1) * T, :], vn_ref[:, h * HEAD_DIM:(h + 1) * HEAD_DIM],
                            preferred_element_type=F32) for r, (_, _, h) in enumerate(heads)], axis=0))
        o = acc_sc[...] * (1.0 / l_sc[...])
        for r, (_, _, h) in enumerate(heads):
            hc = slice(h * HEAD_DIM, (h + 1) * HEAD_DIM)
            gate = z_ref[:, W_B + h * HEAD_DIM:W_B + (h + 1) * HEAD_DIM].astype(F32)
            o_ref[:, hc] = (o[r * T:(r + 1) * T, :] * _silu(gate)).astype(BF16)


def _fox_decode(zb, k16, v16, cache_k, cache_v, fa, fa_rep, *, B, T, P, pc, name):
    c_spec = pl.BlockSpec((1, pc * N_HEADS, HEAD_DIM), lambda b, c: (b, c, 0))
    n_spec = pl.BlockSpec((T, W_B), lambda b, c: (b, 0))
    return pl.pallas_call(
        functools.partial(_fox_decode_kernel, P=P, T=T, pc=pc),
        out_shape=jax.ShapeDtypeStruct((B * T, W_B), BF16),
        grid=(B, P // pc),
        in_specs=[
            pl.BlockSpec((T, 2 * W_B), lambda b, c: (b, 0)),
            c_spec, c_spec, n_spec, n_spec,
            pl.BlockSpec((1, P + T, LANES), lambda b, c: (b, 0, 0)),
            pl.BlockSpec((1, P * DEC_GROUP, LANES), lambda b, c: (b, 0, 0)),
        ],
        out_specs=n_spec,
        scratch_shapes=[
            pltpu.VMEM((N_HEADS * T, 1), F32),
            pltpu.VMEM((N_HEADS * T, 1), F32),
            pltpu.VMEM((N_HEADS * T, HEAD_DIM), F32),
        ],
        compiler_params=_params("parallel", "arbitrary"),
        name=name,
    )(zb, cache_k, cache_v, k16, v16, fa, fa_rep)


def _trunk(x3, pos0, pool_prev, past, wts, tag):
    B, T, D = x3.shape
    M = B * T
    x = x3.reshape(M, D)
    prompt = pool_prev is None
    tm = 1024 if prompt else M
    tm_res = 1024 if prompt else M
    tm_kv = 512 if prompt else M
    tn_in = 2048
    tn_out = 1024 if prompt else 2048
    E = wts["w_in_a0"].shape[2] // 2
    q_scale = HEAD_DIM ** -0.5 * LOG2E

    new_pool = []
    for l in range(wts["norm_a"].shape[0]):
        jobs = wts["to_cast"].pop(l, ()) if prompt else ()
        z, casts = _norm_proj(x, wts["norm_a"][l], wts[f"w_in_a{l}"], 0, tm=tm, tn=tn_in, name=f"in_a{l}_{tag}",
                              side=[(w.reshape(-1, w.shape[-1]), row0) for _, w, row0, _ in jobs])
        for (key, _, _, shape), c in zip(jobs, casts):
            wts[key] = c.reshape(shape)
        u3 = z[:, :E].reshape(B, T, E) if not prompt else None
        if prompt:
            t = _pool_prompt(z, wts["w_grp_a"], l, wts["scale_a"][l], T=T, tm=POOL_TILE, name=f"pool{l}_{tag}")
            new_pool.append(z.reshape(B, T, 2 * E)[:, T - POOL_PAD:, :E].astype(F32))
        else:
            hist = jnp.pad(pool_prev[l].astype(F32), ((0, 0), (HALO - POOL_PAD, 0), (0, 0)))
            full = jnp.concatenate([hist, u3.astype(F32)], axis=1)
            t = _pool_sample(full.reshape(B * (HALO + T), E), z, wts["w_grp_a"], l, wts["scale_a"][l],
                             nseq=B, T=T, pos0=pos0, name=f"pool{l}_{tag}")
            new_pool.append(full[:, HALO + T - POOL_PAD:, :])
        x = _proj_res(t, wts["w_out_a"], l, x, tm=tm_res, tn=tn_out, name=f"out_a{l}_{tag}")

    k32, v32, k16, v16, logf_rep = _kv_proj(x, wts["norm_kv"], wts["w_kv"], wts["w_f_rep"],
                                            wts["b_f_rep"], tm=tm_kv, v_transposed=prompt, name=f"kv_{tag}")
    logf3 = logf_rep[:, :N_HEADS].reshape(B, T, N_HEADS)
    logf_rep = logf_rep.reshape(B, T, LANES)
    if not prompt:
        past_k, past_v, past_logf = past
        P = past_k.shape[1]
        past_rep = jnp.pad(jnp.tile(past_logf.astype(F32), (1, 1, 3)), ((0, 0), (0, 0), (0, LANES - 3 * N_HEADS)))
        logf_rep = jnp.concatenate([past_rep, logf_rep], axis=1)
        ck = past_k.reshape(B, P * N_HEADS, HEAD_DIM)
        cv = past_v.reshape(B, P * N_HEADS, HEAD_DIM)
    fa = _forget_cols(logf_rep, name=f"fcols_{tag}")
    if not prompt:
        fa_rep = jnp.repeat(fa[:, :P], DEC_GROUP, axis=1)

    n_b = wts["w_in_b"].shape[0]
    for l in range(n_b):
        zb, _ = _norm_proj(x, wts["norm_b"][l], wts["w_in_b"], l, tm=tm, tn=tn_in,
                           n_scaled=W_B // tn_in, scale=q_scale, name=f"in_b{l}_{tag}")
        if prompt:
            og = _fox_stream(zb.reshape(B, T, 2 * W_B), k16.reshape(B, T, W_B),
                             v16.reshape(B, T // ATTN_TILE, N_HEADS, HEAD_DIM + ONES_ROWS, ATTN_TILE), fa,
                             B=B, T=T, tq=ATTN_TILE, hps=ATTN_HEADS_PER_STEP,
                             name=f"attn{l}_{tag}").reshape(M, W_B)
        else:
            og = _fox_decode(zb, k16, v16, ck, cv, fa, fa_rep, B=B, T=T, P=P, pc=DEC_CHUNK,
                             name=f"attn{l}_{tag}")
        if l + 1 < n_b:
            x = _proj_res(og, wts["w_out_b"], l, x, tm=tm_res, tn=tn_out, name=f"out_b{l}_{tag}")
        else:
            y = _proj_res_norm(og, wts["w_out_b"], l, x, wts["norm_f"], tm=min(tm_res, 512),
                               name=f"out_b{l}_{tag}")
    return (y.reshape(B, T, D), k32.reshape(B, T, N_HEADS, HEAD_DIM), v32.reshape(B, T, N_HEADS, HEAD_DIM),
            logf3, jnp.stack(new_pool))


def kernel(x_prompt, x_sample, cache_k, cache_v, cache_logf, state_pool, norm_a, w_in_a, w_grp_a, scale_a,
           w_out_a, norm_kv, w_kv, b_f, norm_b, w_in_b, w_out_b, norm_f):
    assert w_in_a.shape[0] == 2
    wts = dict(
        norm_a=norm_a, w_in_a0=w_in_a[:1].astype(BF16), scale_a=scale_a, norm_kv=norm_kv,
        w_kv=w_kv.astype(BF16),
        w_f_rep=jnp.pad(jnp.tile(w_kv[:, 2 * W_B:], (1, 3)), ((0, 0), (0, LANES - 3 * N_HEADS))).astype(BF16),
        b_f_rep=jnp.pad(jnp.tile(b_f, 3), (0, LANES - 3 * N_HEADS)),
        norm_b=norm_b, norm_f=norm_f,
        to_cast={0: (("w_in_a1", w_in_a, w_in_a.shape[1], (1,) + w_in_a.shape[1:]),
                     ("w_out_a", w_out_a, 0, w_out_a.shape), ("w_grp_a", w_grp_a, 0, w_grp_a.shape)),
                 1: (("w_in_b", w_in_b, 0, w_in_b.shape), ("w_out_b", w_out_b, 0, w_out_b.shape))},
    )
    y_p, k_p, v_p, lf_p, pool_p = _trunk(x_prompt, 0, None, None, wts, "p")
    y_s, k_s, v_s, lf_s, pool_s = _trunk(x_sample, cache_k.shape[1], state_pool,
                                         (cache_k, cache_v, cache_logf), wts, "s")
    return (y_p, y_s, k_p, v_p, lf_p, pool_p, k_s, v_s, lf_s, pool_s)
```

```python
import functools

import jax
import jax.numpy as jnp
from jax import lax
from jax.experimental import pallas as pl
from jax.experimental.pallas import tpu as pltpu

F32 = jnp.float32
BF16 = jnp.bfloat16

EPS = 1e-6
N_HEADS = 16
HEAD_DIM = 128
W_B = N_HEADS * HEAD_DIM
POOL_WINDOWS = (2, 4, 8, 16)
POOL_PAD = max(POOL_WINDOWS) - 1
HALO = 16
NEG = -1e30
LOG2E = 1.4426950408889634
ONES_ROWS = 16
LANES = 128
MXU_DEPTH = 256
VMEM_LIMIT = 56 * 1024 * 1024
POOL_TILE = MXU_DEPTH
ATTN_TILE = 512
ATTN_HEADS_PER_STEP = 2
DEC_CHUNK = 1024


def _params(*sem):
    return pltpu.CompilerParams(dimension_semantics=sem, vmem_limit_bytes=VMEM_LIMIT)


def _silu(g):
    return g * (1.0 / (1.0 + jnp.exp(-g)))


def _rms_rows(xf, g):
    r = lax.rsqrt(jnp.mean(xf * xf, axis=-1, keepdims=True) + EPS)
    return (xf * r) * g


def _norm_proj_kernel(x_ref, g_ref, w_ref, *refs, n_scaled, scale, n_side, qt_tile):
    side_in, o_ref, side_out, h_ref = refs[:n_side], refs[n_side], refs[n_side + 1:n_side + 1 + n_side], refs[-1]
    j = pl.program_id(1)

    @pl.when(j == 0)
    def _():
        h_ref[...] = _rms_rows(x_ref[...], g_ref[...]).astype(BF16)

    acc = jnp.dot(h_ref[...], w_ref[...], preferred_element_type=F32)
    if n_scaled:
        acc = acc * jnp.where(j < n_scaled, scale, 1.0)
    ob = acc.astype(o_ref.dtype)
    o_ref[...] = ob
    if qt_tile:
        qt_ref = refs[-2]

        @pl.when(j == 0)
        def _():
            for t in range(ob.shape[0] // qt_tile):
                for hd in range(ob.shape[1] // HEAD_DIM):
                    qt_ref[t, hd] = ob[t * qt_tile:(t + 1) * qt_tile, hd * HEAD_DIM:(hd + 1) * HEAD_DIM].T
    for s_ref, c_ref in zip(side_in, side_out):
        c_ref[...] = s_ref[...].astype(BF16)


def _norm_proj(x, g, w, layer, *, tm, tn, n_scaled=0, scale=1.0, side=(), qt_tile=0, name):
    M, D = x.shape
    N = w.shape[2]
    n_i, n_j = M // tm, N // tn
    steps = n_i * n_j
    side_in, side_out, side_shapes = [], [], []
    for s, row0 in side:
        rows = (s.shape[0] - row0) // steps
        assert rows % 16 == 0 and rows * steps == s.shape[0] - row0 and row0 % rows == 0
        side_in.append(pl.BlockSpec((rows, s.shape[1]), lambda i, j, b0=row0 // rows: (b0 + i * n_j + j, 0)))
        side_out.append(pl.BlockSpec((rows, s.shape[1]), lambda i, j: (i * n_j + j, 0)))
        side_shapes.append(jax.ShapeDtypeStruct((s.shape[0] - row0, s.shape[1]), BF16))
    if qt_tile:
        assert tn % HEAD_DIM == 0 and tm % qt_tile == 0
        side_shapes = side_shapes + [jax.ShapeDtypeStruct((M // qt_tile, tn // HEAD_DIM, HEAD_DIM, qt_tile), BF16)]
        side_out = side_out + [pl.BlockSpec((tm // qt_tile, tn // HEAD_DIM, HEAD_DIM, qt_tile),
                                            lambda i, j: (i, 0, 0, 0))]
    out = pl.pallas_call(
        functools.partial(_norm_proj_kernel, n_scaled=n_scaled, scale=scale, n_side=len(side), qt_tile=qt_tile),
        out_shape=[jax.ShapeDtypeStruct((M, N), BF16)] + side_shapes,
        grid=(n_i, n_j),
        in_specs=[
            pl.BlockSpec((tm, D), lambda i, j: (i, 0)),
            pl.BlockSpec((1, D), lambda i, j: (0, 0)),
            pl.BlockSpec((None, D, tn), lambda i, j: (layer, 0, j)),
        ] + side_in,
        out_specs=[pl.BlockSpec((tm, tn), lambda i, j: (i, j))] + side_out,
        scratch_shapes=[pltpu.VMEM((tm, D), BF16)],
        compiler_params=pltpu.CompilerParams(dimension_semantics=("parallel", "arbitrary"),
                                             vmem_limit_bytes=VMEM_LIMIT + (6 << 20 if qt_tile else 0)),
        name=name,
    )(x, g.reshape(1, D), w, *[s for s, _ in side])
    return out[0], out[1:]


def _proj_res_kernel(a_ref, w_ref, x_ref, o_ref):
    o_ref[...] = x_ref[...] + jnp.dot(a_ref[...], w_ref[...], preferred_element_type=F32)


def _proj_res(a, w, layer, x, *, tm, tn, name):
    M, K = a.shape
    N = w.shape[2]
    return pl.pallas_call(
        _proj_res_kernel,
        out_shape=jax.ShapeDtypeStruct((M, N), F32),
        grid=(N // tn, M // tm),
        in_specs=[
            pl.BlockSpec((tm, K), lambda j, i: (i, 0)),
            pl.BlockSpec((None, K, tn), lambda j, i: (layer, 0, j)),
            pl.BlockSpec((tm, tn), lambda j, i: (i, j)),
        ],
        out_specs=pl.BlockSpec((tm, tn), lambda j, i: (i, j)),
        compiler_params=_params("parallel", "parallel"),
        name=name,
    )(a, w, x)


def _proj_res_norm_kernel(a_ref, w_ref, x_ref, g_ref, o_ref):
    x = x_ref[...] + jnp.dot(a_ref[...], w_ref[...], preferred_element_type=F32)
    o_ref[...] = _rms_rows(x, g_ref[...])


def _proj_res_norm(a, w, layer, x, g, *, tm, name):
    M, K = a.shape
    N = w.shape[2]
    return pl.pallas_call(
        _proj_res_norm_kernel,
        out_shape=jax.ShapeDtypeStruct((M, N), F32),
        grid=(M // tm,),
        in_specs=[
            pl.BlockSpec((tm, K), lambda i: (i, 0)),
            pl.BlockSpec((None, K, N), lambda i: (layer, 0, 0)),
            pl.BlockSpec((tm, N), lambda i: (i, 0)),
            pl.BlockSpec((1, N), lambda i: (0, 0)),
        ],
        out_specs=pl.BlockSpec((tm, N), lambda i: (i, 0)),
        compiler_params=_params("parallel"),
        name=name,
    )(a, w, x, g.reshape(1, N))


def _window_sum(full, w):
    s = full
    k = 1
    while k < w:
        s = s + pltpu.roll(s, k, 0)
        k *= 2
    return s


def _pool_finish(s, uf, inv_cnt, gate, wg, sc):
    d = s * inv_cnt - uf
    y = jnp.dot(d.astype(BF16), wg, preferred_element_type=F32) * sc
    return (y * _silu(gate)).astype(BF16)


def _pool_prompt_kernel(u_ref, halo_ref, gate_ref, band_ref, wg_ref, sc_ref, o_ref, *, tm, tiles_per_seq, pos0):
    G = wg_ref.shape[-1]
    ti = pl.program_id(0) % tiles_per_seq
    t = pos0 + ti * tm + lax.broadcasted_iota(jnp.int32, (HALO, 1), 0)
    for g, w in enumerate(POOL_WINDOWS):
        c = slice(g * G, (g + 1) * G)
        u = u_ref[:, c]
        d_main = jnp.dot(band_ref[g], u, preferred_element_type=F32).astype(BF16)
        u0 = u[:HALO].astype(F32)
        h0 = jnp.where(ti == 0, 0.0, halo_ref[:, c].astype(F32))
        s0 = _window_sum(jnp.concatenate([h0, u0], axis=0), w)[HALO:, :]
        d0 = s0 * (1.0 / jnp.minimum(t + 1, w).astype(F32)) - u0
        d = jnp.concatenate([d0.astype(BF16), d_main[HALO:]], axis=0)
        y = jnp.dot(d, wg_ref[g], preferred_element_type=F32) * sc_ref[:, c]
        o_ref[:, c] = (y * _silu(gate_ref[:, c].astype(F32))).astype(BF16)


def _pool_bands(tm):
    r = jnp.arange(tm)[:, None] - jnp.arange(tm)[None, :]
    return jnp.stack([jnp.where(r == 0, 1.0 / w - 1.0, jnp.where((r > 0) & (r < w), 1.0 / w, 0.0))
                      for w in POOL_WINDOWS]).astype(BF16)


def _pool_prompt(z, wg, layer, sc, *, T, tm, name):
    M = z.shape[0]
    E = z.shape[1] // 2
    hb = tm // HALO
    bands = _pool_bands(tm)
    return pl.pallas_call(
        functools.partial(_pool_prompt_kernel, tm=tm, tiles_per_seq=T // tm, pos0=0),
        out_shape=jax.ShapeDtypeStruct((M, E), BF16),
        grid=(M // tm,),
        in_specs=[
            pl.BlockSpec((tm, E), lambda i: (i, 0)),
            pl.BlockSpec((HALO, E), lambda i: (jnp.maximum(i * hb - 1, 0), 0)),
            pl.BlockSpec((tm, E), lambda i: (i, 1)),
            pl.BlockSpec(bands.shape, lambda i: (0, 0, 0)),
            pl.BlockSpec((None,) + wg.shape[1:], lambda i: (layer, 0, 0, 0)),
            pl.BlockSpec((1, E), lambda i: (0, 0)),
        ],
        out_specs=pl.BlockSpec((tm, E), lambda i: (i, 0)),
        compiler_params=_params("parallel"),
        name=name,
    )(z, z, z, bands, wg, sc.reshape(1, E))


def _pool_sample_kernel(full_ref, gate_ref, wg_ref, sc_ref, o_ref, *, nseq, T, pos0):
    G = wg_ref.shape[-1]
    seg = HALO + T
    t = pos0 + lax.broadcasted_iota(jnp.int32, (nseq * T, 1), 0) % T

    def tail(a):
        return a.reshape(nseq, seg, G)[:, HALO:, :].reshape(nseq * T, G)

    for g, w in enumerate(POOL_WINDOWS):
        c = slice(g * G, (g + 1) * G)
        full = full_ref[:, c]
        inv_cnt = 1.0 / jnp.minimum(t + 1, w).astype(F32)
        o_ref[:, c] = _pool_finish(tail(_window_sum(full, w)), tail(full), inv_cnt,
                                   gate_ref[:, c].astype(F32), wg_ref[g], sc_ref[:, c])


def _pool_sample(full, z, wg, layer, sc, *, nseq, T, pos0, name):
    E = full.shape[1]
    return pl.pallas_call(
        functools.partial(_pool_sample_kernel, nseq=nseq, T=T, pos0=pos0),
        out_shape=jax.ShapeDtypeStruct((nseq * T, E), BF16),
        grid=(1,),
        in_specs=[
            pl.BlockSpec(full.shape, lambda i: (0, 0)),
            pl.BlockSpec((nseq * T, E), lambda i: (0, 1)),
            pl.BlockSpec((None,) + wg.shape[1:], lambda i: (layer, 0, 0, 0)),
            pl.BlockSpec((1, E), lambda i: (0, 0)),
        ],
        out_specs=pl.BlockSpec((nseq * T, E), lambda i: (0, 0)),
        compiler_params=_params("arbitrary"),
        name=name,
    )(full, z, wg, sc.reshape(1, E))


def _kv_kernel(x_ref, g_ref, wk_ref, wv_ref, wf_ref, bf_ref, k32_ref, v32_ref, k16_ref, v16_ref, lf_ref, *, tn):
    tm = x_ref.shape[0]
    h = _rms_rows(x_ref[...], g_ref[...]).astype(BF16)
    for w_ref, o32_ref, o16_ref in ((wk_ref, k32_ref, k16_ref), (wv_ref, v32_ref, v16_ref)):
        for c in range(w_ref.shape[1] // tn):
            cs = slice(c * tn, (c + 1) * tn)
            acc = jnp.dot(h, w_ref[:, cs], preferred_element_type=F32)
            o16_ref[:, cs] = acc.astype(BF16)
            nh = tn // HEAD_DIM
            o32_ref[:, c * nh:(c + 1) * nh, :] = acc.reshape(tm, nh, HEAD_DIM)
    a = jnp.dot(h, wf_ref[...], preferred_element_type=F32) + bf_ref[...]
    lf_ref[...] = jnp.minimum(a, 0.0) - jnp.log1p(jnp.exp(-jnp.abs(a)))


def _kv_proj(x, g, wkv, wf, bf, *, tm, name):
    M, D = x.shape
    N = W_B
    L = wf.shape[1]
    resident = dict(pipeline_mode=pl.Buffered(1))
    row = lambda i: (i, 0)
    fixed = lambda i: (0, 0)
    return pl.pallas_call(
        functools.partial(_kv_kernel, tn=1024),
        out_shape=(jax.ShapeDtypeStruct((M, N_HEADS, HEAD_DIM), F32),
                   jax.ShapeDtypeStruct((M, N_HEADS, HEAD_DIM), F32),
                   jax.ShapeDtypeStruct((M, N), BF16), jax.ShapeDtypeStruct((M, N), BF16),
                   jax.ShapeDtypeStruct((M, L), F32)),
        grid=(M // tm,),
        in_specs=[
            pl.BlockSpec((tm, D), row),
            pl.BlockSpec((1, D), fixed),
            pl.BlockSpec((D, N), fixed, **resident),
            pl.BlockSpec((D, N), lambda i: (0, 1), **resident),
            pl.BlockSpec((D, L), fixed, **resident),
            pl.BlockSpec((1, L), fixed),
        ],
        out_specs=(pl.BlockSpec((tm, N_HEADS, HEAD_DIM), lambda i: (i, 0, 0)),
                   pl.BlockSpec((tm, N_HEADS, HEAD_DIM), lambda i: (i, 0, 0)),
                   pl.BlockSpec((tm, N), row), pl.BlockSpec((tm, N), row), pl.BlockSpec((tm, L), row)),
        compiler_params=_params("parallel"),
        name=name,
    )(x, g.reshape(1, D), wkv, wkv, wf, bf.reshape(1, L))


def _forget_cols_kernel(x_ref, o_ref):
    x = x_ref[0]
    n = x.shape[0]
    row = lax.broadcasted_iota(jnp.int32, x.shape, 0)
    lane = lax.broadcasted_iota(jnp.int32, x.shape, 1)
    k = 1
    while k < n:
        x = x + jnp.where(row >= k, pltpu.roll(x, k, 0), 0.0)
        k *= 2
    x = x * LOG2E
    r1 = x - x.astype(BF16).astype(F32)
    r2 = r1 - r1.astype(BF16).astype(F32)
    piece = jnp.where(lane < N_HEADS, x, jnp.where(lane < 2 * N_HEADS, r1, r2))
    o_ref[0] = jnp.where(lane < 3 * N_HEADS, piece, 0.0).astype(BF16)


def _forget_cols(logf_rep, *, name):
    B, n, L = logf_rep.shape
    return pl.pallas_call(
        _forget_cols_kernel,
        out_shape=jax.ShapeDtypeStruct((B, n, L), BF16),
        grid=(B,),
        in_specs=[pl.BlockSpec((1, n, L), lambda b: (b, 0, 0))],
        out_specs=pl.BlockSpec((1, n, L), lambda b: (b, 0, 0)),
        compiler_params=_params("parallel"),
        name=name,
    )(logf_rep)


def _forget_query_cols(h, rows):
    lane = lax.broadcasted_iota(jnp.int32, (rows, LANES), 1)
    return jnp.where((lane % N_HEADS == h) & (lane < 3 * N_HEADS), -1.0, 0.0).astype(BF16)


N_SCORE_BUFS = 3


def _block_pairs(n, n_trips):
    low = [(i, j) for i in range(n) for j in range(i)]
    diag = [(i, i) for i in range(n)]
    lows, diags = len(low) // n_trips, n // n_trips
    assert lows * n_trips == len(low) and diags * n_trips == n and (lows + diags) % N_SCORE_BUFS == 0
    pairs = []
    for t in range(n_trips):
        pairs += low[t * lows:(t + 1) * lows] + diag[t * diags:(t + 1) * diags]
    for i in range(n):
        assert all(pairs.index((i, j)) < pairs.index((i, i)) for j in range(i))
    pairs.append(pairs[-1])
    return lows, diags, jnp.asarray([p[0] for p in pairs], jnp.int32), jnp.asarray([p[1] for p in pairs], jnp.int32)


def _fox_stream_kernel(qt_ref, kb_ref, q_ref, k_ref, v_ref, fa_ref, gate_ref, o_ref, m_sc, acc_sc, s_sc, *,
                       tq, tk, hps, lows, diags, n_trips):
    heads = range(hps)
    cols = [slice(hh * HEAD_DIM, (hh + 1) * HEAD_DIM) for hh in heads]
    m_sc[...] = jnp.full(m_sc.shape, NEG, F32)
    acc_sc[...] = jnp.zeros(acc_sc.shape, F32)
    ones = jnp.ones((ONES_ROWS, tk), BF16)
    rr = lax.broadcasted_iota(jnp.int32, (LANES, tq), 0)
    fq = [jnp.where((rr % N_HEADS == pl.program_id(1) * hps + hh) & (rr < 3 * N_HEADS), -1.0, 0.0).astype(BF16)
          for hh in heads]

    nt = (((1,), (1,)), ((), ()))
    half = tk // 2

    def scores(p, s_ref, diagonal):
        qoff = pl.multiple_of(qt_ref[p] * tq, tq)
        koff = pl.multiple_of(kb_ref[p] * tk, tk)
        fa = fa_ref[0, pl.ds(koff, tk), :]
        for hh in heads:
            qa = jnp.concatenate([q_ref[0, qt_ref[p], hh], fq[hh]], axis=0)
            ka = jnp.concatenate([k_ref[0, pl.ds(koff, tk), cols[hh]], fa], axis=1)
            if diagonal:
                s_ref[hh, :half, :] = jnp.dot(ka[:half], qa, preferred_element_type=F32)
                s_ref[hh, half:, half:] = jnp.dot(ka[half:], qa[:, half:], preferred_element_type=F32)
            else:
                s_ref[hh] = jnp.dot(ka, qa, preferred_element_type=F32)

    def absorb(p, s_ref, diagonal):
        i = qt_ref[p]
        koff = pl.multiple_of(kb_ref[p] * tk, tk)
        for hh in heads:
            s = s_ref[hh]
            if diagonal:
                s = jnp.where(lax.broadcasted_iota(jnp.int32, (tk, tq), 0)
                              <= lax.broadcasted_iota(jnp.int32, (tk, tq), 1), s, NEG)
            m_prev = m_sc[hh, i]
            m_new = jnp.maximum(m_prev, jnp.max(s, axis=0, keepdims=True))
            alpha = jnp.exp2(m_prev - m_new)
            pr = jnp.exp2(s - m_new).astype(BF16)
            vt = jnp.concatenate([v_ref[0, pl.ds(koff, tk), cols[hh]].T, ones], axis=0)
            if diagonal:
                pv = jnp.concatenate([jnp.dot(vt[:, :half], pr[:half, :half], preferred_element_type=F32),
                                      jnp.dot(vt, pr[:, half:], preferred_element_type=F32)], axis=1)
            else:
                pv = jnp.dot(vt, pr, preferred_element_type=F32)
            acc = alpha * acc_sc[hh, i] + pv
            m_sc[hh, i] = m_new
            if diagonal:
                qoff = pl.multiple_of(i * tq, tq)
                o = (acc[:HEAD_DIM] * (1.0 / acc[HEAD_DIM:HEAD_DIM + 1])).T
                gate = gate_ref[0, pl.ds(qoff, tq), cols[hh]].astype(F32)
                o_ref[0, pl.ds(qoff, tq), cols[hh]] = (o * _silu(gate)).astype(BF16)
            else:
                acc_sc[hh, i] = acc

    per_trip = lows + diags

    def body(t, c):
        p = t * per_trip
        for u in range(per_trip):
            scores(p + u + 1, s_sc.at[(u + 1) % N_SCORE_BUFS], (u + 1) % per_trip >= lows)
            absorb(p + u, s_sc.at[u % N_SCORE_BUFS], u >= lows)
        return c

    scores(0, s_sc.at[0], lows == 0)
    lax.fori_loop(0, n_trips, body, 0)


def _fox_stream(q_t, zb, k16, v16, fa, *, B, T, tq, hps, name):
    assert T % tq == 0 and N_HEADS % hps == 0
    nq = T // tq
    n_trips = nq // 4
    lows, diags, qt, kb = _block_pairs(nq, n_trips)
    w = hps * HEAD_DIM
    seq = lambda b, h, *_: (b, 0, h)
    return pl.pallas_call(
        functools.partial(_fox_stream_kernel, tq=tq, tk=tq, hps=hps, lows=lows, diags=diags, n_trips=n_trips),
        out_shape=jax.ShapeDtypeStruct((B, T, W_B), BF16),
        grid_spec=pltpu.PrefetchScalarGridSpec(
            num_scalar_prefetch=2,
            grid=(B, N_HEADS // hps),
            in_specs=[
                pl.BlockSpec((1, nq, hps, HEAD_DIM, tq), lambda b, h, *_: (b, 0, h, 0, 0)),
                pl.BlockSpec((1, T, w), seq),
                pl.BlockSpec((1, T, w), seq),
                pl.BlockSpec((1, T, LANES), lambda b, h, *_: (b, 0, 0)),
                pl.BlockSpec((1, T, w), lambda b, h, *_: (b, 0, N_HEADS // hps + h)),
            ],
            out_specs=pl.BlockSpec((1, T, w), seq),
            scratch_shapes=[
                pltpu.VMEM((hps, nq, 1, tq), F32),
                pltpu.VMEM((hps, nq, HEAD_DIM + ONES_ROWS, tq), F32),
                pltpu.VMEM((N_SCORE_BUFS, hps, tq, tq), F32),
            ],
        ),
        compiler_params=_params("parallel", "parallel"),
        name=name,
    )(qt, kb, q_t, k16, v16, fa, zb)


DEC_STRIDE = 4
DEC_GROUP = N_HEADS // DEC_STRIDE


def _fox_decode_kernel(z_ref, kc_ref, vc_ref, kn_ref, vn_ref, fa_ref, far_ref, o_ref, m_sc, l_sc, acc_sc, *,
                       P, T, pc):
    c = pl.program_id(1)
    nt = (((1,), (1,)), ((), ()))
    heads = [(g, i, g + DEC_STRIDE * i) for g in range(DEC_STRIDE) for i in range(DEC_GROUP)]

    @pl.when(c == 0)
    def _():
        m_sc[...] = jnp.full(m_sc.shape, NEG, F32)
        l_sc[...] = jnp.zeros(l_sc.shape, F32)
        acc_sc[...] = jnp.zeros(acc_sc.shape, F32)

    def qa_of(h):
        return jnp.concatenate([z_ref[:, h * HEAD_DIM:(h + 1) * HEAD_DIM], _forget_query_cols(h, T)], axis=1)

    def update(s, pv_of):
        m_prev = m_sc[...]
        m_new = jnp.maximum(m_prev, jnp.max(s, axis=-1, keepdims=True))
        alpha = jnp.exp2(m_prev - m_new)
        p = jnp.exp2(s - m_new)
        l_sc[...] = alpha * l_sc[...] + jnp.sum(p, axis=-1, keepdims=True)
        acc_sc[...] = alpha * acc_sc[...] + pv_of(p.astype(BF16))
        m_sc[...] = m_new

    n = DEC_GROUP * pc
    gq = DEC_GROUP * T
    far = far_ref[0, pl.ds(pl.multiple_of(c * n, n), n), :]
    s = jnp.concatenate(
        [lax.dot_general(jnp.concatenate([qa_of(g + DEC_STRIDE * i) for i in range(DEC_GROUP)], axis=0),
                         jnp.concatenate([kc_ref[0, pl.ds(g, n, stride=DEC_STRIDE), :].astype(BF16), far], axis=1),
                         nt, preferred_element_type=F32) for g in range(DEC_STRIDE)], axis=0)
    row_i = (lax.broadcasted_iota(jnp.int32, s.shape, 0) // T) % DEC_GROUP
    col_i = lax.broadcasted_iota(jnp.int32, s.shape, 1) % DEC_GROUP
    update(jnp.where(row_i == col_i, s, NEG),
           lambda pb: jnp.concatenate(
               [jnp.dot(pb[g * gq:(g + 1) * gq, :], vc_ref[0, pl.ds(g, n, stride=DEC_STRIDE), :].astype(BF16),
                        preferred_element_type=F32) for g in range(DEC_STRIDE)], axis=0))

    @pl.when(c == pl.num_programs(1) - 1)
    def _():
        rows = lax.broadcasted_iota(jnp.int32, (N_HEADS * T, T), 0) % T
        cols = lax.broadcasted_iota(jnp.int32, (N_HEADS * T, T), 1)
        fa_n = fa_ref[0, P:, :]
        s_n = jnp.concatenate(
            [lax.dot_general(qa_of(h), jnp.concatenate([kn_ref[:, h * HEAD_DIM:(h + 1) * HEAD_DIM], fa_n], axis=1),
                             nt, preferred_element_type=F32) for _, _, h in heads], axis=0)
        update(jnp.where(cols <= rows, s_n, NEG),
               lambda pb: jnp.concatenate(
                   [jnp.dot(pb[r * T:(r + 1) * T, :], vn_ref[:, h * HEAD_DIM:(h + 1) * HEAD_DIM],
                            preferred_element_type=F32) for r, (_, _, h) in enumerate(heads)], axis=0))
        o = acc_sc[...] * (1.0 / l_sc[...])
        for r, (_, _, h) in enumerate(heads):
            hc = slice(h * HEAD_DIM, (h + 1) * HEAD_DIM)
            gate = z_ref[:, W_B + h * HEAD_DIM:W_B + (h + 1) * HEAD_DIM].astype(F32)
            o_ref[:, hc] = (o[r * T:(r + 1) * T, :] * _silu(gate)).astype(BF16)


def _fox_decode(zb, k16, v16, cache_k, cache_v, fa, fa_rep, *, B, T, P, pc, name):
    c_spec = pl.BlockSpec((1, pc * N_HEADS, HEAD_DIM), lambda b, c: (b, c, 0))
    n_spec = pl.BlockSpec((T, W_B), lambda b, c: (b, 0))
    return pl.pallas_call(
        functools.partial(_fox_decode_kernel, P=P, T=T, pc=pc),
        out_shape=jax.ShapeDtypeStruct((B * T, W_B), BF16),
        grid=(B, P // pc),
        in_specs=[
            pl.BlockSpec((T, 2 * W_B), lambda b, c: (b, 0)),
            c_spec, c_spec, n_spec, n_spec,
            pl.BlockSpec((1, P + T, LANES), lambda b, c: (b, 0, 0)),
            pl.BlockSpec((1, P * DEC_GROUP, LANES), lambda b, c: (b, 0, 0)),
        ],
        out_specs=n_spec,
        scratch_shapes=[
            pltpu.VMEM((N_HEADS * T, 1), F32),
            pltpu.VMEM((N_HEADS * T, 1), F32),
            pltpu.VMEM((N_HEADS * T, HEAD_DIM), F32),
        ],
        compiler_params=_params("parallel", "arbitrary"),
        name=name,
    )(zb, cache_k, cache_v, k16, v16, fa, fa_rep)


def _trunk(x3, pos0, pool_prev, past, wts, tag):
    B, T, D = x3.shape
    M = B * T
    x = x3.reshape(M, D)
    prompt = pool_prev is None
    tm = 1024 if prompt else M
    tm_res = 1024 if prompt else M
    tm_kv = 512 if prompt else M
    tn_in = 2048
    tn_out = 1024 if prompt else 2048
    E = wts["w_in_a0"].shape[2] // 2
    q_scale = HEAD_DIM ** -0.5 * LOG2E

    new_pool = []
    for l in range(wts["norm_a"].shape[0]):
        jobs = wts["to_cast"].pop(l, ()) if prompt else ()
        z, casts = _norm_proj(x, wts["norm_a"][l], wts[f"w_in_a{l}"], 0, tm=tm, tn=tn_in, name=f"in_a{l}_{tag}",
                              side=[(w.reshape(-1, w.shape[-1]), row0) for _, w, row0, _ in jobs])
        for (key, _, _, shape), c in zip(jobs, casts):
            wts[key] = c.reshape(shape)
        u3 = z[:, :E].reshape(B, T, E) if not prompt else None
        if prompt:
            t = _pool_prompt(z, wts["w_grp_a"], l, wts["scale_a"][l], T=T, tm=POOL_TILE, name=f"pool{l}_{tag}")
            new_pool.append(z.reshape(B, T, 2 * E)[:, T - POOL_PAD:, :E].astype(F32))
        else:
            hist = jnp.pad(pool_prev[l].astype(F32), ((0, 0), (HALO - POOL_PAD, 0), (0, 0)))
            full = jnp.concatenate([hist, u3.astype(F32)], axis=1)
            t = _pool_sample(full.reshape(B * (HALO + T), E), z, wts["w_grp_a"], l, wts["scale_a"][l],
                             nseq=B, T=T, pos0=pos0, name=f"pool{l}_{tag}")
            new_pool.append(full[:, HALO + T - POOL_PAD:, :])
        x = _proj_res(t, wts["w_out_a"], l, x, tm=tm_res, tn=tn_out, name=f"out_a{l}_{tag}")

    k32, v32, k16, v16, logf_rep = _kv_proj(x, wts["norm_kv"], wts["w_kv"], wts["w_f_rep"],
                                            wts["b_f_rep"], tm=tm_kv, name=f"kv_{tag}")
    logf3 = logf_rep[:, :N_HEADS].reshape(B, T, N_HEADS)
    logf_rep = logf_rep.reshape(B, T, LANES)
    if not prompt:
        past_k, past_v, past_logf = past
        P = past_k.shape[1]
        past_rep = jnp.pad(jnp.tile(past_logf.astype(F32), (1, 1, 3)), ((0, 0), (0, 0), (0, LANES - 3 * N_HEADS)))
        logf_rep = jnp.concatenate([past_rep, logf_rep], axis=1)
        ck = past_k.reshape(B, P * N_HEADS, HEAD_DIM)
        cv = past_v.reshape(B, P * N_HEADS, HEAD_DIM)
    fa = _forget_cols(logf_rep, name=f"fcols_{tag}")
    if not prompt:
        fa_rep = jnp.repeat(fa[:, :P], DEC_GROUP, axis=1)

    n_b = wts["w_in_b"].shape[0]
    for l in range(n_b):
        zb, extra = _norm_proj(x, wts["norm_b"][l], wts["w_in_b"], l, tm=tm, tn=tn_in,
                               n_scaled=W_B // tn_in, scale=q_scale, qt_tile=ATTN_TILE if prompt else 0,
                               name=f"in_b{l}_{tag}")
        if prompt:
            q_t = extra[-1].reshape(B, T // ATTN_TILE, N_HEADS, HEAD_DIM, ATTN_TILE)
            og = _fox_stream(q_t, zb.reshape(B, T, 2 * W_B), k16.reshape(B, T, W_B), v16.reshape(B, T, W_B), fa,
                             B=B, T=T, tq=ATTN_TILE, hps=ATTN_HEADS_PER_STEP,
                             name=f"attn{l}_{tag}").reshape(M, W_B)
        else:
            og = _fox_decode(zb, k16, v16, ck, cv, fa, fa_rep, B=B, T=T, P=P, pc=DEC_CHUNK,
                             name=f"attn{l}_{tag}")
        if l + 1 < n_b:
            x = _proj_res(og, wts["w_out_b"], l, x, tm=tm_res, tn=tn_out, name=f"out_b{l}_{tag}")
        else:
            y = _proj_res_norm(og, wts["w_out_b"], l, x, wts["norm_f"], tm=min(tm_res, 512),
                               name=f"out_b{l}_{tag}")
    return (y.reshape(B, T, D), k32.reshape(B, T, N_HEADS, HEAD_DIM), v32.reshape(B, T, N_HEADS, HEAD_DIM),
            logf3, jnp.stack(new_pool))


def kernel(x_prompt, x_sample, cache_k, cache_v, cache_logf, state_pool, norm_a, w_in_a, w_grp_a, scale_a,
           w_out_a, norm_kv, w_kv, b_f, norm_b, w_in_b, w_out_b, norm_f):
    assert w_in_a.shape[0] == 2
    wts = dict(
        norm_a=norm_a, w_in_a0=w_in_a[:1].astype(BF16), scale_a=scale_a, norm_kv=norm_kv,
        w_kv=w_kv.astype(BF16),
        w_f_rep=jnp.pad(jnp.tile(w_kv[:, 2 * W_B:], (1, 3)), ((0, 0), (0, LANES - 3 * N_HEADS))).astype(BF16),
        b_f_rep=jnp.pad(jnp.tile(b_f, 3), (0, LANES - 3 * N_HEADS)),
        norm_b=norm_b, norm_f=norm_f,
        to_cast={0: (("w_in_a1", w_in_a, w_in_a.shape[1], (1,) + w_in_a.shape[1:]),
                     ("w_out_a", w_out_a, 0, w_out_a.shape), ("w_grp_a", w_grp_a, 0, w_grp_a.shape)),
                 1: (("w_in_b", w_in_b, 0, w_in_b.shape), ("w_out_b", w_out_b, 0, w_out_b.shape))},
    )
    y_p, k_p, v_p, lf_p, pool_p = _trunk(x_prompt, 0, None, None, wts, "p")
    y_s, k_s, v_s, lf_s, pool_s = _trunk(x_sample, cache_k.shape[1], state_pool,
                                         (cache_k, cache_v, cache_logf), wts, "s")
    return (y_p, y_s, k_p, v_p, lf_p, pool_p, k_s, v_s, lf_s, pool_s)
```
